```python
import math
import jax, jax.numpy as jnp
from jax import lax
import numpy as np

D_MODEL = 2048
BATCH = 2
SEQ = 8192
DEPTH = 4

N_MIXERS = 3
RMS_EPS = 1e-6

ATTN_HEAD_DIM = 128
ATTN_HEADS_PER_GROUP = 8
DILATION_GROUPS = ((128, 1), (512, 4), (2048, 16))
N_ATTN_GROUPS = len(DILATION_GROUPS)
ATTN_BLOCK = 128
ROPE_THETA = 500000.0
ROPE_DIM = ATTN_HEAD_DIM // 4
ATTN_IN_WIDTH = N_ATTN_GROUPS * 3 * ATTN_HEADS_PER_GROUP * ATTN_HEAD_DIM
ATTN_OUT_WIDTH = ATTN_HEADS_PER_GROUP * ATTN_HEAD_DIM

S5_GROUP = 16
S5_GROUPS = D_MODEL // S5_GROUP
S5_STATE = 64
S5_DT_MIN = 1e-3
S5_DT_MAX = 1e-1

RWKV_HEAD = 64
RWKV_HEADS = D_MODEL // RWKV_HEAD
RWKV_DECAY_LORA = max(32, int(round(1.8 * D_MODEL ** 0.5 / 32)) * 32)
RWKV_AAA_LORA = max(32, int(round(1.8 * D_MODEL ** 0.5 / 32)) * 32)
RWKV_GATE_LORA = max(32, int(round(0.6 * D_MODEL ** 0.8 / 32)) * 32)
RWKV_GN_EPS = 64e-5

FFN_HIDDEN = ((8 * D_MODEL + 3 * 256 - 1) // (3 * 256)) * 256

kernel_name = "hybrid_dilated_attn_s5_rwkv7_trunk"


def rms_norm(x, g):
    xf = x.astype(jnp.float32)
    y = xf * lax.rsqrt(jnp.mean(xf * xf, axis=-1, keepdims=True) + RMS_EPS)
    return (y * g.astype(jnp.float32)).astype(x.dtype)


def swiglu_ffn(h, w_in, w_out):
    gate, up = jnp.split(h @ w_in, 2, axis=-1)
    return (jax.nn.silu(gate) * up) @ w_out


def partial_rope(t, positions):
    half = ROPE_DIM // 2
    inv_freq = ROPE_THETA ** (-jnp.arange(half, dtype=jnp.float32) * 2.0 / ROPE_DIM)
    ang = positions.astype(jnp.float32)[..., None] * inv_freq
    ang = ang.reshape(ang.shape[:2] + (1,) * (t.ndim - 3) + (half,))
    cos, sin = jnp.cos(ang), jnp.sin(ang)
    x1, x2, rest = t[..., :half], t[..., half:ROPE_DIM], t[..., ROPE_DIM:]
    return jnp.concatenate([x1 * cos - x2 * sin, x2 * cos + x1 * sin, rest], axis=-1)


def dilated_window_attention(q, k, v, window, dilation):
    B, S, H, E = q.shape
    span = window // dilation
    L = S // dilation
    nb = -(-L // ATTN_BLOCK)
    Lp = nb * ATTN_BLOCK

    def to_blocks(t):
        t = t.reshape(B, L, dilation, H, E)
        t = jnp.pad(t, ((0, 0), (0, Lp - L), (0, 0), (0, 0), (0, 0)))
        return t.reshape(B, nb, ATTN_BLOCK, dilation, H, E)

    def with_prev(t):
        prev = jnp.concatenate([jnp.zeros_like(t[:, :1]), t[:, :-1]], axis=1)
        return jnp.concatenate([prev, t], axis=2)

    qb = to_blocks(q)
    kc = with_prev(to_blocks(k))
    vc = with_prev(to_blocks(v))
    s = jnp.einsum('bnqrhe,bnkrhe->bnrhqk', qb, kc) * (E ** -0.5)
    qi = jnp.arange(ATTN_BLOCK)[:, None]
    kj = jnp.arange(2 * ATTN_BLOCK)[None, :]
    dist = qi + ATTN_BLOCK - kj
    band = (dist >= 0) & (dist <= span)
    first = (jnp.arange(nb) == 0)[:, None, None]
    valid = band[None] & ~(first & (kj < ATTN_BLOCK)[None])
    s = jnp.where(valid[None, :, None, None], s, -jnp.inf)
    mx = jnp.max(s, axis=-1, keepdims=True)
    p = jnp.exp(s - mx)
    den = jnp.sum(p, axis=-1)
    num = jnp.einsum('bnrhqk,bnkrhe->bnqrhe', p, vc)
    num = num.reshape(B, Lp, dilation, H, E)[:, :L].reshape(B, S, H, E)

    def rows_back(t):
        t = jnp.transpose(t, (0, 1, 4, 2, 3)).reshape(B, Lp, dilation, H)
        return t[:, :L].reshape(B, S, H)

    return num, rows_back(den), rows_back(mx[..., 0])


def dilated_attention_mixer(h, positions, w_in, w_out):
    B, S, _ = h.shape
    qkv = (h @ w_in).astype(jnp.float32).reshape(
        B, S, N_ATTN_GROUPS, 3, ATTN_HEADS_PER_GROUP, ATTN_HEAD_DIM)
    q = partial_rope(qkv[:, :, :, 0], positions)
    k = partial_rope(qkv[:, :, :, 1], positions)
    v = qkv[:, :, :, 2]
    parts = [dilated_window_attention(q[:, :, gi], k[:, :, gi], v[:, :, gi], win, dil)
             for gi, (win, dil) in enumerate(DILATION_GROUPS)]
    num = jnp.stack([pt[0] for pt in parts])
    den = jnp.stack([pt[1] for pt in parts])
    mx = jnp.stack([pt[2] for pt in parts])
    wgt = jnp.exp(mx - jnp.max(mx, axis=0))
    o = jnp.sum(wgt[..., None] * num, axis=0) / jnp.sum(wgt * den, axis=0)[..., None]
    return o.reshape(B, S, ATTN_OUT_WIDTH).astype(h.dtype) @ w_out


def _complex_affine_combine(e1, e2):
    a1r, a1i, b1r, b1i = e1
    a2r, a2i, b2r, b2i = e2
    return (a2r * a1r - a2i * a1i,
            a2r * a1i + a2i * a1r,
            a2r * b1r - a2i * b1i + b2r,
            a2r * b1i + a2i * b1r + b2i)


def s5_mixer(h, a_re, a_im, log_dt, b_re, b_im, c_re, c_im, d_skip, w_glu):
    B, S, D = h.shape
    f32 = jnp.float32
    a_re, a_im, log_dt, b_re, b_im, c_re, c_im, d_skip = (
        t.astype(f32) for t in (a_re, a_im, log_dt, b_re, b_im, c_re, c_im, d_skip))
    u = h.astype(f32).reshape(B, S, S5_GROUPS, S5_GROUP)
    dt = jnp.exp(log_dt)[:, None]
    mag = jnp.exp(dt * a_re)
    ab_re = mag * jnp.cos(dt * a_im)
    ab_im = mag * jnp.sin(dt * a_im)
    inv = 1.0 / (a_re * a_re + a_im * a_im)
    f_re = ((ab_re - 1.0) * a_re + ab_im * a_im) * inv
    f_im = (ab_im * a_re - (ab_re - 1.0) * a_im) * inv
    bb_re = f_re[..., None] * b_re - f_im[..., None] * b_im
    bb_im = f_re[..., None] * b_im + f_im[..., None] * b_re
    bu_re = jnp.einsum('bsgc,gpc->bsgp', u, bb_re)
    bu_im = jnp.einsum('bsgc,gpc->bsgp', u, bb_im)
    shape_a = (1, S, S5_GROUPS, S5_STATE)
    elems = (jnp.broadcast_to(ab_re, shape_a), jnp.broadcast_to(ab_im, shape_a), bu_re, bu_im)
    _, _, s_re, s_im = lax.associative_scan(_complex_affine_combine, elems, axis=1)
    y = jnp.einsum('bsgp,gcp->bsgc', s_re, c_re) - jnp.einsum('bsgp,gcp->bsgc', s_im, c_im)
    y = (y + d_skip.reshape(S5_GROUPS, S5_GROUP) * u).reshape(B, S, D)
    z = jax.nn.gelu(y).astype(h.dtype)
    val, gate = jnp.split(z @ w_glu, 2, axis=-1)
    return val * jax.nn.sigmoid(gate)


def _rwkv7_step(state, inp):
    r_t, w_t, k_t, v_t, kk_t, b_t = inp
    sa = jnp.einsum('bhij,bhj->bhi', state, kk_t)
    state = (state * w_t[:, :, None, :] - sa[..., None] * b_t[:, :, None, :]
             + v_t[..., None] * k_t[:, :, None, :])
    return state, jnp.einsum('bhij,bhj->bhi', state, r_t)


def rwkv7_mixer(h, mu, w_r, w_k, w_v, w0, w_w1, w_w2, a0, a_w1, a_w2,
                g_w1, g_w2, k_k, k_a, r_k, ln_w, ln_b, w_o):
    B, S, D = h.shape
    f32 = jnp.float32
    (mu, w_r, w_k, w_v, w0, w_w1, w_w2, a0, a_w1, a_w2, g_w1, g_w2,
     k_k, k_a, r_k, ln_w, ln_b, w_o) = (t.astype(f32) for t in (
        mu, w_r, w_k, w_v, w0, w_w1, w_w2, a0, a_w1, a_w2, g_w1, g_w2,
        k_k, k_a, r_k, ln_w, ln_b, w_o))
    hf = h.astype(f32)
    xx = jnp.pad(hf, ((0, 0), (1, 0), (0, 0)))[:, :-1] - hf
    xr, xw, xk, xv, xa, xg = (hf + xx * mu[i] for i in range(6))
    r = xr @ w_r
    k = xk @ w_k
    v = xv @ w_v
    w_log = -jax.nn.softplus(-(w0 + jnp.tanh(xw @ w_w1) @ w_w2)) - 0.5
    decay = jnp.exp(-jnp.exp(w_log))
    a = jax.nn.sigmoid(a0 + (xa @ a_w1) @ a_w2)
    g = jax.nn.sigmoid(xg @ g_w1) @ g_w2

    def heads(t):
        return t.reshape(B, S, RWKV_HEADS, RWKV_HEAD)

    kk = heads(k * k_k)
    kk = kk * lax.rsqrt(jnp.maximum(jnp.sum(kk * kk, axis=-1, keepdims=True), 1e-24))
    k = k * (1.0 + (a - 1.0) * k_a)
    r_h, k_h, v_h, w_h, a_h = heads(r), heads(k), heads(v), heads(decay), heads(a)
    xs = tuple(jnp.moveaxis(t, 1, 0) for t in (r_h, w_h, k_h, v_h, kk, kk * a_h))
    state0 = jnp.zeros((B, RWKV_HEADS, RWKV_HEAD, RWKV_HEAD), f32)
    _, y = lax.scan(_rwkv7_step, state0, xs)
    y = jnp.moveaxis(y, 0, 1)
    mean = jnp.mean(y, axis=-1, keepdims=True)
    var = jnp.mean((y - mean) ** 2, axis=-1, keepdims=True)
    y = ((y - mean) * lax.rsqrt(var + RWKV_GN_EPS)).reshape(B, S, D) * ln_w + ln_b
    bonus = jnp.sum(r_h * k_h * r_k, axis=-1, keepdims=True) * v_h
    y = y + bonus.reshape(B, S, D)
    return ((y * g) @ w_o).astype(h.dtype)


def setup_inputs(seed: int = 0) -> dict:
    key = jax.random.key(seed)
    ks = iter(jax.random.split(key, 64))

    def nrm(shape, scale):
        return jax.random.normal(next(ks), shape, jnp.float32) * scale

    def gain():
        return 1.0 + nrm((D_MODEL,), 0.02)

    inp = {}
    inp["x"] = nrm((BATCH, SEQ, D_MODEL), 1.0)
    inp["positions"] = jnp.broadcast_to(jnp.arange(SEQ, dtype=jnp.int32), (BATCH, SEQ))

    def add_ffn(p):
        inp[p + "norm_ffn"] = gain()
        inp[p + "ffn_w_in"] = nrm((D_MODEL, 2 * FFN_HIDDEN), D_MODEL ** -0.5)
        inp[p + "ffn_w_out"] = nrm((FFN_HIDDEN, D_MODEL), FFN_HIDDEN ** -0.5)

    def add_attn(p):
        inp[p + "norm_mix"] = gain()
        inp[p + "attn_w_in"] = nrm((D_MODEL, ATTN_IN_WIDTH), D_MODEL ** -0.5)
        inp[p + "attn_w_out"] = nrm((ATTN_OUT_WIDTH, D_MODEL), ATTN_OUT_WIDTH ** -0.5)
        add_ffn(p)

    add_attn("l0_")
    inp["l1_norm_mix"] = gain()
    inp["l1_s5_a_re"] = -0.5 + nrm((S5_GROUPS, S5_STATE), 0.01)
    inp["l1_s5_a_im"] = math.pi * jnp.arange(S5_STATE, dtype=jnp.float32)[None, :] + nrm((S5_GROUPS, S5_STATE), 0.01)
    inp["l1_s5_log_dt"] = jax.random.uniform(next(ks), (S5_GROUPS,), jnp.float32,
                                             math.log(S5_DT_MIN), math.log(S5_DT_MAX))
    inp["l1_s5_b_re"] = nrm((S5_GROUPS, S5_STATE, S5_GROUP), (2 * S5_GROUP) ** -0.5)
    inp["l1_s5_b_im"] = nrm((S5_GROUPS, S5_STATE, S5_GROUP), (2 * S5_GROUP) ** -0.5)
    inp["l1_s5_c_re"] = nrm((S5_GROUPS, S5_GROUP, S5_STATE), (2 * S5_STATE) ** -0.5)
    inp["l1_s5_c_im"] = nrm((S5_GROUPS, S5_GROUP, S5_STATE), (2 * S5_STATE) ** -0.5)
    inp["l1_s5_d"] = nrm((D_MODEL,), 1.0)
    inp["l1_s5_w_glu"] = nrm((D_MODEL, 2 * D_MODEL), D_MODEL ** -0.5)
    add_ffn("l1_")
    inp["l2_norm_mix"] = gain()
    inp["l2_rwkv_mu"] = jax.random.uniform(next(ks), (6, D_MODEL), jnp.float32)
    inp["l2_rwkv_w_r"] = nrm((D_MODEL, D_MODEL), D_MODEL ** -0.5)
    inp["l2_rwkv_w_k"] = nrm((D_MODEL, D_MODEL), D_MODEL ** -0.5)
    inp["l2_rwkv_w_v"] = nrm((D_MODEL, D_MODEL), D_MODEL ** -0.5)
    inp["l2_rwkv_w0"] = jax.random.uniform(next(ks), (D_MODEL,), jnp.float32, -6.0, -1.0)
    inp["l2_rwkv_w_w1"] = nrm((D_MODEL, RWKV_DECAY_LORA), D_MODEL ** -0.5)
    inp["l2_rwkv_w_w2"] = nrm((RWKV_DECAY_LORA, D_MODEL), 0.5 * RWKV_DECAY_LORA ** -0.5)
    inp["l2_rwkv_a0"] = nrm((D_MODEL,), 0.1)
    inp["l2_rwkv_a_w1"] = nrm((D_MODEL, RWKV_AAA_LORA), D_MODEL ** -0.5)
    inp["l2_rwkv_a_w2"] = nrm((RWKV_AAA_LORA, D_MODEL), 0.5 * RWKV_AAA_LORA ** -0.5)
    inp["l2_rwkv_g_w1"] = nrm((D_MODEL, RWKV_GATE_LORA), D_MODEL ** -0.5)
    inp["l2_rwkv_g_w2"] = nrm((RWKV_GATE_LORA, D_MODEL), RWKV_GATE_LORA ** -0.5)
    inp["l2_rwkv_k_k"] = 0.85 + nrm((D_MODEL,), 0.02)
    inp["l2_rwkv_k_a"] = 1.0 + nrm((D_MODEL,), 0.02)
    inp["l2_rwkv_r_k"] = nrm((RWKV_HEADS, RWKV_HEAD), 0.1)
    inp["l2_rwkv_ln_w"] = gain()
    inp["l2_rwkv_ln_b"] = nrm((D_MODEL,), 0.02)
    inp["l2_rwkv_w_o"] = nrm((D_MODEL, D_MODEL), D_MODEL ** -0.5)
    add_ffn("l2_")
    add_attn("l3_")
    inp["final_norm"] = gain()
    return inp


def reference(x, positions,
              l0_norm_mix, l0_attn_w_in, l0_attn_w_out, l0_norm_ffn, l0_ffn_w_in, l0_ffn_w_out,
              l1_norm_mix, l1_s5_a_re, l1_s5_a_im, l1_s5_log_dt, l1_s5_b_re, l1_s5_b_im,
              l1_s5_c_re, l1_s5_c_im, l1_s5_d, l1_s5_w_glu, l1_norm_ffn, l1_ffn_w_in, l1_ffn_w_out,
              l2_norm_mix, l2_rwkv_mu, l2_rwkv_w_r, l2_rwkv_w_k, l2_rwkv_w_v, l2_rwkv_w0,
              l2_rwkv_w_w1, l2_rwkv_w_w2, l2_rwkv_a0, l2_rwkv_a_w1, l2_rwkv_a_w2,
              l2_rwkv_g_w1, l2_rwkv_g_w2, l2_rwkv_k_k, l2_rwkv_k_a, l2_rwkv_r_k,
              l2_rwkv_ln_w, l2_rwkv_ln_b, l2_rwkv_w_o, l2_norm_ffn, l2_ffn_w_in, l2_ffn_w_out,
              l3_norm_mix, l3_attn_w_in, l3_attn_w_out, l3_norm_ffn, l3_ffn_w_in, l3_ffn_w_out,
              final_norm):
    layers = (
        (l0_norm_mix, (l0_attn_w_in, l0_attn_w_out), l0_norm_ffn, l0_ffn_w_in, l0_ffn_w_out),
        (l1_norm_mix, (l1_s5_a_re, l1_s5_a_im, l1_s5_log_dt, l1_s5_b_re, l1_s5_b_im,
                       l1_s5_c_re, l1_s5_c_im, l1_s5_d, l1_s5_w_glu),
         l1_norm_ffn, l1_ffn_w_in, l1_ffn_w_out),
        (l2_norm_mix, (l2_rwkv_mu, l2_rwkv_w_r, l2_rwkv_w_k, l2_rwkv_w_v, l2_rwkv_w0,
                       l2_rwkv_w_w1, l2_rwkv_w_w2, l2_rwkv_a0, l2_rwkv_a_w1, l2_rwkv_a_w2,
                       l2_rwkv_g_w1, l2_rwkv_g_w2, l2_rwkv_k_k, l2_rwkv_k_a, l2_rwkv_r_k,
                       l2_rwkv_ln_w, l2_rwkv_ln_b, l2_rwkv_w_o),
         l2_norm_ffn, l2_ffn_w_in, l2_ffn_w_out),
        (l3_norm_mix, (l3_attn_w_in, l3_attn_w_out), l3_norm_ffn, l3_ffn_w_in, l3_ffn_w_out),
    )
    for i in range(DEPTH):
        norm_mix, mix_params, norm_ffn, ffn_w_in, ffn_w_out = layers[i]
        h = rms_norm(x, norm_mix)
        kind = i % N_MIXERS
        if kind == 0:
            y = dilated_attention_mixer(h, positions, *mix_params)
        elif kind == 1:
            y = s5_mixer(h, *mix_params)
        else:
            y = rwkv7_mixer(h, *mix_params)
        x = x + y.astype(x.dtype)
        x = x + swiglu_ffn(rms_norm(x, norm_ffn), ffn_w_in, ffn_w_out).astype(x.dtype)
    return rms_norm(x, final_norm)
```

```python
import functools
import math

import jax
import jax.numpy as jnp
from jax import lax
from jax.experimental import pallas as pl
from jax.experimental.pallas import tpu as pltpu

F32 = jnp.float32
BF16 = jnp.bfloat16

RMS_EPS = 1e-6
VMEM_LIMIT_BYTES = 56 * 1024 * 1024

ATTN_HEAD_DIM = 128
ATTN_HEADS = 8
ATTN_GROUPS = ((128, 1), (512, 4), (2048, 16))
ATTN_BLOCK = 128
ROPE_THETA = 500000.0
ROPE_DIM = ATTN_HEAD_DIM // 4
ATTN_WIDTH = ATTN_HEADS * ATTN_HEAD_DIM
NEG_BIG = -1e30

S5_GROUP = 16
S5_STATE = 64
S5_CHUNK = 16
S5_GROUPS_PER_STEP = 4

RWKV_HEAD = 64
RWKV_CHUNK = 64
RWKV_HEADS_PER_UNIT = 4
RWKV_UNIT = RWKV_HEADS_PER_UNIT * RWKV_HEAD
RWKV_GN_EPS = 64e-5


def _params(*sem):
    return pltpu.CompilerParams(dimension_semantics=sem, vmem_limit_bytes=VMEM_LIMIT_BYTES)


def _tile(n, pref):
    t = min(n, pref)
    assert n % t == 0, (n, pref)
    return t


def _rms(x, g):
    return x * lax.rsqrt(jnp.mean(x * x, axis=-1, keepdims=True) + RMS_EPS) * g


def _nt(a, b):
    return lax.dot_general(a, b, (((1,), (1,)), ((), ())), preferred_element_type=F32)


def _tn(a, b):
    return lax.dot_general(a, b, (((0,), (0,)), ((), ())), preferred_element_type=F32)


def _dot(a, b):
    return jnp.dot(a, b, preferred_element_type=F32)


def _rmsnorm_kernel(x_ref, g_ref, o_ref):
    o_ref[...] = _rms(x_ref[...], g_ref[...]).astype(o_ref.dtype)


def rmsnorm(x, gain, out_dtype=F32, tm=512):
    t, d = x.shape
    tm = _tile(t, tm)
    return pl.pallas_call(
        _rmsnorm_kernel,
        grid=(t // tm,),
        in_specs=[pl.BlockSpec((tm, d), lambda i: (i, 0)), pl.BlockSpec((1, d), lambda i: (0, 0))],
        out_specs=pl.BlockSpec((tm, d), lambda i: (i, 0)),
        out_shape=jax.ShapeDtypeStruct((t, d), out_dtype),
        compiler_params=_params("parallel"),
        name="rmsnorm",
    )(x, gain.reshape(1, d))


def _mm_kernel(*refs, has_bias, act, has_res):
    it = iter(refs)
    x_ref = next(it)
    w_ref = next(it)
    b_ref = next(it) if has_bias else None
    r_ref = next(it) if has_res else None
    o_ref = next(it)
    acc = _dot(x_ref[...].astype(BF16), w_ref[...])
    if has_bias:
        acc = acc + b_ref[...]
    if act == "tanh":
        acc = jnp.tanh(acc)
    elif act == "sigmoid":
        acc = jax.nn.sigmoid(acc)
    if has_res:
        acc = acc + r_ref[...]
    o_ref[...] = acc.astype(o_ref.dtype)


def matmul(x, w, *, bias=None, act=None, residual=None, out_dtype=F32, tm=1024, tn=1024):
    t, k = x.shape
    n = w.shape[1]
    tm, tn = _tile(t, tm), _tile(n, tn)
    ins = [x, w]
    specs = [pl.BlockSpec((tm, k), lambda i, j: (i, 0)), pl.BlockSpec((k, tn), lambda i, j: (0, j))]
    if bias is not None:
        ins.append(bias.reshape(1, n).astype(F32))
        specs.append(pl.BlockSpec((1, tn), lambda i, j: (0, j)))
    if residual is not None:
        ins.append(residual)
        specs.append(pl.BlockSpec((tm, tn), lambda i, j: (i, j)))
    return pl.pallas_call(
        functools.partial(_mm_kernel, has_bias=bias is not None, act=act, has_res=residual is not None),
        grid=(t // tm, n // tn),
        in_specs=specs,
        out_specs=pl.BlockSpec((tm, tn), lambda i, j: (i, j)),
        out_shape=jax.ShapeDtypeStruct((t, n), out_dtype),
        compiler_params=_params("parallel", "parallel"),
        name="matmul",
    )(*ins)


def _ffn_kernel(x_ref, g_ref, wg_ref, wu_ref, wo_ref, fg_ref, o_ref, xn_ref, *, final_norm):
    j = pl.program_id(1)

    @pl.when(j == 0)
    def _():
        x = x_ref[...]
        xn_ref[...] = _rms(x, g_ref[...]).astype(BF16)
        o_ref[...] = x

    xn = xn_ref[...]
    gate = _dot(xn, wg_ref[...])
    up = _dot(xn, wu_ref[...])
    h = (gate * jax.nn.sigmoid(gate) * up).astype(BF16)
    o_ref[...] += _dot(h, wo_ref[...])

    if final_norm:
        @pl.when(j == pl.num_programs(1) - 1)
        def _():
            o_ref[...] = _rms(o_ref[...], fg_ref[...])


def ffn(x, gain, w_in, w_out, final_gain=None, tm=512, tf=512):
    t, d = x.shape
    f = w_out.shape[0]
    tm, tf = _tile(t, tm), _tile(f, tf)
    nf = f // tf
    fg = gain if final_gain is None else final_gain
    return pl.pallas_call(
        functools.partial(_ffn_kernel, final_norm=final_gain is not None),
        grid=(t // tm, nf),
        in_specs=[
            pl.BlockSpec((tm, d), lambda i, j: (i, 0)),
            pl.BlockSpec((1, d), lambda i, j: (0, 0)),
            pl.BlockSpec((d, tf), lambda i, j: (0, j)),
            pl.BlockSpec((d, tf), lambda i, j: (0, j + nf)),
            pl.BlockSpec((tf, d), lambda i, j: (j, 0)),
            pl.BlockSpec((1, d), lambda i, j: (0, 0)),
        ],
        out_specs=pl.BlockSpec((tm, d), lambda i, j: (i, 0)),
        out_shape=jax.ShapeDtypeStruct((t, d), F32),
        scratch_shapes=[pltpu.VMEM((tm, d), BF16)],
        compiler_params=_params("parallel", "arbitrary"),
        name="ffn",
    )(x, gain.reshape(1, d), w_in, w_in, w_out, fg.reshape(1, d))


def _qkv_kernel(x_ref, g_ref, w_ref, cos_ref, sin_ref, o_ref, xn_ref):
    j = pl.program_id(1)

    @pl.when(j == 0)
    def _():
        xn_ref[...] = _rms(x_ref[...], g_ref[...]).astype(BF16)

    acc = _dot(xn_ref[...], w_ref[...])
    kind = j % 3

    @pl.when(kind == 2)
    def _():
        o_ref[...] = acc.astype(o_ref.dtype)

    @pl.when(kind != 2)
    def _():
        scale = jnp.where(kind == 0, ATTN_HEAD_DIM ** -0.5, 1.0).astype(F32)
        cos = cos_ref[...] * scale
        sin = sin_ref[...] * scale
        lane = lax.broadcasted_iota(jnp.int32, cos.shape, 1)
        half = ROPE_DIM // 2
        for h in range(ATTN_HEADS):
            sl = slice(h * ATTN_HEAD_DIM, (h + 1) * ATTN_HEAD_DIM)
            xh = acc[:, sl]
            rot = jnp.where(lane < half, pltpu.roll(xh, ATTN_HEAD_DIM - half, 1), pltpu.roll(xh, half, 1))
            o_ref[:, sl] = (xh * cos + rot * sin).astype(o_ref.dtype)


def qkv_projection(x, gain, w_in, cos_t, sin_t, tm=1024):
    t, d = x.shape
    n = w_in.shape[1]
    tm = _tile(t, tm)
    tn = ATTN_WIDTH
    return pl.pallas_call(
        _qkv_kernel,
        grid=(t // tm, n // tn),
        in_specs=[
            pl.BlockSpec((tm, d), lambda i, j: (i, 0)),
            pl.BlockSpec((1, d), lambda i, j: (0, 0)),
            pl.BlockSpec((d, tn), lambda i, j: (0, j)),
            pl.BlockSpec((tm, ATTN_HEAD_DIM), lambda i, j: (i, 0)),
            pl.BlockSpec((tm, ATTN_HEAD_DIM), lambda i, j: (i, 0)),
        ],
        out_specs=pl.BlockSpec((tm, tn), lambda i, j: (i, j)),
        out_shape=jax.ShapeDtypeStruct((t, n), BF16),
        scratch_shapes=[pltpu.VMEM((tm, d), BF16)],
        compiler_params=_params("parallel", "arbitrary"),
        name="qkv_rope",
    )(x, gain.reshape(1, d), w_in, cos_t, sin_t)


def _attn_kernel(q_ref, kp_ref, kc_ref, vp_ref, vc_ref, o_ref, st_ref):
    n = pl.program_id(2)
    blk = ATTN_BLOCK
    qi = lax.broadcasted_iota(jnp.int32, (blk, 2 * blk), 0)
    kj = lax.broadcasted_iota(jnp.int32, (blk, 2 * blk), 1)
    dist = qi + blk - kj
    valid = (dist >= 0) & (dist <= blk) & ((kj >= blk) | (n > 0))
    lane = lax.broadcasted_iota(jnp.int32, (blk, ATTN_HEAD_DIM), 1)
    stats = jnp.zeros((blk, ATTN_HEAD_DIM), F32)
    for h in range(ATTN_HEADS):
        sl = slice(h * ATTN_HEAD_DIM, (h + 1) * ATTN_HEAD_DIM)
        q = q_ref[0, :, sl]
        k = jnp.concatenate([kp_ref[0, :, sl], kc_ref[0, :, sl]], axis=0)
        v = jnp.concatenate([vp_ref[0, :, sl], vc_ref[0, :, sl]], axis=0)
        s = jnp.where(valid, _nt(q, k), NEG_BIG)
        mx = jnp.max(s, axis=-1, keepdims=True)
        p = jnp.exp(s - mx)
        den = jnp.sum(p, axis=-1, keepdims=True)
        num = _dot(p.astype(BF16), v)
        o_ref[0, :, sl] = (num / den).astype(o_ref.dtype)
        stats = jnp.where(lane == h, mx + jnp.log(den), stats)
    st_ref[0] = stats


def attention_group(qkv, gi, dilation):
    b, s, n = qkv.shape
    ncol = n // ATTN_WIDTH
    length = s // dilation
    assert length % ATTN_BLOCK == 0
    nb = length // ATTN_BLOCK
    x = qkv.reshape(b, length, dilation * n)
    blk = (1, ATTN_BLOCK, ATTN_WIDTH)
    base = gi * 3

    def cur(off):
        return pl.BlockSpec(blk, lambda bi, r, nn: (bi, nn, r * ncol + base + off))

    def prev(off):
        return pl.BlockSpec(blk, lambda bi, r, nn: (bi, jnp.maximum(nn - 1, 0), r * ncol + base + off))

    o, st = pl.pallas_call(
        _attn_kernel,
        grid=(b, dilation, nb),
        in_specs=[cur(0), prev(1), cur(1), prev(2), cur(2)],
        out_specs=[
            pl.BlockSpec(blk, lambda bi, r, nn: (bi, nn, r)),
            pl.BlockSpec((1, ATTN_BLOCK, ATTN_HEAD_DIM), lambda bi, r, nn: (bi, nn, r)),
        ],
        out_shape=[
            jax.ShapeDtypeStruct((b, length, dilation * ATTN_WIDTH), BF16),
            jax.ShapeDtypeStruct((b, length, dilation * ATTN_HEAD_DIM), F32),
        ],
        compiler_params=_params("parallel", "parallel", "parallel"),
        name="dilated_attn",
    )(x, x, x, x, x)
    return o.reshape(b * s, ATTN_WIDTH), st.reshape(b * s, ATTN_HEAD_DIM)


def _attn_out_kernel(o0_ref, o1_ref, o2_ref, s0_ref, s1_ref, s2_ref, w_ref, r_ref, out_ref, om_ref):
    @pl.when(pl.program_id(1) == 0)
    def _():
        l0, l1, l2 = s0_ref[...], s1_ref[...], s2_ref[...]
        m = jnp.maximum(jnp.maximum(l0, l1), l2)
        e0, e1, e2 = jnp.exp(l0 - m), jnp.exp(l1 - m), jnp.exp(l2 - m)
        inv = 1.0 / (e0 + e1 + e2)
        w0, w1, w2 = e0 * inv, e1 * inv, e2 * inv
        for h in range(ATTN_HEADS):
            sl = slice(h * ATTN_HEAD_DIM, (h + 1) * ATTN_HEAD_DIM)
            om = (w0[:, h:h + 1] * o0_ref[:, sl].astype(F32)
                  + w1[:, h:h + 1] * o1_ref[:, sl].astype(F32)
                  + w2[:, h:h + 1] * o2_ref[:, sl].astype(F32))
            om_ref[:, sl] = om.astype(BF16)

    out_ref[...] = r_ref[...] + _dot(om_ref[...], w_ref[...])


def attention_out(parts, w_out, residual, tm=512, tn=1024):
    (o0, s0), (o1, s1), (o2, s2) = parts
    t, k = o0.shape
    n = w_out.shape[1]
    tm, tn = _tile(t, tm), _tile(n, tn)
    ospec = pl.BlockSpec((tm, k), lambda i, j: (i, 0))
    sspec = pl.BlockSpec((tm, ATTN_HEAD_DIM), lambda i, j: (i, 0))
    return pl.pallas_call(
        _attn_out_kernel,
        grid=(t // tm, n // tn),
        in_specs=[ospec, ospec, ospec, sspec, sspec, sspec,
                  pl.BlockSpec((k, tn), lambda i, j: (0, j)),
                  pl.BlockSpec((tm, tn), lambda i, j: (i, j))],
        out_specs=pl.BlockSpec((tm, tn), lambda i, j: (i, j)),
        out_shape=jax.ShapeDtypeStruct((t, n), F32),
        scratch_shapes=[pltpu.VMEM((tm, k), BF16)],
        compiler_params=_params("parallel", "arbitrary"),
        name="attn_merge_out",
    )(o0, o1, o2, s0, s1, s2, w_out, residual)


def _rope_tables(positions):
    half = ROPE_DIM // 2
    inv_freq = ROPE_THETA ** (-jnp.arange(half, dtype=F32) * 2.0 / ROPE_DIM)
    ang = positions.astype(F32).reshape(-1, 1) * inv_freq
    cos, sin = jnp.cos(ang), jnp.sin(ang)
    pad = ATTN_HEAD_DIM - ROPE_DIM
    cos_t = jnp.concatenate([cos, cos, jnp.ones((ang.shape[0], pad), F32)], axis=1)
    sin_t = jnp.concatenate([-sin, sin, jnp.zeros((ang.shape[0], pad), F32)], axis=1)
    return cos_t, sin_t


def attention_layer(x, b, s, rope, gain, w_in, w_out):
    qkv = qkv_projection(x, gain, w_in.astype(BF16), *rope)
    qkv = qkv.reshape(b, s, -1)
    parts = [attention_group(qkv, gi, dil) for gi, (_, dil) in enumerate(ATTN_GROUPS)]
    return attention_out(parts, w_out.astype(BF16), x)


def _gelu_tanh(y):
    return 0.5 * y * (1.0 + jnp.tanh(math.sqrt(2.0 / math.pi) * (y + 0.044715 * (y * y * y))))


def _s5_kernel(u_ref, toep_ref, wp_ref, q_ref, c1_ref, c2_ref, z_ref, *, chunks_per_seq):
    groups, nch, _ = u_ref.shape
    row = lax.broadcasted_iota(jnp.int32, (nch, 2 * S5_STATE), 0) % chunks_per_seq
    nsteps = chunks_per_seq.bit_length() - 1
    for g in range(groups):
        u = u_ref[g]
        y = _dot(u, toep_ref[g])
        x = _dot(u, wp_ref[g])
        for i in range(nsteps):
            m = 1 << i
            sh = jnp.where(row >= m, pltpu.roll(x, m, 0), 0.0)
            x = x + c1_ref[g, i:i + 1, :] * sh + c2_ref[g, i:i + 1, :] * pltpu.roll(sh, S5_STATE, 1)
        xprev = jnp.where(row >= 1, pltpu.roll(x, 1, 0), 0.0)
        y = y + _dot(xprev.astype(BF16), q_ref[g])
        z_ref[g] = _gelu_tanh(y).astype(z_ref.dtype)


def _s5_operators(a_re, a_im, log_dt, b_re, b_im, c_re, c_im, d_skip, chunks_per_seq):
    lc = S5_CHUNK
    g, p = a_re.shape
    dt = jnp.exp(log_dt)[:, None]
    mag = jnp.exp(dt * a_re)
    ab_re = mag * jnp.cos(dt * a_im)
    ab_im = mag * jnp.sin(dt * a_im)
    inv = 1.0 / (a_re * a_re + a_im * a_im)
    f_re = ((ab_re - 1.0) * a_re + ab_im * a_im) * inv
    f_im = (ab_im * a_re - (ab_re - 1.0) * a_im) * inv
    bb_re = f_re[..., None] * b_re - f_im[..., None] * b_im
    bb_im = f_re[..., None] * b_im + f_im[..., None] * b_re

    def power(j):
        jf = j.astype(F32)[:, None, None]
        m = jnp.exp(jf * (dt * a_re))
        return m * jnp.cos(jf * (dt * a_im)), m * jnp.sin(jf * (dt * a_im))

    pr, pi = power(jnp.arange(lc + 1))
    ba_re = pr[..., None] * bb_re - pi[..., None] * bb_im
    ba_im = pr[..., None] * bb_im + pi[..., None] * bb_re
    kern = (jnp.einsum("jgpa,gcp->jgac", ba_re, c_re) - jnp.einsum("jgpa,gcp->jgac", ba_im, c_im))
    ti = jnp.arange(lc)
    lag = ti[None, :] - ti[:, None]
    toep = jnp.where((lag >= 0)[None, :, None, :, None],
                     jnp.transpose(kern[jnp.clip(lag, 0, lc)], (2, 0, 3, 1, 4)), 0.0)
    eye = jnp.eye(lc, dtype=F32)[:, None, :, None] * jnp.eye(S5_GROUP, dtype=F32)[None, :, None, :]
    toep = toep + d_skip.reshape(g, 1, S5_GROUP, 1, 1) * eye[None]
    toep = toep.reshape(g, lc * S5_GROUP, lc * S5_GROUP)
    rev = lc - 1 - ti
    wp = jnp.concatenate([ba_re[rev], ba_im[rev]], axis=2)
    wp = jnp.transpose(wp, (1, 0, 3, 2)).reshape(g, lc * S5_GROUP, 2 * p)
    qr, qi = pr[1:], pi[1:]
    q_top = (jnp.einsum("tgp,gcp->gptc", qr, c_re) - jnp.einsum("tgp,gcp->gptc", qi, c_im))
    q_bot = (-jnp.einsum("tgp,gcp->gptc", qi, c_re) - jnp.einsum("tgp,gcp->gptc", qr, c_im))
    q = jnp.concatenate([q_top, q_bot], axis=1).reshape(g, 2 * p, lc * S5_GROUP)
    nsteps = chunks_per_seq.bit_length() - 1
    sr, si = power(lc * (2 ** jnp.arange(nsteps)))
    c1 = jnp.transpose(jnp.concatenate([sr, sr], axis=2), (1, 0, 2))
    c2 = jnp.transpose(jnp.concatenate([-si, si], axis=2), (1, 0, 2))
    return toep.astype(BF16), wp.astype(BF16), q.astype(BF16), c1, c2


def s5_core(u, ops, b, s):
    t, d = u.shape
    g = d // S5_GROUP
    lc = S5_CHUNK
    cps = s // lc
    assert cps & (cps - 1) == 0
    nch = t // lc
    width = lc * S5_GROUP
    toep, wp, q, c1, c2 = ops
    nsteps = c1.shape[1]
    ug = jnp.transpose(u.reshape(nch, lc, g, S5_GROUP), (2, 0, 1, 3)).reshape(g, nch, width)
    gs = _tile(g, S5_GROUPS_PER_STEP)
    z = pl.pallas_call(
        functools.partial(_s5_kernel, chunks_per_seq=cps),
        grid=(g // gs,),
        in_specs=[
            pl.BlockSpec((gs, nch, width), lambda i: (i, 0, 0)),
            pl.BlockSpec((gs, width, width), lambda i: (i, 0, 0)),
            pl.BlockSpec((gs, width, 2 * S5_STATE), lambda i: (i, 0, 0)),
            pl.BlockSpec((gs, 2 * S5_STATE, width), lambda i: (i, 0, 0)),
            pl.BlockSpec((gs, nsteps, 2 * S5_STATE), lambda i: (i, 0, 0)),
            pl.BlockSpec((gs, nsteps, 2 * S5_STATE), lambda i: (i, 0, 0)),
        ],
        out_specs=pl.BlockSpec((gs, nch, width), lambda i: (i, 0, 0)),
        out_shape=jax.ShapeDtypeStruct((g, nch, width), BF16),
        compiler_params=_params("parallel"),
        name="s5_chunk_scan",
    )(ug, toep, wp, q, c1, c2)
    return jnp.transpose(z.reshape(g, nch, lc, S5_GROUP), (1, 2, 0, 3)).reshape(t, d)


def _glu_kernel(z_ref, wv_ref, wg_ref, r_ref, o_ref):
    z = z_ref[...]
    val = _dot(z, wv_ref[...])
    gate = _dot(z, wg_ref[...])
    o_ref[...] = r_ref[...] + val * jax.nn.sigmoid(gate)


def glu_out(z, w_glu, residual, tm=1024, tn=1024):
    t, k = z.shape
    n = w_glu.shape[1] // 2
    tm, tn = _tile(t, tm), _tile(n, tn)
    nn = n // tn
    return pl.pallas_call(
        _glu_kernel,
        grid=(t // tm, nn),
        in_specs=[
            pl.BlockSpec((tm, k), lambda i, j: (i, 0)),
            pl.BlockSpec((k, tn), lambda i, j: (0, j)),
            pl.BlockSpec((k, tn), lambda i, j: (0, j + nn)),
            pl.BlockSpec((tm, tn), lambda i, j: (i, j)),
        ],
        out_specs=pl.BlockSpec((tm, tn), lambda i, j: (i, j)),
        out_shape=jax.ShapeDtypeStruct((t, n), F32),
        compiler_params=_params("parallel", "parallel"),
        name="glu_out",
    )(z, w_glu, w_glu, residual)


def s5_layer(x, b, s, gain, a_re, a_im, log_dt, b_re, b_im, c_re, c_im, d_skip, w_glu):
    u = rmsnorm(x, gain, out_dtype=BF16)
    ops = _s5_operators(a_re, a_im, log_dt, b_re, b_im, c_re, c_im, d_skip, s // S5_CHUNK)
    z = s5_core(u, ops, b, s)
    return glu_out(z, w_glu.astype(BF16), x)


def _split2(x):
    hi = x.astype(BF16)
    lo = (x - hi.astype(F32)).astype(BF16)
    return hi, lo


def _rwkv_kernel(r_ref, k_ref, v_ref, wz_ref, az_ref, g_ref, kk_ref, ka_ref, rk_ref, lnw_ref, lnb_ref,
                 o_ref, state_ref):
    nb, c, width = r_ref.shape
    unit = RWKV_UNIT
    nu = width // unit
    reps = unit // c

    @pl.when(pl.program_id(1) == 0)
    def _():
        state_ref[...] = jnp.zeros_like(state_ref)

    ri = lax.broadcasted_iota(jnp.int32, (unit, unit), 0)
    ci = lax.broadcasted_iota(jnp.int32, (unit, unit), 1)
    head_bd = (ri // RWKV_HEAD) == (ci // RWKV_HEAD)
    stack_bd = (ri // c) == (ci // RWKV_HEAD)
    chunk_bd = (ri // c) == (ci // c)
    ones_bd = head_bd.astype(BF16)
    eye = (ri == ci).astype(F32)
    tr = lax.broadcasted_iota(jnp.int32, (c, unit), 0)
    tc = lax.broadcasted_iota(jnp.int32, (c, unit), 1) % c
    strict = tc < tr
    incl = tc <= tr
    li = lax.broadcasted_iota(jnp.int32, (c, c), 0)
    lj = lax.broadcasted_iota(jnp.int32, (c, c), 1)
    tri = (lj <= li).astype(BF16)

    def headsum(x):
        hi, lo = _split2(x)
        return _dot(hi, ones_bd) + _dot(lo, ones_bd)

    def stack(x, mask):
        return jnp.where(mask, jnp.concatenate([x] * reps, axis=0), 0.0).astype(BF16)

    for b in range(nb):
        for q in range(nu):
            sl = slice(q * unit, (q + 1) * unit)
            r, k, v = r_ref[b, :, sl], k_ref[b, :, sl], v_ref[b, :, sl]
            gate = g_ref[b, :, sl]
            wz, az = wz_ref[b, :, sl], az_ref[b, :, sl]
            k_k, k_a, r_k = kk_ref[:, sl], ka_ref[:, sl], rk_ref[:, sl]

            logw = -jnp.exp(-jax.nn.softplus(-wz) - 0.5)
            a = jax.nn.sigmoid(az)
            kk = k * k_k
            kk = kk * lax.rsqrt(jnp.maximum(headsum(kk * kk), 1e-24))
            k2 = k * (1.0 + (a - 1.0) * k_a)
            bvec = kk * a

            p0 = logw.astype(BF16)
            r1 = logw - p0.astype(F32)
            p1 = r1.astype(BF16)
            p2 = (r1 - p1.astype(F32)).astype(BF16)
            cum = _dot(tri, p0) + _dot(tri, p1) + _dot(tri, p2)
            cend = cum[c - 1:c, :]
            w_in = jnp.exp(cum)
            w_prev = jnp.exp(cum - logw)
            w_inv = jnp.exp(-cum)
            w_rest = jnp.exp(cend - cum)
            w_end = jnp.exp(cend)

            a_t = (-kk * w_prev).astype(BF16)
            r_t = (r * w_in).astype(BF16)
            b_t = bvec * w_inv
            k_t = k2 * w_inv
            bs = stack(b_t, stack_bd)
            ks = stack(k_t, stack_bd)
            vs = stack(v, stack_bd)

            scores = _nt(jnp.concatenate([a_t, r_t], axis=0), jnp.concatenate([bs, ks], axis=0))
            s_ab = jnp.where(strict, scores[:c, :unit], 0.0)
            s_ak = jnp.where(strict, scores[:c, unit:], 0.0).astype(BF16)
            s_rb = jnp.where(incl, scores[c:, :unit], 0.0).astype(BF16)
            s_rk = jnp.where(incl, scores[c:, unit:], 0.0).astype(BF16)

            pw = jnp.where(chunk_bd, jnp.concatenate([s_ab] * reps, axis=0), 0.0)
            tm = eye + pw
            for _ in range(c.bit_length() - 2):
                pwb = pw.astype(BF16)
                pw = _dot(pwb, pwb)
                tm = tm + _dot(tm.astype(BF16), pw.astype(BF16))
            t_cat = tm[0:c]
            for h in range(1, reps):
                t_cat = t_cat + tm[h * c:(h + 1) * c]

            state = state_ref[b, q]
            state_b = state.astype(BF16)
            x = _nt(a_t, state_b) + _dot(s_ak, vs)
            u = _dot(t_cat.astype(BF16), stack(x, stack_bd))
            us = stack(u, stack_bd)
            y = (_nt(r_t, state_b)
                 + _dot(jnp.concatenate([s_rb, s_rk], axis=1), jnp.concatenate([us, vs], axis=0)))
            upd = _tn(jnp.concatenate([u, v], axis=0).astype(BF16),
                      jnp.concatenate([bvec * w_rest, k2 * w_rest], axis=0).astype(BF16))
            state_ref[b, q] = jnp.where(head_bd, state * w_end + upd, 0.0)

            inv_n = 1.0 / RWKV_HEAD
            mean = headsum(y) * inv_n
            yc = y - mean
            var = headsum(yc * yc) * inv_n
            yn = yc * lax.rsqrt(var + RWKV_GN_EPS) * lnw_ref[:, sl] + lnb_ref[:, sl]
            bonus = headsum(r * k2 * r_k) * v
            o_ref[b, :, sl] = ((yn + bonus) * gate).astype(o_ref.dtype)


def rwkv_core(r, k, v, wz, az, gate, k_k, k_a, r_k, ln_w, ln_b, units_per_step=2):
    b, s, d = r.shape
    c = RWKV_CHUNK
    assert s % c == 0 and RWKV_UNIT % c == 0
    width = _tile(d, RWKV_UNIT * units_per_step)
    nu = width // RWKV_UNIT
    act = pl.BlockSpec((b, c, width), lambda i, j: (0, j, i))
    row = pl.BlockSpec((1, width), lambda i, j: (0, i))
    rows = [t.reshape(1, d).astype(F32) for t in (k_k, k_a, r_k, ln_w, ln_b)]
    return pl.pallas_call(
        _rwkv_kernel,
        grid=(d // width, s // c),
        in_specs=[act] * 6 + [row] * 5,
        out_specs=act,
        out_shape=jax.ShapeDtypeStruct((b, s, d), BF16),
        scratch_shapes=[pltpu.VMEM((b, nu, RWKV_UNIT, RWKV_UNIT), F32)],
        compiler_params=_params("parallel", "arbitrary"),
        name="rwkv7_chunk",
    )(r, k, v, wz, az, gate, *rows)


def _pad_cols(w, mult=128):
    n = w.shape[1]
    return jnp.pad(w, ((0, 0), (0, (-n) % mult)))


def _pad_rows(w, mult=128):
    n = w.shape[0]
    return jnp.pad(w, ((0, (-n) % mult), (0, 0)))


def rwkv_layer(x, b, s, gain, mu, w_r, w_k, w_v, w0, w_w1, w_w2, a0, a_w1, a_w2, g_w1, g_w2,
               k_k, k_a, r_k, ln_w, ln_b, w_o):
    t, d = x.shape
    hn = rmsnorm(x, gain).reshape(b, s, d)
    xx = jnp.pad(hn, ((0, 0), (1, 0), (0, 0)))[:, :-1] - hn
    xr, xw, xk, xv, xa, xg = ((hn + xx * mu[i]).astype(BF16).reshape(t, d) for i in range(6))
    bf = lambda w: w.astype(BF16)
    r = matmul(xr, bf(w_r))
    k = matmul(xk, bf(w_k))
    v = matmul(xv, bf(w_v))
    hw = matmul(xw, bf(_pad_cols(w_w1)), act="tanh", out_dtype=BF16, tn=128)
    wz = matmul(hw, bf(_pad_rows(w_w2)), bias=w0)
    ha = matmul(xa, bf(_pad_cols(a_w1)), out_dtype=BF16, tn=128)
    az = matmul(ha, bf(_pad_rows(a_w2)), bias=a0)
    hg = matmul(xg, bf(_pad_cols(g_w1)), act="sigmoid", out_dtype=BF16, tn=256)
    gate = matmul(hg, bf(_pad_rows(g_w2)))
    sh = lambda z: z.reshape(b, s, d)
    y = rwkv_core(sh(r), sh(k), sh(v), sh(wz), sh(az), sh(gate), k_k, k_a, r_k, ln_w, ln_b)
    return matmul(y.reshape(t, d), bf(w_o), residual=x)


def kernel(x, positions, l0_norm_mix, l0_attn_w_in, l0_attn_w_out, l0_norm_ffn, l0_ffn_w_in, l0_ffn_w_out, l1_norm_mix, l1_s5_a_re, l1_s5_a_im, l1_s5_log_dt, l1_s5_b_re, l1_s5_b_im, l1_s5_c_re, l1_s5_c_im, l1_s5_d, l1_s5_w_glu, l1_norm_ffn, l1_ffn_w_in, l1_ffn_w_out, l2_norm_mix, l2_rwkv_mu, l2_rwkv_w_r, l2_rwkv_w_k, l2_rwkv_w_v, l2_rwkv_w0, l2_rwkv_w_w1, l2_rwkv_w_w2, l2_rwkv_a0, l2_rwkv_a_w1, l2_rwkv_a_w2, l2_rwkv_g_w1, l2_rwkv_g_w2, l2_rwkv_k_k, l2_rwkv_k_a, l2_rwkv_r_k, l2_rwkv_ln_w, l2_rwkv_ln_b, l2_rwkv_w_o, l2_norm_ffn, l2_ffn_w_in, l2_ffn_w_out, l3_norm_mix, l3_attn_w_in, l3_attn_w_out, l3_norm_ffn, l3_ffn_w_in, l3_ffn_w_out, final_norm):
    b, s, d = x.shape
    h = x.reshape(b * s, d)
    rope = _rope_tables(positions)

    def channel_mixer(h, gain, w_in, w_out, final_gain=None):
        return ffn(h, gain, w_in.astype(BF16), w_out.astype(BF16), final_gain)

    h = attention_layer(h, b, s, rope, l0_norm_mix, l0_attn_w_in, l0_attn_w_out)
    h = channel_mixer(h, l0_norm_ffn, l0_ffn_w_in, l0_ffn_w_out)
    h = s5_layer(h, b, s, l1_norm_mix, l1_s5_a_re, l1_s5_a_im, l1_s5_log_dt, l1_s5_b_re, l1_s5_b_im,
                 l1_s5_c_re, l1_s5_c_im, l1_s5_d, l1_s5_w_glu)
    h = channel_mixer(h, l1_norm_ffn, l1_ffn_w_in, l1_ffn_w_out)
    h = rwkv_layer(h, b, s, l2_norm_mix, l2_rwkv_mu, l2_rwkv_w_r, l2_rwkv_w_k, l2_rwkv_w_v, l2_rwkv_w0,
                   l2_rwkv_w_w1, l2_rwkv_w_w2, l2_rwkv_a0, l2_rwkv_a_w1, l2_rwkv_a_w2, l2_rwkv_g_w1,
                   l2_rwkv_g_w2, l2_rwkv_k_k, l2_rwkv_k_a, l2_rwkv_r_k, l2_rwkv_ln_w, l2_rwkv_ln_b,
                   l2_rwkv_w_o)
    h = channel_mixer(h, l2_norm_ffn, l2_ffn_w_in, l2_ffn_w_out)
    h = attention_layer(h, b, s, rope, l3_norm_mix, l3_attn_w_in, l3_attn_w_out)
    h = channel_mixer(h, l3_norm_ffn, l3_ffn_w_in, l3_ffn_w_out, final_gain=final_norm)
    return h.reshape(b, s, d)
```

```python
import functools
import math

import jax
import jax.numpy as jnp
from jax import lax
from jax.experimental import pallas as pl
from jax.experimental.pallas import tpu as pltpu

F32 = jnp.float32
BF16 = jnp.bfloat16

RMS_EPS = 1e-6
LANES = 128
VMEM_LIMIT_BYTES = 56 * 1024 * 1024

ATTN_HEAD_DIM = 128
ATTN_HEADS = 8
ATTN_GROUPS = ((128, 1), (512, 4), (2048, 16))
ATTN_BLOCK = 128
ROPE_THETA = 500000.0
ROPE_DIM = ATTN_HEAD_DIM // 4
ATTN_WIDTH = ATTN_HEADS * ATTN_HEAD_DIM
NEG_BIG = -1e30

S5_GROUP = 16
S5_STATE = 64
S5_CHUNK = 16
S5_GROUPS_PER_STEP = 4

RWKV_HEAD = 64
RWKV_CHUNK = 64
RWKV_HEADS_PER_UNIT = 4
RWKV_UNIT = RWKV_HEADS_PER_UNIT * RWKV_HEAD
RWKV_GN_EPS = 64e-5


def _params(*sem):
    return pltpu.CompilerParams(dimension_semantics=sem, vmem_limit_bytes=VMEM_LIMIT_BYTES)


def _tile(n, pref):
    t = min(n, pref)
    assert n % t == 0, (n, pref)
    return t


def _rms(x, g):
    return x * lax.rsqrt(jnp.mean(x * x, axis=-1, keepdims=True) + RMS_EPS) * g


def _nt(a, b):
    return lax.dot_general(a, b, (((1,), (1,)), ((), ())), preferred_element_type=F32)


def _tn(a, b):
    return lax.dot_general(a, b, (((0,), (0,)), ((), ())), preferred_element_type=F32)


def _dot(a, b):
    return jnp.dot(a, b, preferred_element_type=F32)


def _rmsnorm_kernel(x_ref, g_ref, o_ref):
    o_ref[...] = _rms(x_ref[...], g_ref[...]).astype(o_ref.dtype)


def rmsnorm(x, gain, out_dtype=F32, tm=512):
    t, d = x.shape
    tm = _tile(t, tm)
    return pl.pallas_call(
        _rmsnorm_kernel,
        grid=(t // tm,),
        in_specs=[pl.BlockSpec((tm, d), lambda i: (i, 0)), pl.BlockSpec((1, d), lambda i: (0, 0))],
        out_specs=pl.BlockSpec((tm, d), lambda i: (i, 0)),
        out_shape=jax.ShapeDtypeStruct((t, d), out_dtype),
        compiler_params=_params("parallel"),
        name="rmsnorm",
    )(x, gain.reshape(1, d))


def _mm_kernel(*refs, has_bias, act, has_res):
    it = iter(refs)
    x_ref = next(it)
    w_ref = next(it)
    b_ref = next(it) if has_bias else None
    r_ref = next(it) if has_res else None
    o_ref = next(it)
    acc = _dot(x_ref[...].astype(BF16), w_ref[...])
    if has_bias:
        acc = acc + b_ref[...]
    if act == "tanh":
        acc = jnp.tanh(acc)
    elif act == "sigmoid":
        acc = jax.nn.sigmoid(acc)
    if has_res:
        acc = acc + r_ref[...]
    o_ref[...] = acc.astype(o_ref.dtype)


def matmul(x, w, *, bias=None, act=None, residual=None, out_dtype=F32, tm=1024, tn=1024):
    t, k = x.shape
    n = w.shape[1]
    tm, tn = _tile(t, tm), _tile(n, tn)
    ins = [x, w]
    specs = [pl.BlockSpec((tm, k), lambda i, j: (i, 0)), pl.BlockSpec((k, tn), lambda i, j: (0, j))]
    if bias is not None:
        ins.append(bias.reshape(1, n).astype(F32))
        specs.append(pl.BlockSpec((1, tn), lambda i, j: (0, j)))
    if residual is not None:
        ins.append(residual)
        specs.append(pl.BlockSpec((tm, tn), lambda i, j: (i, j)))
    return pl.pallas_call(
        functools.partial(_mm_kernel, has_bias=bias is not None, act=act, has_res=residual is not None),
        grid=(t // tm, n // tn),
        in_specs=specs,
        out_specs=pl.BlockSpec((tm, tn), lambda i, j: (i, j)),
        out_shape=jax.ShapeDtypeStruct((t, n), out_dtype),
        compiler_params=_params("parallel", "parallel"),
        name="matmul",
    )(*ins)


def _ffn_kernel(x_ref, g_ref, wg_ref, wu_ref, wo_ref, fg_ref, o_ref, xn_ref, *, final_norm):
    j = pl.program_id(1)

    @pl.when(j == 0)
    def _():
        x = x_ref[...]
        xn_ref[...] = _rms(x, g_ref[...]).astype(BF16)
        o_ref[...] = x

    xn = xn_ref[...]
    gate = _dot(xn, wg_ref[...])
    up = _dot(xn, wu_ref[...])
    h = (gate * jax.nn.sigmoid(gate) * up).astype(BF16)
    o_ref[...] += _dot(h, wo_ref[...])

    if final_norm:
        @pl.when(j == pl.num_programs(1) - 1)
        def _():
            o_ref[...] = _rms(o_ref[...], fg_ref[...])


def ffn(x, gain, w_in, w_out, final_gain=None, tm=512, tf=512):
    t, d = x.shape
    f = w_out.shape[0]
    tm, tf = _tile(t, tm), _tile(f, tf)
    nf = f // tf
    fg = gain if final_gain is None else final_gain
    return pl.pallas_call(
        functools.partial(_ffn_kernel, final_norm=final_gain is not None),
        grid=(t // tm, nf),
        in_specs=[
            pl.BlockSpec((tm, d), lambda i, j: (i, 0)),
            pl.BlockSpec((1, d), lambda i, j: (0, 0)),
            pl.BlockSpec((d, tf), lambda i, j: (0, j)),
            pl.BlockSpec((d, tf), lambda i, j: (0, j + nf)),
            pl.BlockSpec((tf, d), lambda i, j: (j, 0)),
            pl.BlockSpec((1, d), lambda i, j: (0, 0)),
        ],
        out_specs=pl.BlockSpec((tm, d), lambda i, j: (i, 0)),
        out_shape=jax.ShapeDtypeStruct((t, d), F32),
        scratch_shapes=[pltpu.VMEM((tm, d), BF16)],
        compiler_params=_params("parallel", "arbitrary"),
        name="ffn",
    )(x, gain.reshape(1, d), w_in, w_in, w_out, fg.reshape(1, d))


def _norm_residues_kernel(x_ref, g_ref, *refs):
    out_refs, hn_ref = refs[:-1], refs[-1]
    hn = _rms(x_ref[0], g_ref[...])
    ncol, tm, lanes = hn_ref.shape
    for c in range(ncol):
        hn_ref[c] = hn[:, c * lanes:(c + 1) * lanes]
    for o_ref in out_refs:
        d = o_ref.shape[1]
        if d == 1:
            o_ref[0, 0] = hn.astype(o_ref.dtype)
            continue
        for r in range(d):
            for c in range(ncol):
                o_ref[0, r, :, c * lanes:(c + 1) * lanes] = (
                    hn_ref[c, pl.ds(r, tm // d, stride=d), :].astype(o_ref.dtype))


def norm_by_residue(x, gain, dilations, tm=512):
    b, s, d_model = x.shape
    tm = _tile(s, tm)
    return pl.pallas_call(
        _norm_residues_kernel,
        grid=(b, s // tm),
        in_specs=[pl.BlockSpec((1, tm, d_model), lambda bi, i: (bi, i, 0)),
                  pl.BlockSpec((1, d_model), lambda bi, i: (0, 0))],
        out_specs=[pl.BlockSpec((1, d, tm // d, d_model), lambda bi, i: (bi, 0, i, 0)) for d in dilations],
        out_shape=[jax.ShapeDtypeStruct((b, d, s // d, d_model), BF16) for d in dilations],
        scratch_shapes=[pltpu.VMEM((d_model // LANES, tm, LANES), F32)],
        compiler_params=_params("parallel", "parallel"),
        name="norm_by_residue",
    )(x, gain.reshape(1, d_model))


def _qkv_kernel(x_ref, w_ref, cos_ref, sin_ref, o_ref):
    acc = _dot(x_ref[...], w_ref[...])
    kind = pl.program_id(1)

    @pl.when(kind == 2)
    def _():
        o_ref[...] = acc.astype(o_ref.dtype)

    @pl.when(kind != 2)
    def _():
        scale = jnp.where(kind == 0, ATTN_HEAD_DIM ** -0.5, 1.0).astype(F32)
        cos = cos_ref[...] * scale
        sin = sin_ref[...] * scale
        lane = lax.broadcasted_iota(jnp.int32, cos.shape, 1)
        half = ROPE_DIM // 2
        for h in range(ATTN_HEADS):
            sl = slice(h * ATTN_HEAD_DIM, (h + 1) * ATTN_HEAD_DIM)
            xh = acc[:, sl]
            rot = jnp.where(lane < half, pltpu.roll(xh, ATTN_HEAD_DIM - half, 1), pltpu.roll(xh, half, 1))
            o_ref[:, sl] = (xh * cos + rot * sin).astype(o_ref.dtype)


def qkv_projection(x, w_in, gi, cos_t, sin_t, tm=1024):
    t, d = x.shape
    tm = _tile(t, tm)
    tn = ATTN_WIDTH
    return pl.pallas_call(
        _qkv_kernel,
        grid=(t // tm, 3),
        in_specs=[
            pl.BlockSpec((tm, d), lambda i, j: (i, 0)),
            pl.BlockSpec((d, tn), lambda i, j: (0, gi * 3 + j)),
            pl.BlockSpec((tm, ATTN_HEAD_DIM), lambda i, j: (i, 0)),
            pl.BlockSpec((tm, ATTN_HEAD_DIM), lambda i, j: (i, 0)),
        ],
        out_specs=pl.BlockSpec((tm, tn), lambda i, j: (i, j)),
        out_shape=jax.ShapeDtypeStruct((t, 3 * tn), BF16),
        compiler_params=_params("parallel", "parallel"),
        name="qkv_rope",
    )(x, w_in, cos_t, sin_t)


def _attn_kernel(q_ref, kp_ref, kc_ref, vp_ref, vc_ref, o_ref, st_ref):
    n = pl.program_id(1)
    blk = ATTN_BLOCK
    qi = lax.broadcasted_iota(jnp.int32, (blk, 2 * blk), 0)
    kj = lax.broadcasted_iota(jnp.int32, (blk, 2 * blk), 1)
    dist = qi + blk - kj
    valid = (dist >= 0) & (dist <= blk) & ((kj >= blk) | (n > 0))
    lane = lax.broadcasted_iota(jnp.int32, (blk, ATTN_HEAD_DIM), 1)
    stats = jnp.zeros((blk, ATTN_HEAD_DIM), F32)
    for h in range(ATTN_HEADS):
        sl = slice(h * ATTN_HEAD_DIM, (h + 1) * ATTN_HEAD_DIM)
        q = q_ref[0, :, sl]
        k = jnp.concatenate([kp_ref[0, :, sl], kc_ref[0, :, sl]], axis=0)
        v = jnp.concatenate([vp_ref[0, :, sl], vc_ref[0, :, sl]], axis=0)
        s = jnp.where(valid, _nt(q, k), NEG_BIG)
        mx = jnp.max(s, axis=-1, keepdims=True)
        p = jnp.exp(s - mx)
        den = jnp.sum(p, axis=-1, keepdims=True)
        num = _dot(p.astype(BF16), v)
        o_ref[0, :, sl] = (num / den).astype(o_ref.dtype)
        stats = jnp.where(lane == h, mx + jnp.log(den), stats)
    st_ref[0] = stats


def attention_group(qkv):
    nseq, length, _ = qkv.shape
    assert length % ATTN_BLOCK == 0
    nb = length // ATTN_BLOCK
    blk = (1, ATTN_BLOCK, ATTN_WIDTH)

    def cur(off):
        return pl.BlockSpec(blk, lambda r, nn: (r, nn, off))

    def prev(off):
        return pl.BlockSpec(blk, lambda r, nn: (r, jnp.maximum(nn - 1, 0), off))

    return pl.pallas_call(
        _attn_kernel,
        grid=(nseq, nb),
        in_specs=[cur(0), prev(1), cur(1), prev(2), cur(2)],
        out_specs=[
            pl.BlockSpec(blk, lambda r, nn: (r, nn, 0)),
            pl.BlockSpec((1, ATTN_BLOCK, ATTN_HEAD_DIM), lambda r, nn: (r, nn, 0)),
        ],
        out_shape=[
            jax.ShapeDtypeStruct((nseq, length, ATTN_WIDTH), BF16),
            jax.ShapeDtypeStruct((nseq, length, ATTN_HEAD_DIM), F32),
        ],
        compiler_params=_params("parallel", "parallel"),
        name="dilated_attn",
    )(qkv, qkv, qkv, qkv, qkv)


def _attn_out_kernel(o0_ref, o1_ref, o2_ref, s0_ref, s1_ref, s2_ref, w_ref, r_ref, out_ref, om_ref):
    @pl.when(pl.program_id(1) == 0)
    def _():
        l0, l1, l2 = s0_ref[...], s1_ref[...], s2_ref[...]
        m = jnp.maximum(jnp.maximum(l0, l1), l2)
        e0, e1, e2 = jnp.exp(l0 - m), jnp.exp(l1 - m), jnp.exp(l2 - m)
        inv = 1.0 / (e0 + e1 + e2)
        w0, w1, w2 = e0 * inv, e1 * inv, e2 * inv
        for h in range(ATTN_HEADS):
            sl = slice(h * ATTN_HEAD_DIM, (h + 1) * ATTN_HEAD_DIM)
            om = (w0[:, h:h + 1] * o0_ref[:, sl].astype(F32)
                  + w1[:, h:h + 1] * o1_ref[:, sl].astype(F32)
                  + w2[:, h:h + 1] * o2_ref[:, sl].astype(F32))
            om_ref[:, sl] = om.astype(BF16)

    out_ref[...] = r_ref[...] + _dot(om_ref[...], w_ref[...])


def attention_out(parts, w_out, residual, tm=512, tn=1024):
    (o0, s0), (o1, s1), (o2, s2) = parts
    t, k = o0.shape
    n = w_out.shape[1]
    tm, tn = _tile(t, tm), _tile(n, tn)
    ospec = pl.BlockSpec((tm, k), lambda i, j: (i, 0))
    sspec = pl.BlockSpec((tm, ATTN_HEAD_DIM), lambda i, j: (i, 0))
    return pl.pallas_call(
        _attn_out_kernel,
        grid=(t // tm, n // tn),
        in_specs=[ospec, ospec, ospec, sspec, sspec, sspec,
                  pl.BlockSpec((k, tn), lambda i, j: (0, j)),
                  pl.BlockSpec((tm, tn), lambda i, j: (i, j))],
        out_specs=pl.BlockSpec((tm, tn), lambda i, j: (i, j)),
        out_shape=jax.ShapeDtypeStruct((t, n), F32),
        scratch_shapes=[pltpu.VMEM((tm, k), BF16)],
        compiler_params=_params("parallel", "arbitrary"),
        name="attn_merge_out",
    )(o0, o1, o2, s0, s1, s2, w_out, residual)


def _by_residue(t, dilation):
    b, s = t.shape[:2]
    return jnp.swapaxes(t.reshape(b, s // dilation, dilation, *t.shape[2:]), 1, 2)


def _rope_tables(positions):
    half = ROPE_DIM // 2
    inv_freq = ROPE_THETA ** (-jnp.arange(half, dtype=F32) * 2.0 / ROPE_DIM)
    pad = ATTN_HEAD_DIM - ROPE_DIM
    tables = []
    for _, dil in ATTN_GROUPS:
        ang = _by_residue(positions, dil).astype(F32).reshape(-1, 1) * inv_freq
        cos, sin = jnp.cos(ang), jnp.sin(ang)
        tables.append((jnp.concatenate([cos, cos, jnp.ones((ang.shape[0], pad), F32)], axis=1),
                       jnp.concatenate([-sin, sin, jnp.zeros((ang.shape[0], pad), F32)], axis=1)))
    return tables


def attention_layer(x, b, s, rope, gain, w_in, w_out):
    t, d = x.shape
    w_in = w_in.astype(BF16)
    dils = [dil for _, dil in ATTN_GROUPS]
    hns = norm_by_residue(x.reshape(b, s, d), gain, dils)
    parts = []
    for gi, (dil, hn, (cos_t, sin_t)) in enumerate(zip(dils, hns, rope)):
        qkv = qkv_projection(hn.reshape(t, d), w_in, gi, cos_t, sin_t)
        o, st = attention_group(qkv.reshape(b * dil, s // dil, -1))
        back = lambda z: jnp.swapaxes(z.reshape(b, dil, s // dil, -1), 1, 2).reshape(t, -1)
        parts.append((back(o), back(st)))
    return attention_out(parts, w_out.astype(BF16), x)


def _gelu_tanh(y):
    return 0.5 * y * (1.0 + jnp.tanh(math.sqrt(2.0 / math.pi) * (y + 0.044715 * (y * y * y))))


def _s5_kernel(u_ref, toep_ref, wp_ref, q_ref, c1_ref, c2_ref, z_ref, *, chunks_per_seq):
    groups, nch, _ = u_ref.shape
    row = lax.broadcasted_iota(jnp.int32, (nch, 2 * S5_STATE), 0) % chunks_per_seq
    nsteps = chunks_per_seq.bit_length() - 1
    for g in range(groups):
        u = u_ref[g]
        y = _dot(u, toep_ref[g])
        x = _dot(u, wp_ref[g])
        for i in range(nsteps):
            m = 1 << i
            sh = jnp.where(row >= m, pltpu.roll(x, m, 0), 0.0)
            x = x + c1_ref[g, i:i + 1, :] * sh + c2_ref[g, i:i + 1, :] * pltpu.roll(sh, S5_STATE, 1)
        xprev = jnp.where(row >= 1, pltpu.roll(x, 1, 0), 0.0)
        y = y + _dot(xprev.astype(BF16), q_ref[g])
        z_ref[g] = _gelu_tanh(y).astype(z_ref.dtype)


def _s5_operators(a_re, a_im, log_dt, b_re, b_im, c_re, c_im, d_skip, chunks_per_seq):
    lc = S5_CHUNK
    g, p = a_re.shape
    dt = jnp.exp(log_dt)[:, None]
    mag = jnp.exp(dt * a_re)
    ab_re = mag * jnp.cos(dt * a_im)
    ab_im = mag * jnp.sin(dt * a_im)
    inv = 1.0 / (a_re * a_re + a_im * a_im)
    f_re = ((ab_re - 1.0) * a_re + ab_im * a_im) * inv
    f_im = (ab_im * a_re - (ab_re - 1.0) * a_im) * inv
    bb_re = f_re[..., None] * b_re - f_im[..., None] * b_im
    bb_im = f_re[..., None] * b_im + f_im[..., None] * b_re

    def power(j):
        jf = j.astype(F32)[:, None, None]
        m = jnp.exp(jf * (dt * a_re))
        return m * jnp.cos(jf * (dt * a_im)), m * jnp.sin(jf * (dt * a_im))

    pr, pi = power(jnp.arange(lc + 1))
    ba_re = pr[..., None] * bb_re - pi[..., None] * bb_im
    ba_im = pr[..., None] * bb_im + pi[..., None] * bb_re
    kern = (jnp.einsum("jgpa,gcp->jgac", ba_re, c_re) - jnp.einsum("jgpa,gcp->jgac", ba_im, c_im))
    ti = jnp.arange(lc)
    lag = ti[None, :] - ti[:, None]
    toep = jnp.where((lag >= 0)[None, :, None, :, None],
                     jnp.transpose(kern[jnp.clip(lag, 0, lc)], (2, 0, 3, 1, 4)), 0.0)
    eye = jnp.eye(lc, dtype=F32)[:, None, :, None] * jnp.eye(S5_GROUP, dtype=F32)[None, :, None, :]
    toep = toep + d_skip.reshape(g, 1, S5_GROUP, 1, 1) * eye[None]
    toep = toep.reshape(g, lc * S5_GROUP, lc * S5_GROUP)
    rev = lc - 1 - ti
    wp = jnp.concatenate([ba_re[rev], ba_im[rev]], axis=2)
    wp = jnp.transpose(wp, (1, 0, 3, 2)).reshape(g, lc * S5_GROUP, 2 * p)
    qr, qi = pr[1:], pi[1:]
    q_top = (jnp.einsum("tgp,gcp->gptc", qr, c_re) - jnp.einsum("tgp,gcp->gptc", qi, c_im))
    q_bot = (-jnp.einsum("tgp,gcp->gptc", qi, c_re) - jnp.einsum("tgp,gcp->gptc", qr, c_im))
    q = jnp.concatenate([q_top, q_bot], axis=1).reshape(g, 2 * p, lc * S5_GROUP)
    nsteps = chunks_per_seq.bit_length() - 1
    sr, si = power(lc * (2 ** jnp.arange(nsteps)))
    c1 = jnp.transpose(jnp.concatenate([sr, sr], axis=2), (1, 0, 2))
    c2 = jnp.transpose(jnp.concatenate([-si, si], axis=2), (1, 0, 2))
    return toep.astype(BF16), wp.astype(BF16), q.astype(BF16), c1, c2


def s5_core(u, ops, b, s):
    t, d = u.shape
    g = d // S5_GROUP
    lc = S5_CHUNK
    cps = s // lc
    assert cps & (cps - 1) == 0
    nch = t // lc
    width = lc * S5_GROUP
    toep, wp, q, c1, c2 = ops
    nsteps = c1.shape[1]
    ug = jnp.transpose(u.reshape(nch, lc, g, S5_GROUP), (2, 0, 1, 3)).reshape(g, nch, width)
    gs = _tile(g, S5_GROUPS_PER_STEP)
    z = pl.pallas_call(
        functools.partial(_s5_kernel, chunks_per_seq=cps),
        grid=(g // gs,),
        in_specs=[
            pl.BlockSpec((gs, nch, width), lambda i: (i, 0, 0)),
            pl.BlockSpec((gs, width, width), lambda i: (i, 0, 0)),
            pl.BlockSpec((gs, width, 2 * S5_STATE), lambda i: (i, 0, 0)),
            pl.BlockSpec((gs, 2 * S5_STATE, width), lambda i: (i, 0, 0)),
            pl.BlockSpec((gs, nsteps, 2 * S5_STATE), lambda i: (i, 0, 0)),
            pl.BlockSpec((gs, nsteps, 2 * S5_STATE), lambda i: (i, 0, 0)),
        ],
        out_specs=pl.BlockSpec((gs, nch, width), lambda i: (i, 0, 0)),
        out_shape=jax.ShapeDtypeStruct((g, nch, width), BF16),
        compiler_params=_params("parallel"),
        name="s5_chunk_scan",
    )(ug, toep, wp, q, c1, c2)
    return jnp.transpose(z.reshape(g, nch, lc, S5_GROUP), (1, 2, 0, 3)).reshape(t, d)


def _glu_kernel(z_ref, wv_ref, wg_ref, r_ref, o_ref):
    z = z_ref[...]
    val = _dot(z, wv_ref[...])
    gate = _dot(z, wg_ref[...])
    o_ref[...] = r_ref[...] + val * jax.nn.sigmoid(gate)


def glu_out(z, w_glu, residual, tm=1024, tn=1024):
    t, k = z.shape
    n = w_glu.shape[1] // 2
    tm, tn = _tile(t, tm), _tile(n, tn)
    nn = n // tn
    return pl.pallas_call(
        _glu_kernel,
        grid=(t // tm, nn),
        in_specs=[
            pl.BlockSpec((tm, k), lambda i, j: (i, 0)),
            pl.BlockSpec((k, tn), lambda i, j: (0, j)),
            pl.BlockSpec((k, tn), lambda i, j: (0, j + nn)),
            pl.BlockSpec((tm, tn), lambda i, j: (i, j)),
        ],
        out_specs=pl.BlockSpec((tm, tn), lambda i, j: (i, j)),
        out_shape=jax.ShapeDtypeStruct((t, n), F32),
        compiler_params=_params("parallel", "parallel"),
        name="glu_out",
    )(z, w_glu, w_glu, residual)


def s5_layer(x, b, s, gain, a_re, a_im, log_dt, b_re, b_im, c_re, c_im, d_skip, w_glu):
    u = rmsnorm(x, gain, out_dtype=BF16)
    ops = _s5_operators(a_re, a_im, log_dt, b_re, b_im, c_re, c_im, d_skip, s // S5_CHUNK)
    z = s5_core(u, ops, b, s)
    return glu_out(z, w_glu.astype(BF16), x)


def _split2(x):
    hi = x.astype(BF16)
    lo = (x - hi.astype(F32)).astype(BF16)
    return hi, lo


def _rwkv_kernel(r_ref, k_ref, v_ref, wz_ref, az_ref, g_ref, kk_ref, ka_ref, rk_ref, lnw_ref, lnb_ref,
                 o_ref, state_ref):
    nb, c, width = r_ref.shape
    unit = RWKV_UNIT
    nu = width // unit
    reps = unit // c

    @pl.when(pl.program_id(1) == 0)
    def _():
        state_ref[...] = jnp.zeros_like(state_ref)

    ri = lax.broadcasted_iota(jnp.int32, (unit, unit), 0)
    ci = lax.broadcasted_iota(jnp.int32, (unit, unit), 1)
    head_bd = (ri // RWKV_HEAD) == (ci // RWKV_HEAD)
    stack_bd = (ri // c) == (ci // RWKV_HEAD)
    chunk_bd = (ri // c) == (ci // c)
    ones_bd = head_bd.astype(BF16)
    eye = (ri == ci).astype(F32)
    tr = lax.broadcasted_iota(jnp.int32, (c, unit), 0)
    tc = lax.broadcasted_iota(jnp.int32, (c, unit), 1) % c
    strict = tc < tr
    incl = tc <= tr
    li = lax.broadcasted_iota(jnp.int32, (c, c), 0)
    lj = lax.broadcasted_iota(jnp.int32, (c, c), 1)
    tri = (lj <= li).astype(BF16)

    def headsums(*xs):
        parts = _dot(jnp.concatenate([p for x in xs for p in _split2(x)], axis=0), ones_bd)
        return [parts[2 * i * c:(2 * i + 1) * c] + parts[(2 * i + 1) * c:(2 * i + 2) * c] for i in range(len(xs))]

    def stack(x, mask):
        return jnp.where(mask, jnp.concatenate([x] * reps, axis=0), 0.0).astype(BF16)

    units = [(b, slice(q * unit, (q + 1) * unit), q) for b in range(nb) for q in range(nu)]
    each = lambda fn, *cols: [fn(*vals) for vals in zip(*cols)]

    def load(ref):
        return [ref[b, :, sl] for b, sl, _ in units]

    def row(ref):
        return [ref[:, sl] for _, sl, _ in units]

    r, k, v, gate, wz, az = (load(ref) for ref in (r_ref, k_ref, v_ref, g_ref, wz_ref, az_ref))
    k_k, k_a, r_k, ln_w, ln_b = (row(ref) for ref in (kk_ref, ka_ref, rk_ref, lnw_ref, lnb_ref))

    logw = each(lambda z: -jnp.exp(-jax.nn.softplus(-z) - 0.5), wz)
    a = each(jax.nn.sigmoid, az)
    kk0 = each(lambda x, y: x * y, k, k_k)
    k2 = each(lambda x, al, ka: x * (1.0 + (al - 1.0) * ka), k, a, k_a)
    sums = each(lambda x, rr, kv, rk: headsums(x * x, rr * kv * rk), kk0, r, k2, r_k)
    kk = each(lambda x, s: x * lax.rsqrt(jnp.maximum(s[0], 1e-24)), kk0, sums)
    bonus = each(lambda s, vv: s[1] * vv, sums, v)
    bvec = each(lambda x, al: x * al, kk, a)

    def cumsum(lw):
        p0 = lw.astype(BF16)
        r1 = lw - p0.astype(F32)
        p1 = r1.astype(BF16)
        p2 = (r1 - p1.astype(F32)).astype(BF16)
        return _dot(tri, p0) + _dot(tri, p1) + _dot(tri, p2)

    cum = each(cumsum, logw)
    cend = each(lambda x: x[c - 1:c, :], cum)
    a_t = each(lambda x, cu, lw: (-x * jnp.exp(cu - lw)).astype(BF16), kk, cum, logw)
    r_t = each(lambda x, cu: (x * jnp.exp(cu)).astype(BF16), r, cum)
    w_inv = each(lambda cu: jnp.exp(-cu), cum)
    w_rest = each(lambda ce, cu: jnp.exp(ce - cu), cend, cum)
    bs = each(lambda x, w: stack(x * w, stack_bd), bvec, w_inv)
    ks = each(lambda x, w: stack(x * w, stack_bd), k2, w_inv)
    vs = each(lambda x: stack(x, stack_bd), v)

    scores = each(lambda at, rt, b_, k_: _nt(jnp.concatenate([at, rt], axis=0), jnp.concatenate([b_, k_], axis=0)),
                  a_t, r_t, bs, ks)
    s_k = each(lambda s: jnp.concatenate([jnp.where(strict, s[:c, unit:], 0.0), jnp.where(incl, s[c:, unit:], 0.0)],
                                         axis=0).astype(BF16), scores)
    s_rb = each(lambda s: jnp.where(incl, s[c:, :unit], 0.0).astype(BF16), scores)

    pw = each(lambda s: jnp.where(chunk_bd, jnp.concatenate([jnp.where(strict, s[:c, :unit], 0.0)] * reps, axis=0),
                                  0.0), scores)
    tm = each(lambda p: eye + p, pw)
    for _ in range(c.bit_length() - 2):
        pw = each(lambda p: _dot(p.astype(BF16), p.astype(BF16)), pw)
        tm = each(lambda t, p: t + _dot(t.astype(BF16), p.astype(BF16)), tm, pw)
    t_cat = each(lambda t: sum(t[h * c:(h + 1) * c] for h in range(1, reps)) + t[0:c], tm)

    state = [state_ref[b, q] for b, _, q in units]
    state_b = each(lambda s: s.astype(BF16), state)
    base = each(lambda at, rt, sb, sk, vv: _nt(jnp.concatenate([at, rt], axis=0), sb) + _dot(sk, vv),
                a_t, r_t, state_b, s_k, vs)
    u = each(lambda t, bb: _dot(t.astype(BF16), stack(bb[:c], stack_bd)), t_cat, base)
    y = each(lambda bb, srb, uu: bb[c:] + _dot(srb, stack(uu, stack_bd)), base, s_rb, u)
    upd = each(lambda uu, vv, bv, kv, wr: _tn(jnp.concatenate([uu, vv], axis=0).astype(BF16),
                                               jnp.concatenate([bv * wr, kv * wr], axis=0).astype(BF16)),
               u, v, bvec, k2, w_rest)
    for (b, _, q), s, ce, up in zip(units, state, cend, upd):
        state_ref[b, q] = jnp.where(head_bd, s * jnp.exp(ce) + up, 0.0)

    inv_n = 1.0 / RWKV_HEAD
    mean = each(lambda yy: headsums(yy)[0] * inv_n, y)
    yc = each(lambda yy, m: yy - m, y, mean)
    var = each(lambda z: headsums(z * z)[0] * inv_n, yc)
    for (b, sl, _), z, vr, lw, lb, bo, gt in zip(units, yc, var, ln_w, ln_b, bonus, gate):
        o_ref[b, :, sl] = ((z * lax.rsqrt(vr + RWKV_GN_EPS) * lw + lb + bo) * gt).astype(o_ref.dtype)


def rwkv_core(r, k, v, wz, az, gate, k_k, k_a, r_k, ln_w, ln_b, units_per_step=4):
    b, s, d = r.shape
    c = RWKV_CHUNK
    assert s % c == 0 and RWKV_UNIT % c == 0
    width = _tile(d, RWKV_UNIT * units_per_step)
    nu = width // RWKV_UNIT
    act = pl.BlockSpec((b, c, width), lambda i, j: (0, j, i))
    row = pl.BlockSpec((1, width), lambda i, j: (0, i))
    rows = [t.reshape(1, d).astype(F32) for t in (k_k, k_a, r_k, ln_w, ln_b)]
    return pl.pallas_call(
        _rwkv_kernel,
        grid=(d // width, s // c),
        in_specs=[act] * 6 + [row] * 5,
        out_specs=act,
        out_shape=jax.ShapeDtypeStruct((b, s, d), BF16),
        scratch_shapes=[pltpu.VMEM((b, nu, RWKV_UNIT, RWKV_UNIT), F32)],
        compiler_params=_params("parallel", "arbitrary"),
        name="rwkv7_chunk",
    )(r, k, v, wz, az, gate, *rows)


def _pad_cols(w, mult=128):
    n = w.shape[1]
    return jnp.pad(w, ((0, 0), (0, (-n) % mult)))


def _pad_rows(w, mult=128):
    n = w.shape[0]
    return jnp.pad(w, ((0, (-n) % mult), (0, 0)))


def rwkv_layer(x, b, s, gain, mu, w_r, w_k, w_v, w0, w_w1, w_w2, a0, a_w1, a_w2, g_w1, g_w2,
               k_k, k_a, r_k, ln_w, ln_b, w_o):
    t, d = x.shape
    hn = rmsnorm(x, gain).reshape(b, s, d)
    xx = jnp.pad(hn, ((0, 0), (1, 0), (0, 0)))[:, :-1] - hn
    xr, xw, xk, xv, xa, xg = ((hn + xx * mu[i]).astype(BF16).reshape(t, d) for i in range(6))
    bf = lambda w: w.astype(BF16)
    r = matmul(xr, bf(w_r))
    k = matmul(xk, bf(w_k))
    v = matmul(xv, bf(w_v))
    hw = matmul(xw, bf(_pad_cols(w_w1)), act="tanh", out_dtype=BF16, tn=128)
    wz = matmul(hw, bf(_pad_rows(w_w2)), bias=w0)
    ha = matmul(xa, bf(_pad_cols(a_w1)), out_dtype=BF16, tn=128)
    az = matmul(ha, bf(_pad_rows(a_w2)), bias=a0)
    hg = matmul(xg, bf(_pad_cols(g_w1)), act="sigmoid", out_dtype=BF16, tn=256)
    gate = matmul(hg, bf(_pad_rows(g_w2)))
    sh = lambda z: z.reshape(b, s, d)
    y = rwkv_core(sh(r), sh(k), sh(v), sh(wz), sh(az), sh(gate), k_k, k_a, r_k, ln_w, ln_b)
    return matmul(y.reshape(t, d), bf(w_o), residual=x)


def kernel(x, positions, l0_norm_mix, l0_attn_w_in, l0_attn_w_out, l0_norm_ffn, l0_ffn_w_in, l0_ffn_w_out, l1_norm_mix, l1_s5_a_re, l1_s5_a_im, l1_s5_log_dt, l1_s5_b_re, l1_s5_b_im, l1_s5_c_re, l1_s5_c_im, l1_s5_d, l1_s5_w_glu, l1_norm_ffn, l1_ffn_w_in, l1_ffn_w_out, l2_norm_mix, l2_rwkv_mu, l2_rwkv_w_r, l2_rwkv_w_k, l2_rwkv_w_v, l2_rwkv_w0, l2_rwkv_w_w1, l2_rwkv_w_w2, l2_rwkv_a0, l2_rwkv_a_w1, l2_rwkv_a_w2, l2_rwkv_g_w1, l2_rwkv_g_w2, l2_rwkv_k_k, l2_rwkv_k_a, l2_rwkv_r_k, l2_rwkv_ln_w, l2_rwkv_ln_b, l2_rwkv_w_o, l2_norm_ffn, l2_ffn_w_in, l2_ffn_w_out, l3_norm_mix, l3_attn_w_in, l3_attn_w_out, l3_norm_ffn, l3_ffn_w_in, l3_ffn_w_out, final_norm):
    b, s, d = x.shape
    h = x.reshape(b * s, d)
    rope = _rope_tables(positions)

    def channel_mixer(h, gain, w_in, w_out, final_gain=None):
        return ffn(h, gain, w_in.astype(BF16), w_out.astype(BF16), final_gain)

    h = attention_layer(h, b, s, rope, l0_norm_mix, l0_attn_w_in, l0_attn_w_out)
    h = channel_mixer(h, l0_norm_ffn, l0_ffn_w_in, l0_ffn_w_out)
    h = s5_layer(h, b, s, l1_norm_mix, l1_s5_a_re, l1_s5_a_im, l1_s5_log_dt, l1_s5_b_re, l1_s5_b_im,
                 l1_s5_c_re, l1_s5_c_im, l1_s5_d, l1_s5_w_glu)
    h = channel_mixer(h, l1_norm_ffn, l1_ffn_w_in, l1_ffn_w_out)
    h = rwkv_layer(h, b, s, l2_norm_mix, l2_rwkv_mu, l2_rwkv_w_r, l2_rwkv_w_k, l2_rwkv_w_v, l2_rwkv_w0,
                   l2_rwkv_w_w1, l2_rwkv_w_w2, l2_rwkv_a0, l2_rwkv_a_w1, l2_rwkv_a_w2, l2_rwkv_g_w1,
                   l2_rwkv_g_w2, l2_rwkv_k_k, l2_rwkv_k_a, l2_rwkv_r_k, l2_rwkv_ln_w, l2_rwkv_ln_b,
                   l2_rwkv_w_o)
    h = channel_mixer(h, l2_norm_ffn, l2_ffn_w_in, l2_ffn_w_out)
    h = attention_layer(h, b, s, rope, l3_norm_mix, l3_attn_w_in, l3_attn_w_out)
    h = channel_mixer(h, l3_norm_ffn, l3_ffn_w_in, l3_ffn_w_out, final_gain=final_norm)
    return h.reshape(b, s, d)
```

```python
import functools
import math

import jax
import jax.numpy as jnp
from jax import lax
from jax.experimental import pallas as pl
from jax.experimental.pallas import tpu as pltpu

F32 = jnp.float32
BF16 = jnp.bfloat16

RMS_EPS = 1e-6
LANES = 128
VMEM_LIMIT_BYTES = 56 * 1024 * 1024

ATTN_HEAD_DIM = 128
ATTN_HEADS = 8
ATTN_GROUPS = ((128, 1), (512, 4), (2048, 16))
ATTN_BLOCK = 128
ROPE_THETA = 500000.0
ROPE_DIM = ATTN_HEAD_DIM // 4
ATTN_WIDTH = ATTN_HEADS * ATTN_HEAD_DIM
NEG_BIG = -1e30

S5_GROUP = 16
S5_STATE = 64
S5_CHUNK = 16

RWKV_HEAD = 64
RWKV_CHUNK = 64
RWKV_HEADS_PER_UNIT = 4
RWKV_UNIT = RWKV_HEADS_PER_UNIT * RWKV_HEAD
RWKV_GN_EPS = 64e-5


def _params(*sem):
    return pltpu.CompilerParams(dimension_semantics=sem, vmem_limit_bytes=VMEM_LIMIT_BYTES)


def _tile(n, pref):
    t = min(n, pref)
    assert n % t == 0, (n, pref)
    return t


def _rms(x, g):
    return x * lax.rsqrt(jnp.mean(x * x, axis=-1, keepdims=True) + RMS_EPS) * g


def _nt(a, b):
    return lax.dot_general(a, b, (((1,), (1,)), ((), ())), preferred_element_type=F32)


def _tn(a, b):
    return lax.dot_general(a, b, (((0,), (0,)), ((), ())), preferred_element_type=F32)


def _dot(a, b):
    return jnp.dot(a, b, preferred_element_type=F32)


def _rmsnorm_kernel(x_ref, g_ref, o_ref):
    o_ref[...] = _rms(x_ref[...], g_ref[...]).astype(o_ref.dtype)


def rmsnorm(x, gain, out_dtype=F32, tm=512):
    t, d = x.shape
    tm = _tile(t, tm)
    return pl.pallas_call(
        _rmsnorm_kernel,
        grid=(t // tm,),
        in_specs=[pl.BlockSpec((tm, d), lambda i: (i, 0)), pl.BlockSpec((1, d), lambda i: (0, 0))],
        out_specs=pl.BlockSpec((tm, d), lambda i: (i, 0)),
        out_shape=jax.ShapeDtypeStruct((t, d), out_dtype),
        compiler_params=_params("parallel"),
        name="rmsnorm",
    )(x, gain.reshape(1, d))


def _mm_kernel(*refs, has_bias, act, has_res):
    it = iter(refs)
    x_ref = next(it)
    w_ref = next(it)
    b_ref = next(it) if has_bias else None
    r_ref = next(it) if has_res else None
    o_ref = next(it)
    acc = _dot(x_ref[...].astype(BF16), w_ref[...])
    if has_bias:
        acc = acc + b_ref[...]
    if act == "tanh":
        acc = jnp.tanh(acc)
    elif act == "sigmoid":
        acc = jax.nn.sigmoid(acc)
    if has_res:
        acc = acc + r_ref[...]
    o_ref[...] = acc.astype(o_ref.dtype)


def matmul(x, w, *, bias=None, act=None, residual=None, out_dtype=F32, tm=1024, tn=1024):
    t, k = x.shape
    n = w.shape[1]
    tm, tn = _tile(t, tm), _tile(n, tn)
    ins = [x, w]
    specs = [pl.BlockSpec((tm, k), lambda i, j: (i, 0)), pl.BlockSpec((k, tn), lambda i, j: (0, j))]
    if bias is not None:
        ins.append(bias.reshape(1, n).astype(F32))
        specs.append(pl.BlockSpec((1, tn), lambda i, j: (0, j)))
    if residual is not None:
        ins.append(residual)
        specs.append(pl.BlockSpec((tm, tn), lambda i, j: (i, j)))
    return pl.pallas_call(
        functools.partial(_mm_kernel, has_bias=bias is not None, act=act, has_res=residual is not None),
        grid=(t // tm, n // tn),
        in_specs=specs,
        out_specs=pl.BlockSpec((tm, tn), lambda i, j: (i, j)),
        out_shape=jax.ShapeDtypeStruct((t, n), out_dtype),
        compiler_params=_params("parallel", "parallel"),
        name="matmul",
    )(*ins)


def _ffn_kernel(x_ref, g_ref, wg_ref, wu_ref, wo_ref, fg_ref, o_ref, xn_ref, *, final_norm):
    j = pl.program_id(1)

    @pl.when(j == 0)
    def _():
        x = x_ref[...]
        xn_ref[...] = _rms(x, g_ref[...]).astype(BF16)
        o_ref[...] = x

    xn = xn_ref[...]
    gate = _dot(xn, wg_ref[...])
    up = _dot(xn, wu_ref[...])
    h = (gate * jax.nn.sigmoid(gate) * up).astype(BF16)
    o_ref[...] += _dot(h, wo_ref[...])

    if final_norm:
        @pl.when(j == pl.num_programs(1) - 1)
        def _():
            o_ref[...] = _rms(o_ref[...], fg_ref[...])


def ffn(x, gain, w_in, w_out, final_gain=None, tm=1024, tf=512):
    t, d = x.shape
    f = w_out.shape[0]
    tm, tf = _tile(t, tm), _tile(f, tf)
    nf = f // tf
    fg = gain if final_gain is None else final_gain
    once = pl.Buffered(1)
    return pl.pallas_call(
        functools.partial(_ffn_kernel, final_norm=final_gain is not None),
        grid=(t // tm, nf),
        in_specs=[
            pl.BlockSpec((tm, d), lambda i, j: (i, 0), pipeline_mode=once),
            pl.BlockSpec((1, d), lambda i, j: (0, 0)),
            pl.BlockSpec((d, tf), lambda i, j: (0, j)),
            pl.BlockSpec((d, tf), lambda i, j: (0, j + nf)),
            pl.BlockSpec((tf, d), lambda i, j: (j, 0)),
            pl.BlockSpec((1, d), lambda i, j: (0, 0)),
        ],
        out_specs=pl.BlockSpec((tm, d), lambda i, j: (i, 0), pipeline_mode=once),
        out_shape=jax.ShapeDtypeStruct((t, d), F32),
        scratch_shapes=[pltpu.VMEM((tm, d), BF16)],
        compiler_params=_params("parallel", "arbitrary"),
        name="ffn",
    )(x, gain.reshape(1, d), w_in, w_in, w_out, fg.reshape(1, d))


def _norm_residues_kernel(x_ref, g_ref, *refs):
    out_refs, hn_ref = refs[:-1], refs[-1]
    hn = _rms(x_ref[0], g_ref[...])
    ncol, tm, lanes = hn_ref.shape
    for c in range(ncol):
        hn_ref[c] = hn[:, c * lanes:(c + 1) * lanes]
    for o_ref in out_refs:
        d = o_ref.shape[1]
        if d == 1:
            o_ref[0, 0] = hn.astype(o_ref.dtype)
            continue
        for r in range(d):
            for c in range(ncol):
                o_ref[0, r, :, c * lanes:(c + 1) * lanes] = (
                    hn_ref[c, pl.ds(r, tm // d, stride=d), :].astype(o_ref.dtype))


def norm_by_residue(x, gain, dilations, tm=512):
    b, s, d_model = x.shape
    tm = _tile(s, tm)
    return pl.pallas_call(
        _norm_residues_kernel,
        grid=(b, s // tm),
        in_specs=[pl.BlockSpec((1, tm, d_model), lambda bi, i: (bi, i, 0)),
                  pl.BlockSpec((1, d_model), lambda bi, i: (0, 0))],
        out_specs=[pl.BlockSpec((1, d, tm // d, d_model), lambda bi, i: (bi, 0, i, 0)) for d in dilations],
        out_shape=[jax.ShapeDtypeStruct((b, d, s // d, d_model), BF16) for d in dilations],
        scratch_shapes=[pltpu.VMEM((d_model // LANES, tm, LANES), F32)],
        compiler_params=_params("parallel", "parallel"),
        name="norm_by_residue",
    )(x, gain.reshape(1, d_model))


def _qkv_kernel(x_ref, w_ref, cos_ref, sin_ref, o_ref):
    acc = _dot(x_ref[...], w_ref[...])
    kind = pl.program_id(1)

    @pl.when(kind == 2)
    def _():
        o_ref[...] = acc.astype(o_ref.dtype)

    @pl.when(kind != 2)
    def _():
        scale = jnp.where(kind == 0, ATTN_HEAD_DIM ** -0.5, 1.0).astype(F32)
        cos = cos_ref[...] * scale
        sin = sin_ref[...] * scale
        lane = lax.broadcasted_iota(jnp.int32, cos.shape, 1)
        half = ROPE_DIM // 2
        for h in range(ATTN_HEADS):
            sl = slice(h * ATTN_HEAD_DIM, (h + 1) * ATTN_HEAD_DIM)
            xh = acc[:, sl]
            rot = jnp.where(lane < half, pltpu.roll(xh, ATTN_HEAD_DIM - half, 1), pltpu.roll(xh, half, 1))
            o_ref[:, sl] = (xh * cos + rot * sin).astype(o_ref.dtype)


def qkv_projection(x, w_in, gi, cos_t, sin_t, tm=1024):
    t, d = x.shape
    tm = _tile(t, tm)
    tn = ATTN_WIDTH
    return pl.pallas_call(
        _qkv_kernel,
        grid=(t // tm, 3),
        in_specs=[
            pl.BlockSpec((tm, d), lambda i, j: (i, 0)),
            pl.BlockSpec((d, tn), lambda i, j: (0, gi * 3 + j)),
            pl.BlockSpec((tm, ATTN_HEAD_DIM), lambda i, j: (i, 0)),
            pl.BlockSpec((tm, ATTN_HEAD_DIM), lambda i, j: (i, 0)),
        ],
        out_specs=pl.BlockSpec((tm, tn), lambda i, j: (i, j)),
        out_shape=jax.ShapeDtypeStruct((t, 3 * tn), BF16),
        compiler_params=_params("parallel", "parallel"),
        name="qkv_rope",
    )(x, w_in, cos_t, sin_t)


def _attn_kernel(q_ref, kp_ref, kc_ref, vp_ref, vc_ref, o_ref, st_ref):
    n = pl.program_id(1)
    blk = ATTN_BLOCK
    qi = lax.broadcasted_iota(jnp.int32, (blk, 2 * blk), 0)
    kj = lax.broadcasted_iota(jnp.int32, (blk, 2 * blk), 1)
    dist = qi + blk - kj
    valid = (dist >= 0) & (dist <= blk) & ((kj >= blk) | (n > 0))
    lane = lax.broadcasted_iota(jnp.int32, (blk, ATTN_HEAD_DIM), 1)
    stats = jnp.zeros((blk, ATTN_HEAD_DIM), F32)
    for h in range(ATTN_HEADS):
        sl = slice(h * ATTN_HEAD_DIM, (h + 1) * ATTN_HEAD_DIM)
        q = q_ref[0, :, sl]
        k = jnp.concatenate([kp_ref[0, :, sl], kc_ref[0, :, sl]], axis=0)
        v = jnp.concatenate([vp_ref[0, :, sl], vc_ref[0, :, sl]], axis=0)
        s = jnp.where(valid, _nt(q, k), NEG_BIG)
        mx = jnp.max(s, axis=-1, keepdims=True)
        p = jnp.exp(s - mx)
        den = jnp.sum(p, axis=-1, keepdims=True)
        num = _dot(p.astype(BF16), v)
        o_ref[0, :, sl] = (num / den).astype(o_ref.dtype)
        stats = jnp.where(lane == h, mx + jnp.log(den), stats)
    st_ref[0] = stats


def attention_group(qkv):
    nseq, length, _ = qkv.shape
    assert length % ATTN_BLOCK == 0
    nb = length // ATTN_BLOCK
    blk = (1, ATTN_BLOCK, ATTN_WIDTH)

    def cur(off):
        return pl.BlockSpec(blk, lambda r, nn: (r, nn, off))

    def prev(off):
        return pl.BlockSpec(blk, lambda r, nn: (r, jnp.maximum(nn - 1, 0), off))

    return pl.pallas_call(
        _attn_kernel,
        grid=(nseq, nb),
        in_specs=[cur(0), prev(1), cur(1), prev(2), cur(2)],
        out_specs=[
            pl.BlockSpec(blk, lambda r, nn: (r, nn, 0)),
            pl.BlockSpec((1, ATTN_BLOCK, ATTN_HEAD_DIM), lambda r, nn: (r, nn, 0)),
        ],
        out_shape=[
            jax.ShapeDtypeStruct((nseq, length, ATTN_WIDTH), BF16),
            jax.ShapeDtypeStruct((nseq, length, ATTN_HEAD_DIM), F32),
        ],
        compiler_params=_params("parallel", "parallel"),
        name="dilated_attn",
    )(qkv, qkv, qkv, qkv, qkv)


def _attn_out_kernel(o0_ref, o1_ref, o2_ref, s0_ref, s1_ref, s2_ref, w_ref, r_ref, out_ref, om_ref):
    @pl.when(pl.program_id(1) == 0)
    def _():
        l0, l1, l2 = s0_ref[...], s1_ref[...], s2_ref[...]
        m = jnp.maximum(jnp.maximum(l0, l1), l2)
        e0, e1, e2 = jnp.exp(l0 - m), jnp.exp(l1 - m), jnp.exp(l2 - m)
        inv = 1.0 / (e0 + e1 + e2)
        w0, w1, w2 = e0 * inv, e1 * inv, e2 * inv
        for h in range(ATTN_HEADS):
            sl = slice(h * ATTN_HEAD_DIM, (h + 1) * ATTN_HEAD_DIM)
            om = (w0[:, h:h + 1] * o0_ref[:, sl].astype(F32)
                  + w1[:, h:h + 1] * o1_ref[:, sl].astype(F32)
                  + w2[:, h:h + 1] * o2_ref[:, sl].astype(F32))
            om_ref[:, sl] = om.astype(BF16)

    out_ref[...] = r_ref[...] + _dot(om_ref[...], w_ref[...])


def attention_out(parts, w_out, residual, tm=512, tn=1024):
    (o0, s0), (o1, s1), (o2, s2) = parts
    t, k = o0.shape
    n = w_out.shape[1]
    tm, tn = _tile(t, tm), _tile(n, tn)
    ospec = pl.BlockSpec((tm, k), lambda i, j: (i, 0))
    sspec = pl.BlockSpec((tm, ATTN_HEAD_DIM), lambda i, j: (i, 0))
    return pl.pallas_call(
        _attn_out_kernel,
        grid=(t // tm, n // tn),
        in_specs=[ospec, ospec, ospec, sspec, sspec, sspec,
                  pl.BlockSpec((k, tn), lambda i, j: (0, j)),
                  pl.BlockSpec((tm, tn), lambda i, j: (i, j))],
        out_specs=pl.BlockSpec((tm, tn), lambda i, j: (i, j)),
        out_shape=jax.ShapeDtypeStruct((t, n), F32),
        scratch_shapes=[pltpu.VMEM((tm, k), BF16)],
        compiler_params=_params("parallel", "arbitrary"),
        name="attn_merge_out",
    )(o0, o1, o2, s0, s1, s2, w_out, residual)


def _by_residue(t, dilation):
    b, s = t.shape[:2]
    return jnp.swapaxes(t.reshape(b, s // dilation, dilation, *t.shape[2:]), 1, 2)


def _rope_tables(positions):
    half = ROPE_DIM // 2
    inv_freq = ROPE_THETA ** (-jnp.arange(half, dtype=F32) * 2.0 / ROPE_DIM)
    pad = ATTN_HEAD_DIM - ROPE_DIM
    tables = []
    for _, dil in ATTN_GROUPS:
        ang = _by_residue(positions, dil).astype(F32).reshape(-1, 1) * inv_freq
        cos, sin = jnp.cos(ang), jnp.sin(ang)
        tables.append((jnp.concatenate([cos, cos, jnp.ones((ang.shape[0], pad), F32)], axis=1),
                       jnp.concatenate([-sin, sin, jnp.zeros((ang.shape[0], pad), F32)], axis=1)))
    return tables


def attention_layer(x, b, s, rope, gain, w_in, w_out):
    t, d = x.shape
    w_in = w_in.astype(BF16)
    dils = [dil for _, dil in ATTN_GROUPS]
    hns = norm_by_residue(x.reshape(b, s, d), gain, dils)
    parts = []
    for gi, (dil, hn, (cos_t, sin_t)) in enumerate(zip(dils, hns, rope)):
        qkv = qkv_projection(hn.reshape(t, d), w_in, gi, cos_t, sin_t)
        o, st = attention_group(qkv.reshape(b * dil, s // dil, -1))
        back = lambda z: jnp.swapaxes(z.reshape(b, dil, s // dil, -1), 1, 2).reshape(t, -1)
        parts.append((back(o), back(st)))
    return attention_out(parts, w_out.astype(BF16), x)


def _gelu_tanh(y):
    return 0.5 * y * (1.0 + jnp.tanh(math.sqrt(2.0 / math.pi) * (y + 0.044715 * (y * y * y))))


def _s5_kernel(u_ref, perm_ref, toep_ref, wp_ref, q_ref, c1_ref, c2_ref, z_ref):
    lc = S5_CHUNK
    groups = toep_ref.shape[0]
    nch = u_ref.shape[1] // lc
    width = lc * S5_GROUP
    nsteps = c1_ref.shape[1]
    perm = perm_ref[...]
    by_pos = jnp.concatenate([u_ref[0, pl.ds(s, nch, stride=lc), :].astype(BF16) for s in range(lc)], axis=1)
    by_group = _dot(by_pos, perm).astype(BF16)
    row = lax.broadcasted_iota(jnp.int32, (nch, 2 * S5_STATE), 0)
    zs = []
    for g in range(groups):
        u = by_group[:, g * width:(g + 1) * width]
        y = _dot(u, toep_ref[g])
        x = _dot(u, wp_ref[g])
        for i in range(nsteps):
            m = 1 << i
            sh = jnp.where(row >= m, pltpu.roll(x, m, 0), 0.0)
            x = x + c1_ref[g, i:i + 1, :] * sh + c2_ref[g, i:i + 1, :] * pltpu.roll(sh, S5_STATE, 1)
        xprev = jnp.where(row >= 1, pltpu.roll(x, 1, 0), 0.0)
        y = y + _dot(xprev.astype(BF16), q_ref[g])
        zs.append(_gelu_tanh(y).astype(BF16))
    out = _nt(jnp.concatenate(zs, axis=1), perm)
    for s in range(lc):
        z_ref[0, pl.ds(s, nch, stride=lc), :] = out[:, s * LANES:(s + 1) * LANES]


def _s5_operators(a_re, a_im, log_dt, b_re, b_im, c_re, c_im, d_skip, chunks_per_seq):
    lc = S5_CHUNK
    g, p = a_re.shape
    dt = jnp.exp(log_dt)[:, None]
    mag = jnp.exp(dt * a_re)
    ab_re = mag * jnp.cos(dt * a_im)
    ab_im = mag * jnp.sin(dt * a_im)
    inv = 1.0 / (a_re * a_re + a_im * a_im)
    f_re = ((ab_re - 1.0) * a_re + ab_im * a_im) * inv
    f_im = (ab_im * a_re - (ab_re - 1.0) * a_im) * inv
    bb_re = f_re[..., None] * b_re - f_im[..., None] * b_im
    bb_im = f_re[..., None] * b_im + f_im[..., None] * b_re

    def power(j):
        jf = j.astype(F32)[:, None, None]
        m = jnp.exp(jf * (dt * a_re))
        return m * jnp.cos(jf * (dt * a_im)), m * jnp.sin(jf * (dt * a_im))

    pr, pi = power(jnp.arange(lc + 1))
    ba_re = pr[..., None] * bb_re - pi[..., None] * bb_im
    ba_im = pr[..., None] * bb_im + pi[..., None] * bb_re
    kern = (jnp.einsum("jgpa,gcp->jgac", ba_re, c_re) - jnp.einsum("jgpa,gcp->jgac", ba_im, c_im))
    ti = jnp.arange(lc)
    lag = ti[None, :] - ti[:, None]
    toep = jnp.where((lag >= 0)[None, :, None, :, None],
                     jnp.transpose(kern[jnp.clip(lag, 0, lc)], (2, 0, 3, 1, 4)), 0.0)
    eye = jnp.eye(lc, dtype=F32)[:, None, :, None] * jnp.eye(S5_GROUP, dtype=F32)[None, :, None, :]
    toep = toep + d_skip.reshape(g, 1, S5_GROUP, 1, 1) * eye[None]
    toep = toep.reshape(g, lc * S5_GROUP, lc * S5_GROUP)
    rev = lc - 1 - ti
    wp = jnp.concatenate([ba_re[rev], ba_im[rev]], axis=2)
    wp = jnp.transpose(wp, (1, 0, 3, 2)).reshape(g, lc * S5_GROUP, 2 * p)
    qr, qi = pr[1:], pi[1:]
    q_top = (jnp.einsum("tgp,gcp->gptc", qr, c_re) - jnp.einsum("tgp,gcp->gptc", qi, c_im))
    q_bot = (-jnp.einsum("tgp,gcp->gptc", qi, c_re) - jnp.einsum("tgp,gcp->gptc", qr, c_im))
    q = jnp.concatenate([q_top, q_bot], axis=1).reshape(g, 2 * p, lc * S5_GROUP)
    nsteps = chunks_per_seq.bit_length() - 1
    sr, si = power(lc * (2 ** jnp.arange(nsteps)))
    c1 = jnp.transpose(jnp.concatenate([sr, sr], axis=2), (1, 0, 2))
    c2 = jnp.transpose(jnp.concatenate([-si, si], axis=2), (1, 0, 2))
    return toep.astype(BF16), wp.astype(BF16), q.astype(BF16), c1, c2


def s5_core(u, ops):
    b, s, d = u.shape
    lc = S5_CHUNK
    cps = s // lc
    assert cps & (cps - 1) == 0 and d % LANES == 0
    gs = LANES // S5_GROUP
    width = lc * S5_GROUP
    toep, wp, q, c1, c2 = ops
    nsteps = c1.shape[1]
    i = jnp.arange(lc * LANES)
    dest = (i % LANES // S5_GROUP) * width + (i // LANES) * S5_GROUP + i % S5_GROUP
    perm = (dest[:, None] == i[None, :]).astype(BF16)
    per_tile = lambda shape: pl.BlockSpec((gs,) + shape, lambda ti, bi: (ti, 0, 0))
    act = pl.BlockSpec((1, s, LANES), lambda ti, bi: (bi, 0, ti))
    return pl.pallas_call(
        _s5_kernel,
        grid=(d // LANES, b),
        in_specs=[
            act,
            pl.BlockSpec((lc * LANES, lc * LANES), lambda ti, bi: (0, 0), pipeline_mode=pl.Buffered(1)),
            per_tile((width, width)),
            per_tile((width, 2 * S5_STATE)),
            per_tile((2 * S5_STATE, width)),
            per_tile((nsteps, 2 * S5_STATE)),
            per_tile((nsteps, 2 * S5_STATE)),
        ],
        out_specs=act,
        out_shape=jax.ShapeDtypeStruct((b, s, d), F32),
        compiler_params=_params("parallel", "parallel"),
        name="s5_chunk_scan",
    )(u, perm, toep, wp, q, c1, c2)


def _glu_kernel(z_ref, wv_ref, wg_ref, r_ref, o_ref):
    z = z_ref[...].astype(BF16)
    val = _dot(z, wv_ref[...])
    gate = _dot(z, wg_ref[...])
    o_ref[...] = r_ref[...] + val * jax.nn.sigmoid(gate)


def glu_out(z, w_glu, residual, tm=1024, tn=1024):
    t, k = z.shape
    n = w_glu.shape[1] // 2
    tm, tn = _tile(t, tm), _tile(n, tn)
    nn = n // tn
    return pl.pallas_call(
        _glu_kernel,
        grid=(t // tm, nn),
        in_specs=[
            pl.BlockSpec((tm, k), lambda i, j: (i, 0)),
            pl.BlockSpec((k, tn), lambda i, j: (0, j)),
            pl.BlockSpec((k, tn), lambda i, j: (0, j + nn)),
            pl.BlockSpec((tm, tn), lambda i, j: (i, j)),
        ],
        out_specs=pl.BlockSpec((tm, tn), lambda i, j: (i, j)),
        out_shape=jax.ShapeDtypeStruct((t, n), F32),
        compiler_params=_params("parallel", "parallel"),
        name="glu_out",
    )(z, w_glu, w_glu, residual)


def s5_layer(x, b, s, gain, a_re, a_im, log_dt, b_re, b_im, c_re, c_im, d_skip, w_glu):
    u = rmsnorm(x, gain)
    ops = _s5_operators(a_re, a_im, log_dt, b_re, b_im, c_re, c_im, d_skip, s // S5_CHUNK)
    z = s5_core(u.reshape(b, s, -1), ops)
    return glu_out(z.reshape(x.shape), w_glu.astype(BF16), x)


def _split2(x):
    hi = x.astype(BF16)
    lo = (x - hi.astype(F32)).astype(BF16)
    return hi, lo


def _rwkv_kernel(r_ref, k_ref, v_ref, wz_ref, az_ref, g_ref, kk_ref, ka_ref, rk_ref, lnw_ref, lnb_ref,
                 o_ref, state_ref):
    nb, c, width = r_ref.shape
    unit = RWKV_UNIT
    nu = width // unit
    reps = unit // c

    @pl.when(pl.program_id(1) == 0)
    def _():
        state_ref[...] = jnp.zeros_like(state_ref)

    ri = lax.broadcasted_iota(jnp.int32, (unit, unit), 0)
    ci = lax.broadcasted_iota(jnp.int32, (unit, unit), 1)
    head_bd = (ri // RWKV_HEAD) == (ci // RWKV_HEAD)
    stack_bd = (ri // c) == (ci // RWKV_HEAD)
    chunk_bd = (ri // c) == (ci // c)
    ones_bd = head_bd.astype(BF16)
    eye = (ri == ci).astype(F32)
    tr = lax.broadcasted_iota(jnp.int32, (c, unit), 0)
    tc = lax.broadcasted_iota(jnp.int32, (c, unit), 1) % c
    strict = tc < tr
    incl = tc <= tr
    li = lax.broadcasted_iota(jnp.int32, (c, c), 0)
    lj = lax.broadcasted_iota(jnp.int32, (c, c), 1)
    tri = (lj <= li).astype(BF16)

    def headsums(*xs):
        parts = _dot(jnp.concatenate([p for x in xs for p in _split2(x)], axis=0), ones_bd)
        return [parts[2 * i * c:(2 * i + 1) * c] + parts[(2 * i + 1) * c:(2 * i + 2) * c] for i in range(len(xs))]

    def stack(x, mask):
        return jnp.where(mask, jnp.concatenate([x] * reps, axis=0), 0.0).astype(BF16)

    units = [(b, slice(q * unit, (q + 1) * unit), q) for b in range(nb) for q in range(nu)]
    each = lambda fn, *cols: [fn(*vals) for vals in zip(*cols)]

    def load(ref):
        return [ref[b, :, sl] for b, sl, _ in units]

    def row(ref):
        return [ref[:, sl] for _, sl, _ in units]

    r, k, v, gate, wz, az = (load(ref) for ref in (r_ref, k_ref, v_ref, g_ref, wz_ref, az_ref))
    k_k, k_a, r_k, ln_w, ln_b = (row(ref) for ref in (kk_ref, ka_ref, rk_ref, lnw_ref, lnb_ref))

    logw = each(lambda z: -jnp.exp(-jax.nn.softplus(-z) - 0.5), wz)
    a = each(jax.nn.sigmoid, az)
    kk0 = each(lambda x, y: x * y, k, k_k)
    k2 = each(lambda x, al, ka: x * (1.0 + (al - 1.0) * ka), k, a, k_a)
    sums = each(lambda x, rr, kv, rk: headsums(x * x, rr * kv * rk), kk0, r, k2, r_k)
    kk = each(lambda x, s: x * lax.rsqrt(jnp.maximum(s[0], 1e-24)), kk0, sums)
    bonus = each(lambda s, vv: s[1] * vv, sums, v)
    bvec = each(lambda x, al: x * al, kk, a)

    def cumsum(lw):
        p0 = lw.astype(BF16)
        r1 = lw - p0.astype(F32)
        p1 = r1.astype(BF16)
        p2 = (r1 - p1.astype(F32)).astype(BF16)
        return _dot(tri, p0) + _dot(tri, p1) + _dot(tri, p2)

    cum = each(cumsum, logw)
    cend = each(lambda x: x[c - 1:c, :], cum)
    a_t = each(lambda x, cu, lw: (-x * jnp.exp(cu - lw)).astype(BF16), kk, cum, logw)
    r_t = each(lambda x, cu: (x * jnp.exp(cu)).astype(BF16), r, cum)
    w_inv = each(lambda cu: jnp.exp(-cu), cum)
    w_rest = each(lambda ce, cu: jnp.exp(ce - cu), cend, cum)
    bs = each(lambda x, w: stack(x * w, stack_bd), bvec, w_inv)
    ks = each(lambda x, w: stack(x * w, stack_bd), k2, w_inv)
    vs = each(lambda x: stack(x, stack_bd), v)

    scores = each(lambda at, rt, b_, k_: _nt(jnp.concatenate([at, rt], axis=0), jnp.concatenate([b_, k_], axis=0)),
                  a_t, r_t, bs, ks)
    s_k = each(lambda s: jnp.concatenate([jnp.where(strict, s[:c, unit:], 0.0), jnp.where(incl, s[c:, unit:], 0.0)],
                                         axis=0).astype(BF16), scores)
    s_rb = each(lambda s: jnp.where(incl, s[c:, :unit], 0.0).astype(BF16), scores)

    pw = each(lambda s: jnp.where(chunk_bd, jnp.concatenate([jnp.where(strict, s[:c, :unit], 0.0)] * reps, axis=0),
                                  0.0), scores)
    tm = each(lambda p: eye + p, pw)
    for _ in range(c.bit_length() - 2):
        pw = each(lambda p: _dot(p.astype(BF16), p.astype(BF16)), pw)
        tm = each(lambda t, p: t + _dot(t.astype(BF16), p.astype(BF16)), tm, pw)
    t_cat = each(lambda t: sum(t[h * c:(h + 1) * c] for h in range(1, reps)) + t[0:c], tm)

    state = [state_ref[b, q] for b, _, q in units]
    state_b = each(lambda s: s.astype(BF16), state)
    base = each(lambda at, rt, sb, sk, vv: _nt(jnp.concatenate([at, rt], axis=0), sb) + _dot(sk, vv),
                a_t, r_t, state_b, s_k, vs)
    u = each(lambda t, bb: _dot(t.astype(BF16), stack(bb[:c], stack_bd)), t_cat, base)
    y = each(lambda bb, srb, uu: bb[c:] + _dot(srb, stack(uu, stack_bd)), base, s_rb, u)
    upd = each(lambda uu, vv, bv, kv, wr: _tn(jnp.concatenate([uu, vv], axis=0).astype(BF16),
                                               jnp.concatenate([bv * wr, kv * wr], axis=0).astype(BF16)),
               u, v, bvec, k2, w_rest)
    for (b, _, q), s, ce, up in zip(units, state, cend, upd):
        state_ref[b, q] = jnp.where(head_bd, s * jnp.exp(ce) + up, 0.0)

    inv_n = 1.0 / RWKV_HEAD
    mean = each(lambda yy: headsums(yy)[0] * inv_n, y)
    yc = each(lambda yy, m: yy - m, y, mean)
    var = each(lambda z: headsums(z * z)[0] * inv_n, yc)
    for (b, sl, _), z, vr, lw, lb, bo, gt in zip(units, yc, var, ln_w, ln_b, bonus, gate):
        o_ref[b, :, sl] = ((z * lax.rsqrt(vr + RWKV_GN_EPS) * lw + lb + bo) * gt).astype(o_ref.dtype)


def rwkv_core(r, k, v, wz, az, gate, k_k, k_a, r_k, ln_w, ln_b, units_per_step=4):
    b, s, d = r.shape
    c = RWKV_CHUNK
    assert s % c == 0 and RWKV_UNIT % c == 0
    width = _tile(d, RWKV_UNIT * units_per_step)
    nu = width // RWKV_UNIT
    act = pl.BlockSpec((b, c, width), lambda i, j: (0, j, i))
    row = pl.BlockSpec((1, width), lambda i, j: (0, i))
    rows = [t.reshape(1, d).astype(F32) for t in (k_k, k_a, r_k, ln_w, ln_b)]
    return pl.pallas_call(
        _rwkv_kernel,
        grid=(d // width, s // c),
        in_specs=[act] * 6 + [row] * 5,
        out_specs=act,
        out_shape=jax.ShapeDtypeStruct((b, s, d), BF16),
        scratch_shapes=[pltpu.VMEM((b, nu, RWKV_UNIT, RWKV_UNIT), F32)],
        compiler_params=_params("parallel", "arbitrary"),
        name="rwkv7_chunk",
    )(r, k, v, wz, az, gate, *rows)


def _rwkv_mix_kernel(x_ref, xp_ref, g_ref, mu_ref, *o_refs):
    gain = g_ref[...]
    hn = _rms(x_ref[0], gain)
    tail = xp_ref.shape[1]
    prev_last = _rms(xp_ref[0, tail - 1:tail, :], gain)
    prev_last = jnp.where(pl.program_id(1) > 0, prev_last, 0.0)
    row = lax.broadcasted_iota(jnp.int32, hn.shape, 0)
    xx = jnp.where(row == 0, prev_last, pltpu.roll(hn, 1, 0)) - hn
    for i, o_ref in enumerate(o_refs):
        o_ref[0] = (hn + xx * mu_ref[i:i + 1, :]).astype(o_ref.dtype)


def rwkv_mix(x, gain, mu, tm=512, tail=8):
    b, s, d = x.shape
    tm = _tile(s, tm)
    n = mu.shape[0]
    out = pl.BlockSpec((1, tm, d), lambda bi, i: (bi, i, 0))
    return pl.pallas_call(
        _rwkv_mix_kernel,
        grid=(b, s // tm),
        in_specs=[out,
                  pl.BlockSpec((1, tail, d), lambda bi, i: (bi, jnp.maximum(i * (tm // tail) - 1, 0), 0)),
                  pl.BlockSpec((1, d), lambda bi, i: (0, 0)),
                  pl.BlockSpec((n, d), lambda bi, i: (0, 0))],
        out_specs=[out] * n,
        out_shape=[jax.ShapeDtypeStruct((b, s, d), BF16)] * n,
        compiler_params=_params("parallel", "parallel"),
        name="rwkv_mix",
    )(x, x, gain.reshape(1, d), mu.astype(F32))


def _pad_cols(w, mult=128):
    n = w.shape[1]
    return jnp.pad(w, ((0, 0), (0, (-n) % mult)))


def _pad_rows(w, mult=128):
    n = w.shape[0]
    return jnp.pad(w, ((0, (-n) % mult), (0, 0)))


def rwkv_layer(x, b, s, gain, mu, w_r, w_k, w_v, w0, w_w1, w_w2, a0, a_w1, a_w2, g_w1, g_w2,
               k_k, k_a, r_k, ln_w, ln_b, w_o):
    t, d = x.shape
    xr, xw, xk, xv, xa, xg = (m.reshape(t, d) for m in rwkv_mix(x.reshape(b, s, d), gain, mu))
    bf = lambda w: w.astype(BF16)
    r = matmul(xr, bf(w_r))
    k = matmul(xk, bf(w_k))
    v = matmul(xv, bf(w_v))
    hw = matmul(xw, bf(_pad_cols(w_w1)), act="tanh", out_dtype=BF16, tn=128)
    wz = matmul(hw, bf(_pad_rows(w_w2)), bias=w0)
    ha = matmul(xa, bf(_pad_cols(a_w1)), out_dtype=BF16, tn=128)
    az = matmul(ha, bf(_pad_rows(a_w2)), bias=a0)
    hg = matmul(xg, bf(_pad_cols(g_w1)), act="sigmoid", out_dtype=BF16, tn=256)
    gate = matmul(hg, bf(_pad_rows(g_w2)))
    sh = lambda z: z.reshape(b, s, d)
    y = rwkv_core(sh(r), sh(k), sh(v), sh(wz), sh(az), sh(gate), k_k, k_a, r_k, ln_w, ln_b)
    return matmul(y.reshape(t, d), bf(w_o), residual=x)


def kernel(x, positions, l0_norm_mix, l0_attn_w_in, l0_attn_w_out, l0_norm_ffn, l0_ffn_w_in, l0_ffn_w_out, l1_norm_mix, l1_s5_a_re, l1_s5_a_im, l1_s5_log_dt, l1_s5_b_re, l1_s5_b_im, l1_s5_c_re, l1_s5_c_im, l1_s5_d, l1_s5_w_glu, l1_norm_ffn, l1_ffn_w_in, l1_ffn_w_out, l2_norm_mix, l2_rwkv_mu, l2_rwkv_w_r, l2_rwkv_w_k, l2_rwkv_w_v, l2_rwkv_w0, l2_rwkv_w_w1, l2_rwkv_w_w2, l2_rwkv_a0, l2_rwkv_a_w1, l2_rwkv_a_w2, l2_rwkv_g_w1, l2_rwkv_g_w2, l2_rwkv_k_k, l2_rwkv_k_a, l2_rwkv_r_k, l2_rwkv_ln_w, l2_rwkv_ln_b, l2_rwkv_w_o, l2_norm_ffn, l2_ffn_w_in, l2_ffn_w_out, l3_norm_mix, l3_attn_w_in, l3_attn_w_out, l3_norm_ffn, l3_ffn_w_in, l3_ffn_w_out, final_norm):
    b, s, d = x.shape
    h = x.reshape(b * s, d)
    rope = _rope_tables(positions)

    def channel_mixer(h, gain, w_in, w_out, final_gain=None):
        return ffn(h, gain, w_in.astype(BF16), w_out.astype(BF16), final_gain)

    h = attention_layer(h, b, s, rope, l0_norm_mix, l0_attn_w_in, l0_attn_w_out)
    h = channel_mixer(h, l0_norm_ffn, l0_ffn_w_in, l0_ffn_w_out)
    h = s5_layer(h, b, s, l1_norm_mix, l1_s5_a_re, l1_s5_a_im, l1_s5_log_dt, l1_s5_b_re, l1_s5_b_im,
                 l1_s5_c_re, l1_s5_c_im, l1_s5_d, l1_s5_w_glu)
    h = channel_mixer(h, l1_norm_ffn, l1_ffn_w_in, l1_ffn_w_out)
    h = rwkv_layer(h, b, s, l2_norm_mix, l2_rwkv_mu, l2_rwkv_w_r, l2_rwkv_w_k, l2_rwkv_w_v, l2_rwkv_w0,
                   l2_rwkv_w_w1, l2_rwkv_w_w2, l2_rwkv_a0, l2_rwkv_a_w1, l2_rwkv_a_w2, l2_rwkv_g_w1,
                   l2_rwkv_g_w2, l2_rwkv_k_k, l2_rwkv_k_a, l2_rwkv_r_k, l2_rwkv_ln_w, l2_rwkv_ln_b,
                   l2_rwkv_w_o)
    h = channel_mixer(h, l2_norm_ffn, l2_ffn_w_in, l2_ffn_w_out)
    h = attention_layer(h, b, s, rope, l3_norm_mix, l3_attn_w_in, l3_attn_w_out)
    h = channel_mixer(h, l3_norm_ffn, l3_ffn_w_in, l3_ffn_w_out, final_gain=final_norm)
    return h.reshape(b, s, d)
```

```python
import functools
import math

import jax
import jax.numpy as jnp
from jax import lax
from jax.experimental import pallas as pl
from jax.experimental.pallas import tpu as pltpu

F32 = jnp.float32
BF16 = jnp.bfloat16

RMS_EPS = 1e-6
LANES = 128
VMEM_LIMIT_BYTES = 56 * 1024 * 1024

ATTN_HEAD_DIM = 128
ATTN_HEADS = 8
ATTN_GROUPS = ((128, 1), (512, 4), (2048, 16))
ATTN_BLOCK = 128
ROPE_THETA = 500000.0
ROPE_DIM = ATTN_HEAD_DIM // 4
ATTN_WIDTH = ATTN_HEADS * ATTN_HEAD_DIM
QKV_COL_CHUNK = 256
NEG_BIG = -1e30

S5_GROUP = 16
S5_STATE = 64
S5_CHUNK = 16

RWKV_HEAD = 64
RWKV_CHUNK = 64
RWKV_HEADS_PER_UNIT = 4
RWKV_UNIT = RWKV_HEADS_PER_UNIT * RWKV_HEAD
RWKV_GN_EPS = 64e-5


def _params(*sem):
    return pltpu.CompilerParams(dimension_semantics=sem, vmem_limit_bytes=VMEM_LIMIT_BYTES)


def _tile(n, pref):
    t = min(n, pref)
    assert n % t == 0, (n, pref)
    return t


def _rms(x, g):
    return x * lax.rsqrt(jnp.mean(x * x, axis=-1, keepdims=True) + RMS_EPS) * g


def _nt(a, b):
    return lax.dot_general(a, b, (((1,), (1,)), ((), ())), preferred_element_type=F32)


def _tn(a, b):
    return lax.dot_general(a, b, (((0,), (0,)), ((), ())), preferred_element_type=F32)


def _dot(a, b):
    return jnp.dot(a, b, preferred_element_type=F32)


def _rmsnorm_kernel(x_ref, g_ref, o_ref):
    o_ref[...] = _rms(x_ref[...], g_ref[...]).astype(o_ref.dtype)


def rmsnorm(x, gain, out_dtype=F32, tm=512):
    t, d = x.shape
    tm = _tile(t, tm)
    return pl.pallas_call(
        _rmsnorm_kernel,
        grid=(t // tm,),
        in_specs=[pl.BlockSpec((tm, d), lambda i: (i, 0)), pl.BlockSpec((1, d), lambda i: (0, 0))],
        out_specs=pl.BlockSpec((tm, d), lambda i: (i, 0)),
        out_shape=jax.ShapeDtypeStruct((t, d), out_dtype),
        compiler_params=_params("parallel"),
        name="rmsnorm",
    )(x, gain.reshape(1, d))


def _mm_kernel(x_ref, w_ref, *refs):
    *r_ref, o_ref = refs
    acc = _dot(x_ref[...], w_ref[...])
    if r_ref:
        acc = acc + r_ref[0][...]
    o_ref[...] = acc.astype(o_ref.dtype)


def matmul(x, w, *, residual=None, out_dtype=F32, tm=1024, tn=1024):
    t, k = x.shape
    n = w.shape[1]
    tm, tn = _tile(t, tm), _tile(n, tn)
    ins = [x, w]
    specs = [pl.BlockSpec((tm, k), lambda i, j: (i, 0)), pl.BlockSpec((k, tn), lambda i, j: (0, j))]
    if residual is not None:
        ins.append(residual)
        specs.append(pl.BlockSpec((tm, tn), lambda i, j: (i, j)))
    return pl.pallas_call(
        _mm_kernel,
        grid=(t // tm, n // tn),
        in_specs=specs,
        out_specs=pl.BlockSpec((tm, tn), lambda i, j: (i, j)),
        out_shape=jax.ShapeDtypeStruct((t, n), out_dtype),
        compiler_params=_params("parallel", "parallel"),
        name="matmul",
    )(*ins)


def _ffn_kernel(x_ref, g_ref, wg_ref, wu_ref, wo_ref, fg_ref, o_ref, xn_ref, *, final_norm):
    j = pl.program_id(1)

    @pl.when(j == 0)
    def _():
        x = x_ref[...]
        xn_ref[...] = _rms(x, g_ref[...]).astype(BF16)
        o_ref[...] = x

    xn = xn_ref[...]
    gate = _dot(xn, wg_ref[...])
    up = _dot(xn, wu_ref[...])
    h = (gate * jax.nn.sigmoid(gate) * up).astype(BF16)
    o_ref[...] += _dot(h, wo_ref[...])

    if final_norm:
        @pl.when(j == pl.num_programs(1) - 1)
        def _():
            o_ref[...] = _rms(o_ref[...], fg_ref[...])


def ffn(x, gain, w_in, w_out, final_gain=None, tm=512, tf=512):
    t, d = x.shape
    f = w_out.shape[0]
    tm, tf = _tile(t, tm), _tile(f, tf)
    nf = f // tf
    fg = gain if final_gain is None else final_gain
    return pl.pallas_call(
        functools.partial(_ffn_kernel, final_norm=final_gain is not None),
        grid=(t // tm, nf),
        in_specs=[
            pl.BlockSpec((tm, d), lambda i, j: (i, 0)),
            pl.BlockSpec((1, d), lambda i, j: (0, 0)),
            pl.BlockSpec((d, tf), lambda i, j: (0, j)),
            pl.BlockSpec((d, tf), lambda i, j: (0, j + nf)),
            pl.BlockSpec((tf, d), lambda i, j: (j, 0)),
            pl.BlockSpec((1, d), lambda i, j: (0, 0)),
        ],
        out_specs=pl.BlockSpec((tm, d), lambda i, j: (i, 0)),
        out_shape=jax.ShapeDtypeStruct((t, d), F32),
        scratch_shapes=[pltpu.VMEM((tm, d), BF16)],
        compiler_params=_params("parallel", "arbitrary"),
        name="ffn",
    )(x, gain.reshape(1, d), w_in, w_in, w_out, fg.reshape(1, d))


def _norm_residues_kernel(x_ref, g_ref, *refs):
    out_refs, hn_ref = refs[:-1], refs[-1]
    hn = _rms(x_ref[0], g_ref[...])
    ncol, tm, lanes = hn_ref.shape
    for c in range(ncol):
        hn_ref[c] = hn[:, c * lanes:(c + 1) * lanes]
    for o_ref in out_refs:
        d = o_ref.shape[1]
        if d == 1:
            o_ref[0, 0] = hn.astype(o_ref.dtype)
            continue
        for r in range(d):
            for c in range(ncol):
                o_ref[0, r, :, c * lanes:(c + 1) * lanes] = (
                    hn_ref[c, pl.ds(r, tm // d, stride=d), :].astype(o_ref.dtype))


def norm_by_residue(x, gain, dilations, tm=512):
    b, s, d_model = x.shape
    tm = _tile(s, tm)
    return pl.pallas_call(
        _norm_residues_kernel,
        grid=(b, s // tm),
        in_specs=[pl.BlockSpec((1, tm, d_model), lambda bi, i: (bi, i, 0)),
                  pl.BlockSpec((1, d_model), lambda bi, i: (0, 0))],
        out_specs=[pl.BlockSpec((1, d, tm // d, d_model), lambda bi, i: (bi, 0, i, 0)) for d in dilations],
        out_shape=[jax.ShapeDtypeStruct((b, d, s // d, d_model), BF16) for d in dilations],
        scratch_shapes=[pltpu.VMEM((d_model // LANES, tm, LANES), F32)],
        compiler_params=_params("parallel", "parallel"),
        name="norm_by_residue",
    )(x, gain.reshape(1, d_model))


def _qkv_kernel(x_ref, w_ref, cos_ref, sin_ref, o_ref):
    kind = pl.program_id(1)
    scale = jnp.where(kind == 0, ATTN_HEAD_DIM ** -0.5, 1.0).astype(F32)
    is_v = kind == 2
    cos = jnp.where(is_v, 1.0, cos_ref[...] * scale)
    sin = jnp.where(is_v, 0.0, sin_ref[...] * scale)
    lane = lax.broadcasted_iota(jnp.int32, cos.shape, 1)
    half = ROPE_DIM // 2
    x = x_ref[...]
    for c0 in range(0, o_ref.shape[1], QKV_COL_CHUNK):
        acc = _dot(x, w_ref[:, c0:c0 + QKV_COL_CHUNK])
        for h0 in range(0, QKV_COL_CHUNK, ATTN_HEAD_DIM):
            xh = acc[:, h0:h0 + ATTN_HEAD_DIM]
            rot = jnp.where(lane < half, pltpu.roll(xh, ATTN_HEAD_DIM - half, 1), pltpu.roll(xh, half, 1))
            o_ref[:, c0 + h0:c0 + h0 + ATTN_HEAD_DIM] = (xh * cos + rot * sin).astype(o_ref.dtype)


def qkv_projection(x, w_in, gi, cos_t, sin_t, tm=1024):
    t, d = x.shape
    tm = _tile(t, tm)
    tn = ATTN_WIDTH
    return pl.pallas_call(
        _qkv_kernel,
        grid=(t // tm, 3),
        in_specs=[
            pl.BlockSpec((tm, d), lambda i, j: (i, 0)),
            pl.BlockSpec((d, tn), lambda i, j: (0, gi * 3 + j)),
            pl.BlockSpec((tm, ATTN_HEAD_DIM), lambda i, j: (i, 0)),
            pl.BlockSpec((tm, ATTN_HEAD_DIM), lambda i, j: (i, 0)),
        ],
        out_specs=pl.BlockSpec((tm, tn), lambda i, j: (i, j)),
        out_shape=jax.ShapeDtypeStruct((t, 3 * tn), BF16),
        compiler_params=_params("parallel", "parallel"),
        name="qkv_rope",
    )(x, w_in, cos_t, sin_t)


def _attn_kernel(q_ref, kp_ref, kc_ref, vp_ref, vc_ref, o_ref, st_ref):
    n = pl.program_id(1)
    blk = ATTN_BLOCK
    qi = lax.broadcasted_iota(jnp.int32, (blk, 2 * blk), 0)
    kj = lax.broadcasted_iota(jnp.int32, (blk, 2 * blk), 1)
    dist = qi + blk - kj
    valid = (dist >= 0) & (dist <= blk) & ((kj >= blk) | (n > 0))
    lane = lax.broadcasted_iota(jnp.int32, (blk, ATTN_HEAD_DIM), 1)
    stats = jnp.zeros((blk, ATTN_HEAD_DIM), F32)
    for h in range(ATTN_HEADS):
        sl = slice(h * ATTN_HEAD_DIM, (h + 1) * ATTN_HEAD_DIM)
        q = q_ref[0, :, sl]
        k = jnp.concatenate([kp_ref[0, :, sl], kc_ref[0, :, sl]], axis=0)
        v = jnp.concatenate([vp_ref[0, :, sl], vc_ref[0, :, sl]], axis=0)
        s = jnp.where(valid, _nt(q, k), NEG_BIG)
        mx = jnp.max(s, axis=-1, keepdims=True)
        p = jnp.exp(s - mx)
        den = jnp.sum(p, axis=-1, keepdims=True)
        num = _dot(p.astype(BF16), v)
        o_ref[0, :, sl] = (num / den).astype(o_ref.dtype)
        stats = jnp.where(lane == h, mx + jnp.log(den), stats)
    st_ref[0] = stats


def attention_group(qkv):
    nseq, length, _ = qkv.shape
    assert length % ATTN_BLOCK == 0
    nb = length // ATTN_BLOCK
    blk = (1, ATTN_BLOCK, ATTN_WIDTH)

    def cur(off):
        return pl.BlockSpec(blk, lambda r, nn: (r, nn, off))

    def prev(off):
        return pl.BlockSpec(blk, lambda r, nn: (r, jnp.maximum(nn - 1, 0), off))

    return pl.pallas_call(
        _attn_kernel,
        grid=(nseq, nb),
        in_specs=[cur(0), prev(1), cur(1), prev(2), cur(2)],
        out_specs=[
            pl.BlockSpec(blk, lambda r, nn: (r, nn, 0)),
            pl.BlockSpec((1, ATTN_BLOCK, ATTN_HEAD_DIM), lambda r, nn: (r, nn, 0)),
        ],
        out_shape=[
            jax.ShapeDtypeStruct((nseq, length, ATTN_WIDTH), BF16),
            jax.ShapeDtypeStruct((nseq, length, ATTN_HEAD_DIM), F32),
        ],
        compiler_params=_params("parallel", "parallel"),
        name="dilated_attn",
    )(qkv, qkv, qkv, qkv, qkv)


def _attn_out_kernel(*refs):
    ng = len(ATTN_GROUPS)
    o_refs, s_refs = refs[:ng], refs[ng:2 * ng]
    w_ref, r_ref, out_ref, om_ref, ou_ref, su_ref = refs[2 * ng:]

    @pl.when(pl.program_id(2) == 0)
    def _():
        tm = om_ref.shape[0]
        for g in range(ng):
            d = o_refs[g].shape[1]
            for r in range(d):
                rows = pl.ds(r, tm // d, stride=d)
                su_ref[g, rows, :] = s_refs[g][0, r]
                for h in range(ATTN_HEADS):
                    sl = slice(h * ATTN_HEAD_DIM, (h + 1) * ATTN_HEAD_DIM)
                    ou_ref[g, h, rows, :] = o_refs[g][0, r, :, sl].astype(F32)
        lse = [su_ref[g] for g in range(ng)]
        m = functools.reduce(jnp.maximum, lse)
        e = [jnp.exp(l - m) for l in lse]
        inv = 1.0 / sum(e)
        for h in range(ATTN_HEADS):
            om = sum((e[g] * inv)[:, h:h + 1] * ou_ref[g, h] for g in range(ng))
            om_ref[:, h * ATTN_HEAD_DIM:(h + 1) * ATTN_HEAD_DIM] = om.astype(BF16)

    out_ref[0] = r_ref[0] + _dot(om_ref[...], w_ref[...])


def attention_out(parts, w_out, residual, tm=512, tn=1024):
    b, s, n = residual.shape
    k = w_out.shape[0]
    tm, tn = _tile(s, tm), _tile(n, tn)
    ng = len(parts)

    def by_residue(arr):
        d, width = arr.shape[1], arr.shape[3]
        return pl.BlockSpec((1, d, tm // d, width), lambda bi, i, j: (bi, 0, i, 0))

    res = pl.BlockSpec((1, tm, tn), lambda bi, i, j: (bi, i, j))
    return pl.pallas_call(
        _attn_out_kernel,
        grid=(b, s // tm, n // tn),
        in_specs=[by_residue(o) for o, _ in parts] + [by_residue(st) for _, st in parts]
        + [pl.BlockSpec((k, tn), lambda bi, i, j: (0, j)), res],
        out_specs=res,
        out_shape=jax.ShapeDtypeStruct((b, s, n), F32),
        scratch_shapes=[pltpu.VMEM((tm, k), BF16),
                        pltpu.VMEM((ng, ATTN_HEADS, tm, ATTN_HEAD_DIM), F32),
                        pltpu.VMEM((ng, tm, ATTN_HEAD_DIM), F32)],
        compiler_params=_params("parallel", "parallel", "arbitrary"),
        name="attn_merge_out",
    )(*[o for o, _ in parts], *[st for _, st in parts], w_out, residual)


def _by_residue(t, dilation):
    b, s = t.shape[:2]
    return jnp.swapaxes(t.reshape(b, s // dilation, dilation, *t.shape[2:]), 1, 2)


def _rope_tables(positions):
    half = ROPE_DIM // 2
    inv_freq = ROPE_THETA ** (-jnp.arange(half, dtype=F32) * 2.0 / ROPE_DIM)
    pad = ATTN_HEAD_DIM - ROPE_DIM
    tables = []
    for _, dil in ATTN_GROUPS:
        ang = _by_residue(positions, dil).astype(F32).reshape(-1, 1) * inv_freq
        cos, sin = jnp.cos(ang), jnp.sin(ang)
        tables.append((jnp.concatenate([cos, cos, jnp.ones((ang.shape[0], pad), F32)], axis=1),
                       jnp.concatenate([-sin, sin, jnp.zeros((ang.shape[0], pad), F32)], axis=1)))
    return tables


def attention_layer(x, b, s, rope, gain, w_in, w_out):
    t, d = x.shape
    w_in = w_in.astype(BF16)
    dils = [dil for _, dil in ATTN_GROUPS]
    hns = norm_by_residue(x.reshape(b, s, d), gain, dils)
    parts = []
    for gi, (dil, hn, (cos_t, sin_t)) in enumerate(zip(dils, hns, rope)):
        qkv = qkv_projection(hn.reshape(t, d), w_in, gi, cos_t, sin_t)
        o, st = attention_group(qkv.reshape(b * dil, s // dil, -1))
        parts.append((o.reshape(b, dil, s // dil, -1), st.reshape(b, dil, s // dil, -1)))
    return attention_out(parts, w_out.astype(BF16), x.reshape(b, s, d)).reshape(t, d)


def _gelu_tanh(y):
    return 0.5 * y * (1.0 + jnp.tanh(math.sqrt(2.0 / math.pi) * (y + 0.044715 * (y * y * y))))


def _s5_kernel(u_ref, perm_ref, toep_ref, wp_ref, q_ref, c1_ref, c2_ref, z_ref):
    lc = S5_CHUNK
    groups = toep_ref.shape[0]
    nch = u_ref.shape[1] // lc
    width = lc * S5_GROUP
    nsteps = c1_ref.shape[1]
    perm = perm_ref[...]
    by_pos = jnp.concatenate([u_ref[0, pl.ds(s, nch, stride=lc), :].astype(BF16) for s in range(lc)], axis=1)
    by_group = _dot(by_pos, perm).astype(BF16)
    row = lax.broadcasted_iota(jnp.int32, (nch, 2 * S5_STATE), 0)
    zs = []
    for g in range(groups):
        u = by_group[:, g * width:(g + 1) * width]
        y = _dot(u, toep_ref[g])
        x = _dot(u, wp_ref[g])
        for i in range(nsteps):
            m = 1 << i
            sh = jnp.where(row >= m, pltpu.roll(x, m, 0), 0.0)
            x = x + c1_ref[g, i:i + 1, :] * sh + c2_ref[g, i:i + 1, :] * pltpu.roll(sh, S5_STATE, 1)
        xprev = jnp.where(row >= 1, pltpu.roll(x, 1, 0), 0.0)
        y = y + _dot(xprev.astype(BF16), q_ref[g])
        zs.append(_gelu_tanh(y).astype(BF16))
    out = _nt(jnp.concatenate(zs, axis=1), perm)
    for s in range(lc):
        z_ref[0, pl.ds(s, nch, stride=lc), :] = out[:, s * LANES:(s + 1) * LANES]


def _s5_operators(a_re, a_im, log_dt, b_re, b_im, c_re, c_im, d_skip, chunks_per_seq):
    lc = S5_CHUNK
    g, p = a_re.shape
    dt = jnp.exp(log_dt)[:, None]
    mag = jnp.exp(dt * a_re)
    ab_re = mag * jnp.cos(dt * a_im)
    ab_im = mag * jnp.sin(dt * a_im)
    inv = 1.0 / (a_re * a_re + a_im * a_im)
    f_re = ((ab_re - 1.0) * a_re + ab_im * a_im) * inv
    f_im = (ab_im * a_re - (ab_re - 1.0) * a_im) * inv
    bb_re = f_re[..., None] * b_re - f_im[..., None] * b_im
    bb_im = f_re[..., None] * b_im + f_im[..., None] * b_re

    def power(j):
        jf = j.astype(F32)[:, None, None]
        m = jnp.exp(jf * (dt * a_re))
        return m * jnp.cos(jf * (dt * a_im)), m * jnp.sin(jf * (dt * a_im))

    pr, pi = power(jnp.arange(lc + 1))
    ba_re = pr[..., None] * bb_re - pi[..., None] * bb_im
    ba_im = pr[..., None] * bb_im + pi[..., None] * bb_re
    kern = (jnp.einsum("jgpa,gcp->jgac", ba_re, c_re) - jnp.einsum("jgpa,gcp->jgac", ba_im, c_im))
    ti = jnp.arange(lc)
    lag = ti[None, :] - ti[:, None]
    toep = jnp.where((lag >= 0)[None, :, None, :, None],
                     jnp.transpose(kern[jnp.clip(lag, 0, lc)], (2, 0, 3, 1, 4)), 0.0)
    eye = jnp.eye(lc, dtype=F32)[:, None, :, None] * jnp.eye(S5_GROUP, dtype=F32)[None, :, None, :]
    toep = toep + d_skip.reshape(g, 1, S5_GROUP, 1, 1) * eye[None]
    toep = toep.reshape(g, lc * S5_GROUP, lc * S5_GROUP)
    rev = lc - 1 - ti
    wp = jnp.concatenate([ba_re[rev], ba_im[rev]], axis=2)
    wp = jnp.transpose(wp, (1, 0, 3, 2)).reshape(g, lc * S5_GROUP, 2 * p)
    qr, qi = pr[1:], pi[1:]
    q_top = (jnp.einsum("tgp,gcp->gptc", qr, c_re) - jnp.einsum("tgp,gcp->gptc", qi, c_im))
    q_bot = (-jnp.einsum("tgp,gcp->gptc", qi, c_re) - jnp.einsum("tgp,gcp->gptc", qr, c_im))
    q = jnp.concatenate([q_top, q_bot], axis=1).reshape(g, 2 * p, lc * S5_GROUP)
    nsteps = chunks_per_seq.bit_length() - 1
    sr, si = power(lc * (2 ** jnp.arange(nsteps)))
    c1 = jnp.transpose(jnp.concatenate([sr, sr], axis=2), (1, 0, 2))
    c2 = jnp.transpose(jnp.concatenate([-si, si], axis=2), (1, 0, 2))
    return toep.astype(BF16), wp.astype(BF16), q.astype(BF16), c1, c2


def s5_core(u, ops):
    b, s, d = u.shape
    lc = S5_CHUNK
    cps = s // lc
    assert cps & (cps - 1) == 0 and d % LANES == 0
    gs = LANES // S5_GROUP
    width = lc * S5_GROUP
    toep, wp, q, c1, c2 = ops
    nsteps = c1.shape[1]
    i = jnp.arange(lc * LANES)
    dest = (i % LANES // S5_GROUP) * width + (i // LANES) * S5_GROUP + i % S5_GROUP
    perm = (dest[:, None] == i[None, :]).astype(BF16)
    per_tile = lambda shape: pl.BlockSpec((gs,) + shape, lambda ti, bi: (ti, 0, 0))
    act = pl.BlockSpec((1, s, LANES), lambda ti, bi: (bi, 0, ti))
    return pl.pallas_call(
        _s5_kernel,
        grid=(d // LANES, b),
        in_specs=[
            act,
            pl.BlockSpec((lc * LANES, lc * LANES), lambda ti, bi: (0, 0), pipeline_mode=pl.Buffered(1)),
            per_tile((width, width)),
            per_tile((width, 2 * S5_STATE)),
            per_tile((2 * S5_STATE, width)),
            per_tile((nsteps, 2 * S5_STATE)),
            per_tile((nsteps, 2 * S5_STATE)),
        ],
        out_specs=act,
        out_shape=jax.ShapeDtypeStruct((b, s, d), F32),
        compiler_params=_params("parallel", "parallel"),
        name="s5_chunk_scan",
    )(u, perm, toep, wp, q, c1, c2)


def _glu_kernel(z_ref, wv_ref, wg_ref, r_ref, o_ref):
    z = z_ref[...].astype(BF16)
    val = _dot(z, wv_ref[...])
    gate = _dot(z, wg_ref[...])
    o_ref[...] = r_ref[...] + val * jax.nn.sigmoid(gate)


def glu_out(z, w_glu, residual, tm=1024, tn=1024):
    t, k = z.shape
    n = w_glu.shape[1] // 2
    tm, tn = _tile(t, tm), _tile(n, tn)
    nn = n // tn
    return pl.pallas_call(
        _glu_kernel,
        grid=(t // tm, nn),
        in_specs=[
            pl.BlockSpec((tm, k), lambda i, j: (i, 0)),
            pl.BlockSpec((k, tn), lambda i, j: (0, j)),
            pl.BlockSpec((k, tn), lambda i, j: (0, j + nn)),
            pl.BlockSpec((tm, tn), lambda i, j: (i, j)),
        ],
        out_specs=pl.BlockSpec((tm, tn), lambda i, j: (i, j)),
        out_shape=jax.ShapeDtypeStruct((t, n), F32),
        compiler_params=_params("parallel", "parallel"),
        name="glu_out",
    )(z, w_glu, w_glu, residual)


def s5_layer(x, b, s, gain, a_re, a_im, log_dt, b_re, b_im, c_re, c_im, d_skip, w_glu):
    u = rmsnorm(x, gain)
    ops = _s5_operators(a_re, a_im, log_dt, b_re, b_im, c_re, c_im, d_skip, s // S5_CHUNK)
    z = s5_core(u.reshape(b, s, -1), ops)
    return glu_out(z.reshape(x.shape), w_glu.astype(BF16), x)


def _split2(x):
    hi = x.astype(BF16)
    lo = (x - hi.astype(F32)).astype(BF16)
    return hi, lo


def _rwkv_kernel(r_ref, k_ref, v_ref, hw_ref, ha_ref, hg_ref, ww2_ref, aw2_ref, gw2_ref, w0_ref, a0_ref,
                 kk_ref, ka_ref, rk_ref, lnw_ref, lnb_ref, o_ref, state_ref):
    nb, c, width = r_ref.shape
    unit = RWKV_UNIT
    nu = width // unit
    reps = unit // c

    @pl.when(pl.program_id(1) == 0)
    def _():
        state_ref[...] = jnp.zeros_like(state_ref)

    ri = lax.broadcasted_iota(jnp.int32, (unit, unit), 0)
    ci = lax.broadcasted_iota(jnp.int32, (unit, unit), 1)
    head_bd = (ri // RWKV_HEAD) == (ci // RWKV_HEAD)
    stack_bd = (ri // c) == (ci // RWKV_HEAD)
    chunk_bd = (ri // c) == (ci // c)
    ones_bd = head_bd.astype(BF16)
    tr = lax.broadcasted_iota(jnp.int32, (c, unit), 0)
    tc = lax.broadcasted_iota(jnp.int32, (c, unit), 1) % c
    eye_cat = (tc == tr).astype(F32)
    strict = tc < tr
    incl = tc <= tr
    li = lax.broadcasted_iota(jnp.int32, (c, c), 0)
    lj = lax.broadcasted_iota(jnp.int32, (c, c), 1)
    tri = (lj <= li).astype(BF16)

    def headsums(*cols):
        pieces = [p for xs in zip(*cols) for x in xs for p in _split2(x)]
        parts = _dot(jnp.concatenate(pieces, axis=0), ones_bd)
        sums = [parts[2 * i * c:(2 * i + 1) * c] + parts[(2 * i + 1) * c:(2 * i + 2) * c]
                for i in range(len(pieces) // 2)]
        return [sums[i::len(cols)] for i in range(len(cols))]

    def stack(x, mask):
        return jnp.where(mask, jnp.concatenate([x] * reps, axis=0), 0.0).astype(BF16)

    units = [(b, slice(q * unit, (q + 1) * unit), q) for b in range(nb) for q in range(nu)]
    each = lambda fn, *cols: [fn(*vals) for vals in zip(*cols)]

    def load(ref):
        return [ref[b, :, sl] for b, sl, _ in units]

    def row(ref):
        return [ref[:, sl] for _, sl, _ in units]

    r, k, v = (load(ref) for ref in (r_ref, k_ref, v_ref))
    k_k, k_a, r_k, ln_w, ln_b = (row(ref) for ref in (kk_ref, ka_ref, rk_ref, lnw_ref, lnb_ref))

    def low_rank(h_ref, w2_ref):
        wide = [_dot(h_ref[b], w2_ref[...]) for b in range(nb)]
        return [wide[b][:, sl] for b, sl, _ in units]

    wz = each(lambda z, bias: z + bias, low_rank(hw_ref, ww2_ref), row(w0_ref))
    az = each(lambda z, bias: z + bias, low_rank(ha_ref, aw2_ref), row(a0_ref))
    gate = low_rank(hg_ref, gw2_ref)

    logw = each(lambda z: -jnp.exp(-jax.nn.softplus(-z) - 0.5), wz)
    a = each(jax.nn.sigmoid, az)
    kk0 = each(lambda x, y: x * y, k, k_k)
    k2 = each(lambda x, al, ka: x * (1.0 + (al - 1.0) * ka), k, a, k_a)
    ssq, rk_sum = headsums(each(lambda x: x * x, kk0), each(lambda rr, kv, rk: rr * kv * rk, r, k2, r_k))
    kk = each(lambda x, s: x * lax.rsqrt(jnp.maximum(s, 1e-24)), kk0, ssq)
    bonus = each(lambda s, vv: s * vv, rk_sum, v)
    bvec = each(lambda x, al: x * al, kk, a)

    def cumsum(lw):
        p0 = lw.astype(BF16)
        r1 = lw - p0.astype(F32)
        p1 = r1.astype(BF16)
        p2 = (r1 - p1.astype(F32)).astype(BF16)
        return _dot(tri, p0) + _dot(tri, p1) + _dot(tri, p2)

    cum = each(cumsum, logw)
    cend = each(lambda x: x[c - 1:c, :], cum)
    a_t = each(lambda x, cu, lw: (-x * jnp.exp(cu - lw)).astype(BF16), kk, cum, logw)
    r_t = each(lambda x, cu: (x * jnp.exp(cu)).astype(BF16), r, cum)
    w_inv = each(lambda cu: jnp.exp(-cu), cum)
    w_rest = each(lambda ce, cu: jnp.exp(ce - cu), cend, cum)
    bs = each(lambda x, w: stack(x * w, stack_bd), bvec, w_inv)
    ks = each(lambda x, w: stack(x * w, stack_bd), k2, w_inv)
    vs = each(lambda x: stack(x, stack_bd), v)

    scores = each(lambda at, rt, b_, k_: _nt(jnp.concatenate([at, rt], axis=0), jnp.concatenate([b_, k_], axis=0)),
                  a_t, r_t, bs, ks)
    s_k = each(lambda s: jnp.concatenate([jnp.where(strict, s[:c, unit:], 0.0), jnp.where(incl, s[c:, unit:], 0.0)],
                                         axis=0).astype(BF16), scores)
    s_rb = each(lambda s: jnp.where(incl, s[c:, :unit], 0.0).astype(BF16), scores)

    pw = each(lambda s: jnp.where(strict, s[:c, :unit], 0.0), scores)
    t_cat = each(lambda p: eye_cat + p, pw)
    pw = each(lambda p: _dot(p.astype(BF16), stack(p, chunk_bd)), pw)
    for _ in range(c.bit_length() - 3):
        both = each(lambda t, p: _dot(jnp.concatenate([t, p], axis=0).astype(BF16), stack(p, chunk_bd)), t_cat, pw)
        t_cat = each(lambda t, r_: t + r_[:c], t_cat, both)
        pw = each(lambda r_: r_[c:], both)
    t_cat = each(lambda t, p: t + _dot(t.astype(BF16), stack(p, chunk_bd)), t_cat, pw)

    state = [state_ref[b, q] for b, _, q in units]
    state_b = each(lambda s: s.astype(BF16), state)
    base = each(lambda at, rt, sb, sk, vv: _nt(jnp.concatenate([at, rt], axis=0), sb) + _dot(sk, vv),
                a_t, r_t, state_b, s_k, vs)
    u = each(lambda t, bb: _dot(t.astype(BF16), stack(bb[:c], stack_bd)), t_cat, base)
    y = each(lambda bb, srb, uu: bb[c:] + _dot(srb, stack(uu, stack_bd)), base, s_rb, u)
    upd = each(lambda uu, vv, bv, kv, wr: _tn(jnp.concatenate([uu, vv], axis=0).astype(BF16),
                                               jnp.concatenate([bv * wr, kv * wr], axis=0).astype(BF16)),
               u, v, bvec, k2, w_rest)
    for (b, _, q), s, ce, up in zip(units, state, cend, upd):
        state_ref[b, q] = jnp.where(head_bd, s * jnp.exp(ce) + up, 0.0)

    inv_n = 1.0 / RWKV_HEAD
    (y_sum,) = headsums(y)
    yc = each(lambda yy, m: yy - m * inv_n, y, y_sum)
    (sq_sum,) = headsums(each(lambda z: z * z, yc))
    var = each(lambda s: s * inv_n, sq_sum)
    for (b, sl, _), z, vr, lw, lb, bo, gt in zip(units, yc, var, ln_w, ln_b, bonus, gate):
        o_ref[b, :, sl] = ((z * lax.rsqrt(vr + RWKV_GN_EPS) * lw + lb + bo) * gt).astype(o_ref.dtype)


def rwkv_core(r, k, v, low, low_w2, w0, a0, k_k, k_a, r_k, ln_w, ln_b, units_per_step=4):
    b, s, d = r.shape
    c = RWKV_CHUNK
    assert s % c == 0 and RWKV_UNIT % c == 0
    width = _tile(d, RWKV_UNIT * units_per_step)
    nu = width // RWKV_UNIT
    act = pl.BlockSpec((b, c, width), lambda i, j: (0, j, i))
    row = pl.BlockSpec((1, width), lambda i, j: (0, i))
    rows = [t.reshape(1, d).astype(F32) for t in (w0, a0, k_k, k_a, r_k, ln_w, ln_b)]
    return pl.pallas_call(
        _rwkv_kernel,
        grid=(d // width, s // c),
        in_specs=[act] * 3
        + [pl.BlockSpec((b, c, h.shape[2]), lambda i, j: (0, j, 0)) for h in low]
        + [pl.BlockSpec((w.shape[0], width), lambda i, j: (0, i)) for w in low_w2]
        + [row] * len(rows),
        out_specs=act,
        out_shape=jax.ShapeDtypeStruct((b, s, d), BF16),
        scratch_shapes=[pltpu.VMEM((b, nu, RWKV_UNIT, RWKV_UNIT), F32)],
        compiler_params=_params("parallel", "arbitrary"),
        name="rwkv7_chunk",
    )(r, k, v, *low, *low_w2, *rows)


def _rwkv_mix_kernel(x_ref, xp_ref, g_ref, mu_ref, ww1_ref, aw1_ref, gw1_ref,
                     xr_ref, xk_ref, xv_ref, hw_ref, ha_ref, hg_ref):
    gain = g_ref[...]
    hn = _rms(x_ref[0], gain)
    tail = xp_ref.shape[1]
    prev_last = _rms(xp_ref[0, tail - 1:tail, :], gain)
    prev_last = jnp.where(pl.program_id(1) > 0, prev_last, 0.0)
    row = lax.broadcasted_iota(jnp.int32, hn.shape, 0)
    xx = jnp.where(row == 0, prev_last, pltpu.roll(hn, 1, 0)) - hn
    mix = lambda i: (hn + xx * mu_ref[i:i + 1, :]).astype(BF16)
    xr_ref[0] = mix(0)
    xk_ref[0] = mix(2)
    xv_ref[0] = mix(3)
    hw_ref[0] = jnp.tanh(_dot(mix(1), ww1_ref[...])).astype(BF16)
    ha_ref[0] = _dot(mix(4), aw1_ref[...]).astype(BF16)
    hg_ref[0] = jax.nn.sigmoid(_dot(mix(5), gw1_ref[...])).astype(BF16)


def rwkv_mix(x, gain, mu, w_w1, a_w1, g_w1, tm=512, tail=8):
    b, s, d = x.shape
    tm = _tile(s, tm)
    tok = lambda width: pl.BlockSpec((1, tm, width), lambda bi, i: (bi, i, 0))
    whole = lambda arr: pl.BlockSpec(arr.shape, lambda bi, i: (0, 0))
    ranks = [w.shape[1] for w in (w_w1, a_w1, g_w1)]
    mu = mu.astype(F32)
    return pl.pallas_call(
        _rwkv_mix_kernel,
        grid=(b, s // tm),
        in_specs=[tok(d),
                  pl.BlockSpec((1, tail, d), lambda bi, i: (bi, jnp.maximum(i * (tm // tail) - 1, 0), 0)),
                  pl.BlockSpec((1, d), lambda bi, i: (0, 0)),
                  whole(mu), whole(w_w1), whole(a_w1), whole(g_w1)],
        out_specs=[tok(d)] * 3 + [tok(rk) for rk in ranks],
        out_shape=[jax.ShapeDtypeStruct((b, s, d), BF16)] * 3
        + [jax.ShapeDtypeStruct((b, s, rk), BF16) for rk in ranks],
        compiler_params=_params("parallel", "parallel"),
        name="rwkv_mix",
    )(x, x, gain.reshape(1, d), mu, w_w1, a_w1, g_w1)


def _pad_cols(w, mult=128):
    n = w.shape[1]
    return jnp.pad(w, ((0, 0), (0, (-n) % mult)))


def _pad_rows(w, mult=128):
    n = w.shape[0]
    return jnp.pad(w, ((0, (-n) % mult), (0, 0)))


def rwkv_layer(x, b, s, gain, mu, w_r, w_k, w_v, w0, w_w1, w_w2, a0, a_w1, a_w2, g_w1, g_w2,
               k_k, k_a, r_k, ln_w, ln_b, w_o):
    t, d = x.shape
    bf = lambda w: w.astype(BF16)
    xr, xk, xv, *low = rwkv_mix(x.reshape(b, s, d), gain, mu,
                                bf(_pad_cols(w_w1)), bf(_pad_cols(a_w1)), bf(_pad_cols(g_w1)))
    r, k, v = (matmul(m.reshape(t, d), bf(w)).reshape(b, s, d) for m, w in ((xr, w_r), (xk, w_k), (xv, w_v)))
    low_w2 = [bf(_pad_rows(w)) for w in (w_w2, a_w2, g_w2)]
    y = rwkv_core(r, k, v, low, low_w2, w0, a0, k_k, k_a, r_k, ln_w, ln_b)
    return matmul(y.reshape(t, d), bf(w_o), residual=x)


def kernel(x, positions, l0_norm_mix, l0_attn_w_in, l0_attn_w_out, l0_norm_ffn, l0_ffn_w_in, l0_ffn_w_out, l1_norm_mix, l1_s5_a_re, l1_s5_a_im, l1_s5_log_dt, l1_s5_b_re, l1_s5_b_im, l1_s5_c_re, l1_s5_c_im, l1_s5_d, l1_s5_w_glu, l1_norm_ffn, l1_ffn_w_in, l1_ffn_w_out, l2_norm_mix, l2_rwkv_mu, l2_rwkv_w_r, l2_rwkv_w_k, l2_rwkv_w_v, l2_rwkv_w0, l2_rwkv_w_w1, l2_rwkv_w_w2, l2_rwkv_a0, l2_rwkv_a_w1, l2_rwkv_a_w2, l2_rwkv_g_w1, l2_rwkv_g_w2, l2_rwkv_k_k, l2_rwkv_k_a, l2_rwkv_r_k, l2_rwkv_ln_w, l2_rwkv_ln_b, l2_rwkv_w_o, l2_norm_ffn, l2_ffn_w_in, l2_ffn_w_out, l3_norm_mix, l3_attn_w_in, l3_attn_w_out, l3_norm_ffn, l3_ffn_w_in, l3_ffn_w_out, final_norm):
    b, s, d = x.shape
    h = x.reshape(b * s, d)
    rope = _rope_tables(positions)

    def channel_mixer(h, gain, w_in, w_out, final_gain=None):
        return ffn(h, gain, w_in.astype(BF16), w_out.astype(BF16), final_gain)

    h = attention_layer(h, b, s, rope, l0_norm_mix, l0_attn_w_in, l0_attn_w_out)
    h = channel_mixer(h, l0_norm_ffn, l0_ffn_w_in, l0_ffn_w_out)
    h = s5_layer(h, b, s, l1_norm_mix, l1_s5_a_re, l1_s5_a_im, l1_s5_log_dt, l1_s5_b_re, l1_s5_b_im,
                 l1_s5_c_re, l1_s5_c_im, l1_s5_d, l1_s5_w_glu)
    h = channel_mixer(h, l1_norm_ffn, l1_ffn_w_in, l1_ffn_w_out)
    h = rwkv_layer(h, b, s, l2_norm_mix, l2_rwkv_mu, l2_rwkv_w_r, l2_rwkv_w_k, l2_rwkv_w_v, l2_rwkv_w0,
                   l2_rwkv_w_w1, l2_rwkv_w_w2, l2_rwkv_a0, l2_rwkv_a_w1, l2_rwkv_a_w2, l2_rwkv_g_w1,
                   l2_rwkv_g_w2, l2_rwkv_k_k, l2_rwkv_k_a, l2_rwkv_r_k, l2_rwkv_ln_w, l2_rwkv_ln_b,
                   l2_rwkv_w_o)
    h = channel_mixer(h, l2_norm_ffn, l2_ffn_w_in, l2_ffn_w_out)
    h = attention_layer(h, b, s, rope, l3_norm_mix, l3_attn_w_in, l3_attn_w_out)
    h = channel_mixer(h, l3_norm_ffn, l3_ffn_w_in, l3_ffn_w_out, final_gain=final_norm)
    return h.reshape(b, s, d)
```

```python
import functools
import math

import jax
import jax.numpy as jnp
from jax import lax
from jax.experimental import pallas as pl
from jax.experimental.pallas import tpu as pltpu

F32 = jnp.float32
BF16 = jnp.bfloat16

RMS_EPS = 1e-6
LANES = 128
VMEM_LIMIT_BYTES = 56 * 1024 * 1024

ATTN_HEAD_DIM = 128
ATTN_HEADS = 8
ATTN_GROUPS = ((128, 1), (512, 4), (2048, 16))
ATTN_BLOCK = 128
ATTN_BLOCKS_PER_STEP = 2
ROPE_THETA = 500000.0
ROPE_DIM = ATTN_HEAD_DIM // 4
ATTN_WIDTH = ATTN_HEADS * ATTN_HEAD_DIM
QKV_COL_CHUNK = 256
NEG_BIG = -1e30

S5_GROUP = 16
S5_STATE = 64
S5_CHUNK = 16

RWKV_HEAD = 64
RWKV_CHUNK = 64
RWKV_HEADS_PER_UNIT = 4
RWKV_UNIT = RWKV_HEADS_PER_UNIT * RWKV_HEAD
RWKV_GN_EPS = 64e-5


def _params(*sem):
    return pltpu.CompilerParams(dimension_semantics=sem, vmem_limit_bytes=VMEM_LIMIT_BYTES)


def _tile(n, pref):
    t = min(n, pref)
    assert n % t == 0, (n, pref)
    return t


def _rms(x, g):
    return x * lax.rsqrt(jnp.mean(x * x, axis=-1, keepdims=True) + RMS_EPS) * g


def _nt(a, b):
    return lax.dot_general(a, b, (((1,), (1,)), ((), ())), preferred_element_type=F32)


def _tn(a, b):
    return lax.dot_general(a, b, (((0,), (0,)), ((), ())), preferred_element_type=F32)


def _dot(a, b):
    return jnp.dot(a, b, preferred_element_type=F32)


def _rmsnorm_kernel(x_ref, g_ref, o_ref):
    o_ref[...] = _rms(x_ref[...], g_ref[...]).astype(o_ref.dtype)


def rmsnorm(x, gain, out_dtype=F32, tm=512):
    t, d = x.shape
    tm = _tile(t, tm)
    return pl.pallas_call(
        _rmsnorm_kernel,
        grid=(t // tm,),
        in_specs=[pl.BlockSpec((tm, d), lambda i: (i, 0)), pl.BlockSpec((1, d), lambda i: (0, 0))],
        out_specs=pl.BlockSpec((tm, d), lambda i: (i, 0)),
        out_shape=jax.ShapeDtypeStruct((t, d), out_dtype),
        compiler_params=_params("parallel"),
        name="rmsnorm",
    )(x, gain.reshape(1, d))


def _mm_kernel(x_ref, w_ref, *refs):
    *r_ref, o_ref = refs
    acc = _dot(x_ref[...], w_ref[...])
    if r_ref:
        acc = acc + r_ref[0][...]
    o_ref[...] = acc.astype(o_ref.dtype)


def matmul(x, w, *, residual=None, out_dtype=F32, tm=1024, tn=1024):
    t, k = x.shape
    n = w.shape[1]
    tm, tn = _tile(t, tm), _tile(n, tn)
    ins = [x, w]
    specs = [pl.BlockSpec((tm, k), lambda i, j: (i, 0)), pl.BlockSpec((k, tn), lambda i, j: (0, j))]
    if residual is not None:
        ins.append(residual)
        specs.append(pl.BlockSpec((tm, tn), lambda i, j: (i, j)))
    return pl.pallas_call(
        _mm_kernel,
        grid=(t // tm, n // tn),
        in_specs=specs,
        out_specs=pl.BlockSpec((tm, tn), lambda i, j: (i, j)),
        out_shape=jax.ShapeDtypeStruct((t, n), out_dtype),
        compiler_params=_params("parallel", "parallel"),
        name="matmul",
    )(*ins)


def _ffn_kernel(x_ref, g_ref, wg_ref, wu_ref, wo_ref, fg_ref, o_ref, xn_ref, *, final_norm):
    j = pl.program_id(1)

    @pl.when(j == 0)
    def _():
        x = x_ref[...]
        xn_ref[...] = _rms(x, g_ref[...]).astype(BF16)
        o_ref[...] = x

    xn = xn_ref[...]
    gate = _dot(xn, wg_ref[...])
    up = _dot(xn, wu_ref[...])
    h = (gate * jax.nn.sigmoid(gate) * up).astype(BF16)
    o_ref[...] += _dot(h, wo_ref[...])

    if final_norm:
        @pl.when(j == pl.num_programs(1) - 1)
        def _():
            o_ref[...] = _rms(o_ref[...], fg_ref[...])


def ffn(x, gain, w_in, w_out, final_gain=None, tm=512, tf=512):
    t, d = x.shape
    f = w_out.shape[0]
    tm, tf = _tile(t, tm), _tile(f, tf)
    nf = f // tf
    fg = gain if final_gain is None else final_gain
    return pl.pallas_call(
        functools.partial(_ffn_kernel, final_norm=final_gain is not None),
        grid=(t // tm, nf),
        in_specs=[
            pl.BlockSpec((tm, d), lambda i, j: (i, 0)),
            pl.BlockSpec((1, d), lambda i, j: (0, 0)),
            pl.BlockSpec((d, tf), lambda i, j: (0, j)),
            pl.BlockSpec((d, tf), lambda i, j: (0, j + nf)),
            pl.BlockSpec((tf, d), lambda i, j: (j, 0)),
            pl.BlockSpec((1, d), lambda i, j: (0, 0)),
        ],
        out_specs=pl.BlockSpec((tm, d), lambda i, j: (i, 0)),
        out_shape=jax.ShapeDtypeStruct((t, d), F32),
        scratch_shapes=[pltpu.VMEM((tm, d), BF16)],
        compiler_params=_params("parallel", "arbitrary"),
        name="ffn",
    )(x, gain.reshape(1, d), w_in, w_in, w_out, fg.reshape(1, d))


def _norm_residues_kernel(x_ref, g_ref, *refs):
    out_refs, hn_ref = refs[:-1], refs[-1]
    hn = _rms(x_ref[0], g_ref[...])
    ncol, tm, lanes = hn_ref.shape
    for c in range(ncol):
        hn_ref[c] = hn[:, c * lanes:(c + 1) * lanes]
    for o_ref in out_refs:
        d = o_ref.shape[1]
        if d == 1:
            o_ref[0, 0] = hn.astype(o_ref.dtype)
            continue
        for r in range(d):
            for c in range(ncol):
                o_ref[0, r, :, c * lanes:(c + 1) * lanes] = (
                    hn_ref[c, pl.ds(r, tm // d, stride=d), :].astype(o_ref.dtype))


def norm_by_residue(x, gain, dilations, tm=512):
    b, s, d_model = x.shape
    tm = _tile(s, tm)
    return pl.pallas_call(
        _norm_residues_kernel,
        grid=(b, s // tm),
        in_specs=[pl.BlockSpec((1, tm, d_model), lambda bi, i: (bi, i, 0)),
                  pl.BlockSpec((1, d_model), lambda bi, i: (0, 0))],
        out_specs=[pl.BlockSpec((1, d, tm // d, d_model), lambda bi, i: (bi, 0, i, 0)) for d in dilations],
        out_shape=[jax.ShapeDtypeStruct((b, d, s // d, d_model), BF16) for d in dilations],
        scratch_shapes=[pltpu.VMEM((d_model // LANES, tm, LANES), F32)],
        compiler_params=_params("parallel", "parallel"),
        name="norm_by_residue",
    )(x, gain.reshape(1, d_model))


def _qkv_kernel(x_ref, w_ref, cos_ref, sin_ref, o_ref):
    kind = pl.program_id(1)
    scale = jnp.where(kind == 0, ATTN_HEAD_DIM ** -0.5, 1.0).astype(F32)
    is_v = kind == 2
    cos = jnp.where(is_v, 1.0, cos_ref[...] * scale)
    sin = jnp.where(is_v, 0.0, sin_ref[...] * scale)
    x = x_ref[...]
    for c0 in range(0, o_ref.shape[1], QKV_COL_CHUNK):
        acc = _dot(x, w_ref[:, c0:c0 + QKV_COL_CHUNK])
        for h0 in range(0, QKV_COL_CHUNK, ATTN_HEAD_DIM):
            xh = acc[:, h0:h0 + ATTN_HEAD_DIM]
            rot = pltpu.roll(xh, ATTN_HEAD_DIM // 2, 1)
            o_ref[:, c0 + h0:c0 + h0 + ATTN_HEAD_DIM] = (xh * cos + rot * sin).astype(o_ref.dtype)


def qkv_projection(x, w_in, gi, cos_t, sin_t, tm=1024):
    t, d = x.shape
    tm = _tile(t, tm)
    tn = ATTN_WIDTH
    return pl.pallas_call(
        _qkv_kernel,
        grid=(t // tm, 3),
        in_specs=[
            pl.BlockSpec((tm, d), lambda i, j: (i, 0)),
            pl.BlockSpec((d, tn), lambda i, j: (0, gi * 3 + j)),
            pl.BlockSpec((tm, ATTN_HEAD_DIM), lambda i, j: (i, 0)),
            pl.BlockSpec((tm, ATTN_HEAD_DIM), lambda i, j: (i, 0)),
        ],
        out_specs=pl.BlockSpec((tm, tn), lambda i, j: (i, j)),
        out_shape=jax.ShapeDtypeStruct((t, 3 * tn), BF16),
        compiler_params=_params("parallel", "parallel"),
        name="qkv_rope",
    )(x, w_in, cos_t, sin_t)


def _attn_kernel(q_ref, kp_ref, kc_ref, vp_ref, vc_ref, o_ref, st_ref):
    n = pl.program_id(1)
    blk = ATTN_BLOCK
    qi = lax.broadcasted_iota(jnp.int32, (blk, 2 * blk), 0)
    kj = lax.broadcasted_iota(jnp.int32, (blk, 2 * blk), 1)
    dist = qi + blk - kj
    band = (dist >= 0) & (dist <= blk)
    first = band & ((kj >= blk) | (n > 0))
    lane = lax.broadcasted_iota(jnp.int32, (blk, ATTN_HEAD_DIM), 1)
    stats = [jnp.zeros((blk, ATTN_HEAD_DIM), F32) for _ in range(ATTN_BLOCKS_PER_STEP)]
    for h in range(ATTN_HEADS):
        sl = slice(h * ATTN_HEAD_DIM, (h + 1) * ATTN_HEAD_DIM)
        k = jnp.concatenate([kp_ref[0, :, sl], kc_ref[0, :, sl]], axis=0)
        v = jnp.concatenate([vp_ref[0, :, sl], vc_ref[0, :, sl]], axis=0)
        for j in range(ATTN_BLOCKS_PER_STEP):
            rows = slice(j * blk, (j + 1) * blk)
            keys = slice(j * blk, (j + 2) * blk)
            s = jnp.where(first if j == 0 else band, _nt(q_ref[0, rows, sl], k[keys]), NEG_BIG)
            mx = jnp.max(s, axis=-1, keepdims=True)
            p = jnp.exp(s - mx)
            den = jnp.sum(p, axis=-1, keepdims=True)
            num = _dot(p.astype(BF16), v[keys])
            o_ref[0, rows, sl] = (num / den).astype(o_ref.dtype)
            stats[j] = jnp.where(lane == h, mx + jnp.log(den), stats[j])
    for j in range(ATTN_BLOCKS_PER_STEP):
        st_ref[0, j * blk:(j + 1) * blk, :] = stats[j]


def attention_group(qkv):
    nseq, length, _ = qkv.shape
    per = ATTN_BLOCKS_PER_STEP
    rows = per * ATTN_BLOCK
    assert length % rows == 0
    nb = length // rows
    blk = (1, rows, ATTN_WIDTH)

    def cur(off):
        return pl.BlockSpec(blk, lambda r, nn: (r, nn, off))

    def prev(off):
        return pl.BlockSpec((1, ATTN_BLOCK, ATTN_WIDTH), lambda r, nn: (r, jnp.maximum(nn * per - 1, 0), off))

    return pl.pallas_call(
        _attn_kernel,
        grid=(nseq, nb),
        in_specs=[cur(0), prev(1), cur(1), prev(2), cur(2)],
        out_specs=[
            pl.BlockSpec(blk, lambda r, nn: (r, nn, 0)),
            pl.BlockSpec((1, rows, ATTN_HEAD_DIM), lambda r, nn: (r, nn, 0)),
        ],
        out_shape=[
            jax.ShapeDtypeStruct((nseq, length, ATTN_WIDTH), BF16),
            jax.ShapeDtypeStruct((nseq, length, ATTN_HEAD_DIM), F32),
        ],
        compiler_params=_params("parallel", "parallel"),
        name="dilated_attn",
    )(qkv, qkv, qkv, qkv, qkv)


def _attn_out_kernel(*refs):
    ng = len(ATTN_GROUPS)
    o_refs, s_refs = refs[:ng], refs[ng:2 * ng]
    w_ref, r_ref, out_ref, om_ref, su_ref = refs[2 * ng:]

    @pl.when(pl.program_id(2) == 0)
    def _():
        tm = om_ref.shape[0]
        tok = lax.broadcasted_iota(jnp.int32, (tm, tm), 0)
        src = lax.broadcasted_iota(jnp.int32, (tm, tm), 1)
        outs = []
        for g in range(ng):
            d = o_refs[g].shape[1]
            for r in range(d):
                su_ref[g, pl.ds(r, tm // d, stride=d), :] = s_refs[g][0, r]
            o = o_refs[g][0].reshape(tm, ATTN_WIDTH)
            if d > 1:
                to_token_order = (src == (tok % d) * (tm // d) + tok // d).astype(BF16)
                o = _dot(to_token_order, o)
            outs.append(o)
        lse = [su_ref[g] for g in range(ng)]
        m = functools.reduce(jnp.maximum, lse)
        e = [jnp.exp(l - m) for l in lse]
        inv = 1.0 / sum(e)
        wts = [e[g] * inv for g in range(1, ng)]
        for h in range(ATTN_HEADS):
            sl = slice(h * ATTN_HEAD_DIM, (h + 1) * ATTN_HEAD_DIM)
            base = outs[0][:, sl].astype(F32)
            om = base + sum(wt[:, h:h + 1] * (o[:, sl].astype(F32) - base) for wt, o in zip(wts, outs[1:]))
            om_ref[:, sl] = om.astype(BF16)

    out_ref[0] = r_ref[0] + _dot(om_ref[...], w_ref[...])


def attention_out(parts, w_out, residual, tm=512, tn=1024):
    b, s, n = residual.shape
    k = w_out.shape[0]
    tm, tn = _tile(s, tm), _tile(n, tn)
    ng = len(parts)

    def by_residue(arr):
        d, width = arr.shape[1], arr.shape[3]
        return pl.BlockSpec((1, d, tm // d, width), lambda bi, i, j: (bi, 0, i, 0))

    res = pl.BlockSpec((1, tm, tn), lambda bi, i, j: (bi, i, j))
    return pl.pallas_call(
        _attn_out_kernel,
        grid=(b, s // tm, n // tn),
        in_specs=[by_residue(o) for o, _ in parts] + [by_residue(st) for _, st in parts]
        + [pl.BlockSpec((k, tn), lambda bi, i, j: (0, j)), res],
        out_specs=res,
        out_shape=jax.ShapeDtypeStruct((b, s, n), F32),
        scratch_shapes=[pltpu.VMEM((tm, k), BF16),
                        pltpu.VMEM((ng, tm, ATTN_HEAD_DIM), F32)],
        compiler_params=_params("parallel", "parallel", "arbitrary"),
        name="attn_merge_out",
    )(*[o for o, _ in parts], *[st for _, st in parts], w_out, residual)


def _by_residue(t, dilation):
    b, s = t.shape[:2]
    return jnp.swapaxes(t.reshape(b, s // dilation, dilation, *t.shape[2:]), 1, 2)


def _rope_layout(w_in):
    half = ROPE_DIM // 2
    d = w_in.shape[0]
    w = w_in.reshape(d, len(ATTN_GROUPS), 3, ATTN_HEADS, ATTN_HEAD_DIM)
    qk, v = w[:, :, :2], w[:, :, 2:]
    mid = ATTN_HEAD_DIM // 2 + half
    qk = jnp.concatenate([qk[..., :half], qk[..., ROPE_DIM:mid], qk[..., half:ROPE_DIM], qk[..., mid:]], axis=-1)
    return jnp.concatenate([qk, v], axis=2).reshape(w_in.shape)


def _rope_tables(positions):
    half = ROPE_DIM // 2
    inv_freq = ROPE_THETA ** (-jnp.arange(half, dtype=F32) * 2.0 / ROPE_DIM)
    pad = ATTN_HEAD_DIM // 2 - half
    tables = []
    for _, dil in ATTN_GROUPS:
        ang = _by_residue(positions, dil).astype(F32).reshape(-1, 1) * inv_freq
        cos, sin = jnp.cos(ang), jnp.sin(ang)
        one, zero = jnp.ones((ang.shape[0], pad), F32), jnp.zeros((ang.shape[0], pad), F32)
        tables.append((jnp.concatenate([cos, one, cos, one], axis=1),
                       jnp.concatenate([-sin, zero, sin, zero], axis=1)))
    return tables


def attention_layer(x, b, s, rope, gain, w_in, w_out):
    t, d = x.shape
    w_in = _rope_layout(w_in).astype(BF16)
    dils = [dil for _, dil in ATTN_GROUPS]
    hns = norm_by_residue(x.reshape(b, s, d), gain, dils)
    parts = []
    for gi, (dil, hn, (cos_t, sin_t)) in enumerate(zip(dils, hns, rope)):
        qkv = qkv_projection(hn.reshape(t, d), w_in, gi, cos_t, sin_t)
        o, st = attention_group(qkv.reshape(b * dil, s // dil, -1))
        parts.append((o.reshape(b, dil, s // dil, -1), st.reshape(b, dil, s // dil, -1)))
    return attention_out(parts, w_out.astype(BF16), x.reshape(b, s, d)).reshape(t, d)


def _gelu_tanh(y):
    return 0.5 * y * (1.0 + jnp.tanh(math.sqrt(2.0 / math.pi) * (y + 0.044715 * (y * y * y))))


def _s5_kernel(u_ref, perm_ref, toep_ref, wp_ref, q_ref, c1_ref, c2_ref, z_ref):
    lc = S5_CHUNK
    groups = toep_ref.shape[0]
    nch = u_ref.shape[1] // lc
    width = lc * S5_GROUP
    nsteps = c1_ref.shape[1]
    perm = perm_ref[...]
    by_pos = jnp.concatenate([u_ref[0, pl.ds(s, nch, stride=lc), :].astype(BF16) for s in range(lc)], axis=1)
    by_group = _dot(by_pos, perm).astype(BF16)
    row = lax.broadcasted_iota(jnp.int32, (nch, 2 * S5_STATE), 0)
    zs = []
    for g in range(groups):
        u = by_group[:, g * width:(g + 1) * width]
        y = _dot(u, toep_ref[g])
        x = _dot(u, wp_ref[g])
        for i in range(nsteps):
            m = 1 << i
            sh = jnp.where(row >= m, pltpu.roll(x, m, 0), 0.0)
            x = x + c1_ref[g, i:i + 1, :] * sh + c2_ref[g, i:i + 1, :] * pltpu.roll(sh, S5_STATE, 1)
        xprev = jnp.where(row >= 1, pltpu.roll(x, 1, 0), 0.0)
        y = y + _dot(xprev.astype(BF16), q_ref[g])
        zs.append(_gelu_tanh(y).astype(BF16))
    out = _nt(jnp.concatenate(zs, axis=1), perm)
    for s in range(lc):
        z_ref[0, pl.ds(s, nch, stride=lc), :] = out[:, s * LANES:(s + 1) * LANES]


def _s5_operators(a_re, a_im, log_dt, b_re, b_im, c_re, c_im, d_skip, chunks_per_seq):
    lc = S5_CHUNK
    g, p = a_re.shape
    dt = jnp.exp(log_dt)[:, None]
    mag = jnp.exp(dt * a_re)
    ab_re = mag * jnp.cos(dt * a_im)
    ab_im = mag * jnp.sin(dt * a_im)
    inv = 1.0 / (a_re * a_re + a_im * a_im)
    f_re = ((ab_re - 1.0) * a_re + ab_im * a_im) * inv
    f_im = (ab_im * a_re - (ab_re - 1.0) * a_im) * inv
    bb_re = f_re[..., None] * b_re - f_im[..., None] * b_im
    bb_im = f_re[..., None] * b_im + f_im[..., None] * b_re

    def power(j):
        jf = j.astype(F32)[:, None, None]
        m = jnp.exp(jf * (dt * a_re))
        return m * jnp.cos(jf * (dt * a_im)), m * jnp.sin(jf * (dt * a_im))

    pr, pi = power(jnp.arange(lc + 1))
    ba_re = pr[..., None] * bb_re - pi[..., None] * bb_im
    ba_im = pr[..., None] * bb_im + pi[..., None] * bb_re
    kern = (jnp.einsum("jgpa,gcp->jgac", ba_re, c_re) - jnp.einsum("jgpa,gcp->jgac", ba_im, c_im))
    ti = jnp.arange(lc)
    lag = ti[None, :] - ti[:, None]
    toep = jnp.where((lag >= 0)[None, :, None, :, None],
                     jnp.transpose(kern[jnp.clip(lag, 0, lc)], (2, 0, 3, 1, 4)), 0.0)
    eye = jnp.eye(lc, dtype=F32)[:, None, :, None] * jnp.eye(S5_GROUP, dtype=F32)[None, :, None, :]
    toep = toep + d_skip.reshape(g, 1, S5_GROUP, 1, 1) * eye[None]
    toep = toep.reshape(g, lc * S5_GROUP, lc * S5_GROUP)
    rev = lc - 1 - ti
    wp = jnp.concatenate([ba_re[rev], ba_im[rev]], axis=2)
    wp = jnp.transpose(wp, (1, 0, 3, 2)).reshape(g, lc * S5_GROUP, 2 * p)
    qr, qi = pr[1:], pi[1:]
    q_top = (jnp.einsum("tgp,gcp->gptc", qr, c_re) - jnp.einsum("tgp,gcp->gptc", qi, c_im))
    q_bot = (-jnp.einsum("tgp,gcp->gptc", qi, c_re) - jnp.einsum("tgp,gcp->gptc", qr, c_im))
    q = jnp.concatenate([q_top, q_bot], axis=1).reshape(g, 2 * p, lc * S5_GROUP)
    nsteps = chunks_per_seq.bit_length() - 1
    sr, si = power(lc * (2 ** jnp.arange(nsteps)))
    c1 = jnp.transpose(jnp.concatenate([sr, sr], axis=2), (1, 0, 2))
    c2 = jnp.transpose(jnp.concatenate([-si, si], axis=2), (1, 0, 2))
    return toep.astype(BF16), wp.astype(BF16), q.astype(BF16), c1, c2


def s5_core(u, ops):
    b, s, d = u.shape
    lc = S5_CHUNK
    cps = s // lc
    assert cps & (cps - 1) == 0 and d % LANES == 0
    gs = LANES // S5_GROUP
    width = lc * S5_GROUP
    toep, wp, q, c1, c2 = ops
    nsteps = c1.shape[1]
    i = jnp.arange(lc * LANES)
    dest = (i % LANES // S5_GROUP) * width + (i // LANES) * S5_GROUP + i % S5_GROUP
    perm = (dest[:, None] == i[None, :]).astype(BF16)
    per_tile = lambda shape: pl.BlockSpec((gs,) + shape, lambda ti, bi: (ti, 0, 0))
    act = pl.BlockSpec((1, s, LANES), lambda ti, bi: (bi, 0, ti))
    return pl.pallas_call(
        _s5_kernel,
        grid=(d // LANES, b),
        in_specs=[
            act,
            pl.BlockSpec((lc * LANES, lc * LANES), lambda ti, bi: (0, 0), pipeline_mode=pl.Buffered(1)),
            per_tile((width, width)),
            per_tile((width, 2 * S5_STATE)),
            per_tile((2 * S5_STATE, width)),
            per_tile((nsteps, 2 * S5_STATE)),
            per_tile((nsteps, 2 * S5_STATE)),
        ],
        out_specs=act,
        out_shape=jax.ShapeDtypeStruct((b, s, d), F32),
        compiler_params=_params("parallel", "parallel"),
        name="s5_chunk_scan",
    )(u, perm, toep, wp, q, c1, c2)


def _glu_kernel(z_ref, wv_ref, wg_ref, r_ref, o_ref):
    z = z_ref[...].astype(BF16)
    val = _dot(z, wv_ref[...])
    gate = _dot(z, wg_ref[...])
    o_ref[...] = r_ref[...] + val * jax.nn.sigmoid(gate)


def glu_out(z, w_glu, residual, tm=1024, tn=1024):
    t, k = z.shape
    n = w_glu.shape[1] // 2
    tm, tn = _tile(t, tm), _tile(n, tn)
    nn = n // tn
    return pl.pallas_call(
        _glu_kernel,
        grid=(t // tm, nn),
        in_specs=[
            pl.BlockSpec((tm, k), lambda i, j: (i, 0)),
            pl.BlockSpec((k, tn), lambda i, j: (0, j)),
            pl.BlockSpec((k, tn), lambda i, j: (0, j + nn)),
            pl.BlockSpec((tm, tn), lambda i, j: (i, j)),
        ],
        out_specs=pl.BlockSpec((tm, tn), lambda i, j: (i, j)),
        out_shape=jax.ShapeDtypeStruct((t, n), F32),
        compiler_params=_params("parallel", "parallel"),
        name="glu_out",
    )(z, w_glu, w_glu, residual)


def s5_layer(x, b, s, gain, a_re, a_im, log_dt, b_re, b_im, c_re, c_im, d_skip, w_glu):
    u = rmsnorm(x, gain)
    ops = _s5_operators(a_re, a_im, log_dt, b_re, b_im, c_re, c_im, d_skip, s // S5_CHUNK)
    z = s5_core(u.reshape(b, s, -1), ops)
    return glu_out(z.reshape(x.shape), w_glu.astype(BF16), x)


def _split2(x):
    hi = x.astype(BF16)
    lo = (x - hi.astype(F32)).astype(BF16)
    return hi, lo


def _rwkv_kernel(r_ref, k_ref, v_ref, hw_ref, ha_ref, hg_ref, ww2_ref, aw2_ref, gw2_ref, w0_ref, a0_ref,
                 kk_ref, ka_ref, rk_ref, lnw_ref, lnb_ref, o_ref, state_ref):
    nb, c, width = r_ref.shape
    unit = RWKV_UNIT
    nu = width // unit
    reps = unit // c

    @pl.when(pl.program_id(1) == 0)
    def _():
        state_ref[...] = jnp.zeros_like(state_ref)

    ri = lax.broadcasted_iota(jnp.int32, (unit, unit), 0)
    ci = lax.broadcasted_iota(jnp.int32, (unit, unit), 1)
    head_bd = (ri // RWKV_HEAD) == (ci // RWKV_HEAD)
    stack_bd = (ri // c) == (ci // RWKV_HEAD)
    chunk_bd = (ri // c) == (ci // c)
    ones_bd = head_bd.astype(BF16)
    tr = lax.broadcasted_iota(jnp.int32, (c, unit), 0)
    tc = lax.broadcasted_iota(jnp.int32, (c, unit), 1) % c
    eye_cat = (tc == tr).astype(F32)
    strict = tc < tr
    incl = tc <= tr
    li = lax.broadcasted_iota(jnp.int32, (c, c), 0)
    lj = lax.broadcasted_iota(jnp.int32, (c, c), 1)
    tri = (lj <= li).astype(BF16)

    def headsums(*cols):
        pieces = [p for xs in zip(*cols) for x in xs for p in _split2(x)]
        parts = _dot(jnp.concatenate(pieces, axis=0), ones_bd)
        sums = [parts[2 * i * c:(2 * i + 1) * c] + parts[(2 * i + 1) * c:(2 * i + 2) * c]
                for i in range(len(pieces) // 2)]
        return [sums[i::len(cols)] for i in range(len(cols))]

    def stack(x, mask):
        return jnp.where(mask, jnp.concatenate([x] * reps, axis=0), 0.0).astype(BF16)

    units = [(b, slice(q * unit, (q + 1) * unit), q) for b in range(nb) for q in range(nu)]
    each = lambda fn, *cols: [fn(*vals) for vals in zip(*cols)]

    def load(ref):
        return [ref[b, :, sl] for b, sl, _ in units]

    def row(ref):
        return [ref[:, sl] for _, sl, _ in units]

    r, k, v = (load(ref) for ref in (r_ref, k_ref, v_ref))
    k_k, k_a, r_k, ln_w, ln_b = (row(ref) for ref in (kk_ref, ka_ref, rk_ref, lnw_ref, lnb_ref))

    def low_rank(h_ref, w2_ref):
        wide = [_dot(h_ref[b], w2_ref[...]) for b in range(nb)]
        return [wide[b][:, sl] for b, sl, _ in units]

    wz = each(lambda z, bias: z + bias, low_rank(hw_ref, ww2_ref), row(w0_ref))
    az = each(lambda z, bias: z + bias, low_rank(ha_ref, aw2_ref), row(a0_ref))
    gate = low_rank(hg_ref, gw2_ref)

    logw = each(lambda z: -jnp.exp(-jax.nn.softplus(-z) - 0.5), wz)
    a = each(jax.nn.sigmoid, az)
    kk0 = each(lambda x, y: x * y, k, k_k)
    k2 = each(lambda x, al, ka: x * (1.0 + (al - 1.0) * ka), k, a, k_a)
    ssq, rk_sum = headsums(each(lambda x: x * x, kk0), each(lambda rr, kv, rk: rr * kv * rk, r, k2, r_k))
    kk = each(lambda x, s: x * lax.rsqrt(jnp.maximum(s, 1e-24)), kk0, ssq)
    bonus = each(lambda s, vv: s * vv, rk_sum, v)
    bvec = each(lambda x, al: x * al, kk, a)

    def cumsum(lw):
        p0 = lw.astype(BF16)
        r1 = lw - p0.astype(F32)
        p1 = r1.astype(BF16)
        p2 = (r1 - p1.astype(F32)).astype(BF16)
        return _dot(tri, p0) + _dot(tri, p1) + _dot(tri, p2)

    cum = each(cumsum, logw)
    cend = each(lambda x: x[c - 1:c, :], cum)
    a_t = each(lambda x, cu, lw: (-x * jnp.exp(cu - lw)).astype(BF16), kk, cum, logw)
    r_t = each(lambda x, cu: (x * jnp.exp(cu)).astype(BF16), r, cum)
    w_inv = each(lambda cu: jnp.exp(-cu), cum)
    w_rest = each(lambda ce, cu: jnp.exp(ce - cu), cend, cum)
    bs = each(lambda x, w: stack(x * w, stack_bd), bvec, w_inv)
    ks = each(lambda x, w: stack(x * w, stack_bd), k2, w_inv)
    vs = each(lambda x: stack(x, stack_bd), v)

    scores = each(lambda at, rt, b_, k_: _nt(jnp.concatenate([at, rt], axis=0), jnp.concatenate([b_, k_], axis=0)),
                  a_t, r_t, bs, ks)
    s_k = each(lambda s: jnp.concatenate([jnp.where(strict, s[:c, unit:], 0.0), jnp.where(incl, s[c:, unit:], 0.0)],
                                         axis=0).astype(BF16), scores)
    s_rb = each(lambda s: jnp.where(incl, s[c:, :unit], 0.0).astype(BF16), scores)

    pw = each(lambda s: jnp.where(strict, s[:c, :unit], 0.0), scores)
    t_cat = each(lambda p: eye_cat + p, pw)
    pw = each(lambda p: _dot(p.astype(BF16), stack(p, chunk_bd)), pw)
    for _ in range(c.bit_length() - 3):
        both = each(lambda t, p: _dot(jnp.concatenate([t, p], axis=0).astype(BF16), stack(p, chunk_bd)), t_cat, pw)
        t_cat = each(lambda t, r_: t + r_[:c], t_cat, both)
        pw = each(lambda r_: r_[c:], both)
    t_cat = each(lambda t, p: t + _dot(t.astype(BF16), stack(p, chunk_bd)), t_cat, pw)

    state = [state_ref[b, q] for b, _, q in units]
    state_b = each(lambda s: s.astype(BF16), state)
    base = each(lambda at, rt, sb, sk, vv: _nt(jnp.concatenate([at, rt], axis=0), sb) + _dot(sk, vv),
                a_t, r_t, state_b, s_k, vs)
    u = each(lambda t, bb: _dot(t.astype(BF16), stack(bb[:c], stack_bd)), t_cat, base)
    y = each(lambda bb, srb, uu: bb[c:] + _dot(srb, stack(uu, stack_bd)), base, s_rb, u)
    upd = each(lambda uu, vv, bv, kv, wr: _tn(jnp.concatenate([uu, vv], axis=0).astype(BF16),
                                               jnp.concatenate([bv * wr, kv * wr], axis=0).astype(BF16)),
               u, v, bvec, k2, w_rest)
    for (b, _, q), s, ce, up in zip(units, state, cend, upd):
        state_ref[b, q] = jnp.where(head_bd, s * jnp.exp(ce) + up, 0.0)

    inv_n = 1.0 / RWKV_HEAD
    (y_sum,) = headsums(y)
    yc = each(lambda yy, m: yy - m * inv_n, y, y_sum)
    (sq_sum,) = headsums(each(lambda z: z * z, yc))
    var = each(lambda s: s * inv_n, sq_sum)
    for (b, sl, _), z, vr, lw, lb, bo, gt in zip(units, yc, var, ln_w, ln_b, bonus, gate):
        o_ref[b, :, sl] = ((z * lax.rsqrt(vr + RWKV_GN_EPS) * lw + lb + bo) * gt).astype(o_ref.dtype)


def rwkv_core(r, k, v, low, low_w2, w0, a0, k_k, k_a, r_k, ln_w, ln_b, units_per_step=4):
    b, s, d = r.shape
    c = RWKV_CHUNK
    assert s % c == 0 and RWKV_UNIT % c == 0
    width = _tile(d, RWKV_UNIT * units_per_step)
    nu = width // RWKV_UNIT
    act = pl.BlockSpec((b, c, width), lambda i, j: (0, j, i))
    row = pl.BlockSpec((1, width), lambda i, j: (0, i))
    rows = [t.reshape(1, d).astype(F32) for t in (w0, a0, k_k, k_a, r_k, ln_w, ln_b)]
    return pl.pallas_call(
        _rwkv_kernel,
        grid=(d // width, s // c),
        in_specs=[act] * 3
        + [pl.BlockSpec((b, c, h.shape[2]), lambda i, j: (0, j, 0)) for h in low]
        + [pl.BlockSpec((w.shape[0], width), lambda i, j: (0, i)) for w in low_w2]
        + [row] * len(rows),
        out_specs=act,
        out_shape=jax.ShapeDtypeStruct((b, s, d), BF16),
        scratch_shapes=[pltpu.VMEM((b, nu, RWKV_UNIT, RWKV_UNIT), F32)],
        compiler_params=_params("parallel", "arbitrary"),
        name="rwkv7_chunk",
    )(r, k, v, *low, *low_w2, *rows)


def _rwkv_mix_kernel(x_ref, xp_ref, g_ref, mu_ref, ww1_ref, aw1_ref, gw1_ref,
                     xr_ref, xk_ref, xv_ref, hw_ref, ha_ref, hg_ref):
    gain = g_ref[...]
    hn = _rms(x_ref[0], gain)
    tail = xp_ref.shape[1]
    prev_last = _rms(xp_ref[0, tail - 1:tail, :], gain)
    prev_last = jnp.where(pl.program_id(1) > 0, prev_last, 0.0)
    row = lax.broadcasted_iota(jnp.int32, hn.shape, 0)
    xx = jnp.where(row == 0, prev_last, pltpu.roll(hn, 1, 0)) - hn
    mix = lambda i: (hn + xx * mu_ref[i:i + 1, :]).astype(BF16)
    xr_ref[0] = mix(0)
    xk_ref[0] = mix(2)
    xv_ref[0] = mix(3)
    hw_ref[0] = jnp.tanh(_dot(mix(1), ww1_ref[...])).astype(BF16)
    ha_ref[0] = _dot(mix(4), aw1_ref[...]).astype(BF16)
    hg_ref[0] = jax.nn.sigmoid(_dot(mix(5), gw1_ref[...])).astype(BF16)


def rwkv_mix(x, gain, mu, w_w1, a_w1, g_w1, tm=512, tail=8):
    b, s, d = x.shape
    tm = _tile(s, tm)
    tok = lambda width: pl.BlockSpec((1, tm, width), lambda bi, i: (bi, i, 0))
    whole = lambda arr: pl.BlockSpec(arr.shape, lambda bi, i: (0, 0))
    ranks = [w.shape[1] for w in (w_w1, a_w1, g_w1)]
    mu = mu.astype(F32)
    return pl.pallas_call(
        _rwkv_mix_kernel,
        grid=(b, s // tm),
        in_specs=[tok(d),
                  pl.BlockSpec((1, tail, d), lambda bi, i: (bi, jnp.maximum(i * (tm // tail) - 1, 0), 0)),
                  pl.BlockSpec((1, d), lambda bi, i: (0, 0)),
                  whole(mu), whole(w_w1), whole(a_w1), whole(g_w1)],
        out_specs=[tok(d)] * 3 + [tok(rk) for rk in ranks],
        out_shape=[jax.ShapeDtypeStruct((b, s, d), BF16)] * 3
        + [jax.ShapeDtypeStruct((b, s, rk), BF16) for rk in ranks],
        compiler_params=_params("parallel", "parallel"),
        name="rwkv_mix",
    )(x, x, gain.reshape(1, d), mu, w_w1, a_w1, g_w1)


def _pad_cols(w, mult=128):
    n = w.shape[1]
    return jnp.pad(w, ((0, 0), (0, (-n) % mult)))


def _pad_rows(w, mult=128):
    n = w.shape[0]
    return jnp.pad(w, ((0, (-n) % mult), (0, 0)))


def rwkv_layer(x, b, s, gain, mu, w_r, w_k, w_v, w0, w_w1, w_w2, a0, a_w1, a_w2, g_w1, g_w2,
               k_k, k_a, r_k, ln_w, ln_b, w_o):
    t, d = x.shape
    bf = lambda w: w.astype(BF16)
    xr, xk, xv, *low = rwkv_mix(x.reshape(b, s, d), gain, mu,
                                bf(_pad_cols(w_w1)), bf(_pad_cols(a_w1)), bf(_pad_cols(g_w1)))
    r, k, v = (matmul(m.reshape(t, d), bf(w)).reshape(b, s, d) for m, w in ((xr, w_r), (xk, w_k), (xv, w_v)))
    low_w2 = [bf(_pad_rows(w)) for w in (w_w2, a_w2, g_w2)]
    y = rwkv_core(r, k, v, low, low_w2, w0, a0, k_k, k_a, r_k, ln_w, ln_b)
    return matmul(y.reshape(t, d), bf(w_o), residual=x)


def kernel(x, positions, l0_norm_mix, l0_attn_w_in, l0_attn_w_out, l0_norm_ffn, l0_ffn_w_in, l0_ffn_w_out, l1_norm_mix, l1_s5_a_re, l1_s5_a_im, l1_s5_log_dt, l1_s5_b_re, l1_s5_b_im, l1_s5_c_re, l1_s5_c_im, l1_s5_d, l1_s5_w_glu, l1_norm_ffn, l1_ffn_w_in, l1_ffn_w_out, l2_norm_mix, l2_rwkv_mu, l2_rwkv_w_r, l2_rwkv_w_k, l2_rwkv_w_v, l2_rwkv_w0, l2_rwkv_w_w1, l2_rwkv_w_w2, l2_rwkv_a0, l2_rwkv_a_w1, l2_rwkv_a_w2, l2_rwkv_g_w1, l2_rwkv_g_w2, l2_rwkv_k_k, l2_rwkv_k_a, l2_rwkv_r_k, l2_rwkv_ln_w, l2_rwkv_ln_b, l2_rwkv_w_o, l2_norm_ffn, l2_ffn_w_in, l2_ffn_w_out, l3_norm_mix, l3_attn_w_in, l3_attn_w_out, l3_norm_ffn, l3_ffn_w_in, l3_ffn_w_out, final_norm):
    b, s, d = x.shape
    h = x.reshape(b * s, d)
    rope = _rope_tables(positions)

    def channel_mixer(h, gain, w_in, w_out, final_gain=None):
        return ffn(h, gain, w_in.astype(BF16), w_out.astype(BF16), final_gain)

    h = attention_layer(h, b, s, rope, l0_norm_mix, l0_attn_w_in, l0_attn_w_out)
    h = channel_mixer(h, l0_norm_ffn, l0_ffn_w_in, l0_ffn_w_out)
    h = s5_layer(h, b, s, l1_norm_mix, l1_s5_a_re, l1_s5_a_im, l1_s5_log_dt, l1_s5_b_re, l1_s5_b_im,
                 l1_s5_c_re, l1_s5_c_im, l1_s5_d, l1_s5_w_glu)
    h = channel_mixer(h, l1_norm_ffn, l1_ffn_w_in, l1_ffn_w_out)
    h = rwkv_layer(h, b, s, l2_norm_mix, l2_rwkv_mu, l2_rwkv_w_r, l2_rwkv_w_k, l2_rwkv_w_v, l2_rwkv_w0,
                   l2_rwkv_w_w1, l2_rwkv_w_w2, l2_rwkv_a0, l2_rwkv_a_w1, l2_rwkv_a_w2, l2_rwkv_g_w1,
                   l2_rwkv_g_w2, l2_rwkv_k_k, l2_rwkv_k_a, l2_rwkv_r_k, l2_rwkv_ln_w, l2_rwkv_ln_b,
                   l2_rwkv_w_o)
    h = channel_mixer(h, l2_norm_ffn, l2_ffn_w_in, l2_ffn_w_out)
    h = attention_layer(h, b, s, rope, l3_norm_mix, l3_attn_w_in, l3_attn_w_out)
    h = channel_mixer(h, l3_norm_ffn, l3_ffn_w_in, l3_ffn_w_out, final_gain=final_norm)
    return h.reshape(b, s, d)
```

```python
import functools
import math

import jax
import jax.numpy as jnp
from jax import lax
from jax.experimental import pallas as pl
from jax.experimental.pallas import tpu as pltpu

F32 = jnp.float32
BF16 = jnp.bfloat16

RMS_EPS = 1e-6
LANES = 128
VMEM_LIMIT_BYTES = 56 * 1024 * 1024

ATTN_HEAD_DIM = 128
ATTN_HEADS = 8
ATTN_GROUPS = ((128, 1), (512, 4), (2048, 16))
ATTN_BLOCK = 128
ATTN_BLOCKS_PER_STEP = 4
ROPE_THETA = 500000.0
ROPE_DIM = ATTN_HEAD_DIM // 4
ATTN_WIDTH = ATTN_HEADS * ATTN_HEAD_DIM
QKV_COL_CHUNK = 256
NEG_BIG = -1e30

S5_GROUP = 16
S5_STATE = 64
S5_CHUNK = 16

RWKV_HEAD = 64
RWKV_CHUNK = 64
RWKV_HEADS_PER_UNIT = 4
RWKV_UNIT = RWKV_HEADS_PER_UNIT * RWKV_HEAD
RWKV_GN_EPS = 64e-5


def _params(*sem):
    return pltpu.CompilerParams(dimension_semantics=sem, vmem_limit_bytes=VMEM_LIMIT_BYTES)


def _tile(n, pref):
    t = min(n, pref)
    assert n % t == 0, (n, pref)
    return t


def _rms(x, g):
    return x * lax.rsqrt(jnp.mean(x * x, axis=-1, keepdims=True) + RMS_EPS) * g


def _nt(a, b):
    return lax.dot_general(a, b, (((1,), (1,)), ((), ())), preferred_element_type=F32)


def _tn(a, b):
    return lax.dot_general(a, b, (((0,), (0,)), ((), ())), preferred_element_type=F32)


def _dot(a, b):
    return jnp.dot(a, b, preferred_element_type=F32)


def _rmsnorm_kernel(x_ref, g_ref, o_ref):
    o_ref[...] = _rms(x_ref[...], g_ref[...]).astype(o_ref.dtype)


def rmsnorm(x, gain, out_dtype=F32, tm=512):
    t, d = x.shape
    tm = _tile(t, tm)
    return pl.pallas_call(
        _rmsnorm_kernel,
        grid=(t // tm,),
        in_specs=[pl.BlockSpec((tm, d), lambda i: (i, 0)), pl.BlockSpec((1, d), lambda i: (0, 0))],
        out_specs=pl.BlockSpec((tm, d), lambda i: (i, 0)),
        out_shape=jax.ShapeDtypeStruct((t, d), out_dtype),
        compiler_params=_params("parallel"),
        name="rmsnorm",
    )(x, gain.reshape(1, d))


def _mm_kernel(x_ref, w_ref, *refs):
    *r_ref, o_ref = refs
    acc = _dot(x_ref[...], w_ref[...])
    if r_ref:
        acc = acc + r_ref[0][...]
    o_ref[...] = acc.astype(o_ref.dtype)


def matmul(x, w, *, residual=None, out_dtype=F32, tm=1024, tn=1024):
    t, k = x.shape
    n = w.shape[1]
    tm, tn = _tile(t, tm), _tile(n, tn)
    ins = [x, w]
    specs = [pl.BlockSpec((tm, k), lambda i, j: (i, 0)), pl.BlockSpec((k, tn), lambda i, j: (0, j))]
    if residual is not None:
        ins.append(residual)
        specs.append(pl.BlockSpec((tm, tn), lambda i, j: (i, j)))
    return pl.pallas_call(
        _mm_kernel,
        grid=(t // tm, n // tn),
        in_specs=specs,
        out_specs=pl.BlockSpec((tm, tn), lambda i, j: (i, j)),
        out_shape=jax.ShapeDtypeStruct((t, n), out_dtype),
        compiler_params=_params("parallel", "parallel"),
        name="matmul",
    )(*ins)


def _ffn_kernel(x_ref, g_ref, wg_ref, wu_ref, wo_ref, fg_ref, o_ref, xn_ref, *, final_norm):
    j = pl.program_id(1)

    @pl.when(j == 0)
    def _():
        x = x_ref[...]
        xn_ref[...] = _rms(x, g_ref[...]).astype(BF16)
        o_ref[...] = x

    xn = xn_ref[...]
    gate = _dot(xn, wg_ref[...])
    up = _dot(xn, wu_ref[...])
    h = (gate * jax.nn.sigmoid(gate) * up).astype(BF16)
    o_ref[...] += _dot(h, wo_ref[...])

    if final_norm:
        @pl.when(j == pl.num_programs(1) - 1)
        def _():
            o_ref[...] = _rms(o_ref[...], fg_ref[...])


def ffn(x, gain, w_in, w_out, final_gain=None, tm=512, tf=512):
    t, d = x.shape
    f = w_out.shape[0]
    tm, tf = _tile(t, tm), _tile(f, tf)
    nf = f // tf
    fg = gain if final_gain is None else final_gain
    return pl.pallas_call(
        functools.partial(_ffn_kernel, final_norm=final_gain is not None),
        grid=(t // tm, nf),
        in_specs=[
            pl.BlockSpec((tm, d), lambda i, j: (i, 0)),
            pl.BlockSpec((1, d), lambda i, j: (0, 0)),
            pl.BlockSpec((d, tf), lambda i, j: (0, j)),
            pl.BlockSpec((d, tf), lambda i, j: (0, j + nf)),
            pl.BlockSpec((tf, d), lambda i, j: (j, 0)),
            pl.BlockSpec((1, d), lambda i, j: (0, 0)),
        ],
        out_specs=pl.BlockSpec((tm, d), lambda i, j: (i, 0)),
        out_shape=jax.ShapeDtypeStruct((t, d), F32),
        scratch_shapes=[pltpu.VMEM((tm, d), BF16)],
        compiler_params=_params("parallel", "arbitrary"),
        name="ffn",
    )(x, gain.reshape(1, d), w_in, w_in, w_out, fg.reshape(1, d))


def _norm_residues_kernel(x_ref, g_ref, *refs):
    out_refs, hn_ref = refs[:-1], refs[-1]
    hn = _rms(x_ref[0], g_ref[...])
    ncol, tm, lanes = hn_ref.shape
    for c in range(ncol):
        hn_ref[c] = hn[:, c * lanes:(c + 1) * lanes]
    for o_ref in out_refs:
        d = o_ref.shape[1]
        if d == 1:
            o_ref[0, 0] = hn.astype(o_ref.dtype)
            continue
        for r in range(d):
            for c in range(ncol):
                o_ref[0, r, :, c * lanes:(c + 1) * lanes] = (
                    hn_ref[c, pl.ds(r, tm // d, stride=d), :].astype(o_ref.dtype))


def norm_by_residue(x, gain, dilations, tm=512):
    b, s, d_model = x.shape
    tm = _tile(s, tm)
    return pl.pallas_call(
        _norm_residues_kernel,
        grid=(b, s // tm),
        in_specs=[pl.BlockSpec((1, tm, d_model), lambda bi, i: (bi, i, 0)),
                  pl.BlockSpec((1, d_model), lambda bi, i: (0, 0))],
        out_specs=[pl.BlockSpec((1, d, tm // d, d_model), lambda bi, i: (bi, 0, i, 0)) for d in dilations],
        out_shape=[jax.ShapeDtypeStruct((b, d, s // d, d_model), BF16) for d in dilations],
        scratch_shapes=[pltpu.VMEM((d_model // LANES, tm, LANES), F32)],
        compiler_params=_params("parallel", "parallel"),
        name="norm_by_residue",
    )(x, gain.reshape(1, d_model))


def _qkv_kernel(x_ref, w_ref, cos_ref, sin_ref, o_ref):
    kind = pl.program_id(1)
    scale = jnp.where(kind == 0, ATTN_HEAD_DIM ** -0.5, 1.0).astype(F32)
    is_v = kind == 2
    cos = jnp.where(is_v, 1.0, cos_ref[...] * scale)
    sin = jnp.where(is_v, 0.0, sin_ref[...] * scale)
    x = x_ref[...]
    for c0 in range(0, o_ref.shape[1], QKV_COL_CHUNK):
        acc = _dot(x, w_ref[:, c0:c0 + QKV_COL_CHUNK])
        for h0 in range(0, QKV_COL_CHUNK, ATTN_HEAD_DIM):
            xh = acc[:, h0:h0 + ATTN_HEAD_DIM]
            rot = pltpu.roll(xh, ATTN_HEAD_DIM // 2, 1)
            o_ref[:, c0 + h0:c0 + h0 + ATTN_HEAD_DIM] = (xh * cos + rot * sin).astype(o_ref.dtype)


def qkv_projection(x, w_in, gi, cos_t, sin_t, tm=1024):
    t, d = x.shape
    tm = _tile(t, tm)
    tn = ATTN_WIDTH
    return pl.pallas_call(
        _qkv_kernel,
        grid=(t // tm, 3),
        in_specs=[
            pl.BlockSpec((tm, d), lambda i, j: (i, 0)),
            pl.BlockSpec((d, tn), lambda i, j: (0, gi * 3 + j)),
            pl.BlockSpec((tm, ATTN_HEAD_DIM), lambda i, j: (i, 0)),
            pl.BlockSpec((tm, ATTN_HEAD_DIM), lambda i, j: (i, 0)),
        ],
        out_specs=pl.BlockSpec((tm, tn), lambda i, j: (i, j)),
        out_shape=jax.ShapeDtypeStruct((t, 3 * tn), BF16),
        compiler_params=_params("parallel", "parallel"),
        name="qkv_rope",
    )(x, w_in, cos_t, sin_t)


def _attn_kernel(q_ref, kp_ref, kc_ref, vp_ref, vc_ref, o_ref, st_ref):
    n = pl.program_id(1)
    blk = ATTN_BLOCK
    qi = lax.broadcasted_iota(jnp.int32, (blk, 2 * blk), 0)
    kj = lax.broadcasted_iota(jnp.int32, (blk, 2 * blk), 1)
    dist = qi + blk - kj
    band = (dist >= 0) & (dist <= blk)
    first = band & ((kj >= blk) | (n > 0))
    lane = lax.broadcasted_iota(jnp.int32, (blk, ATTN_HEAD_DIM), 1)
    stats = [jnp.zeros((blk, ATTN_HEAD_DIM), F32) for _ in range(ATTN_BLOCKS_PER_STEP)]
    for h in range(ATTN_HEADS):
        sl = slice(h * ATTN_HEAD_DIM, (h + 1) * ATTN_HEAD_DIM)
        k = jnp.concatenate([kp_ref[0, :, sl], kc_ref[0, :, sl]], axis=0)
        v = jnp.concatenate([vp_ref[0, :, sl], vc_ref[0, :, sl]], axis=0)
        for j in range(ATTN_BLOCKS_PER_STEP):
            rows = slice(j * blk, (j + 1) * blk)
            keys = slice(j * blk, (j + 2) * blk)
            s = jnp.where(first if j == 0 else band, _nt(q_ref[0, rows, sl], k[keys]), NEG_BIG)
            mx = jnp.max(s, axis=-1, keepdims=True)
            p = jnp.exp(s - mx)
            den = jnp.sum(p, axis=-1, keepdims=True)
            num = _dot(p.astype(BF16), v[keys])
            o_ref[0, rows, sl] = (num / den).astype(o_ref.dtype)
            stats[j] = jnp.where(lane == h, mx + jnp.log(den), stats[j])
    for j in range(ATTN_BLOCKS_PER_STEP):
        st_ref[0, j * blk:(j + 1) * blk, :] = stats[j]


def attention_group(qkv):
    nseq, length, _ = qkv.shape
    per = ATTN_BLOCKS_PER_STEP
    rows = per * ATTN_BLOCK
    assert length % rows == 0
    nb = length // rows
    blk = (1, rows, ATTN_WIDTH)

    def cur(off):
        return pl.BlockSpec(blk, lambda r, nn: (r, nn, off))

    def prev(off):
        return pl.BlockSpec((1, ATTN_BLOCK, ATTN_WIDTH), lambda r, nn: (r, jnp.maximum(nn * per - 1, 0), off))

    return pl.pallas_call(
        _attn_kernel,
        grid=(nseq, nb),
        in_specs=[cur(0), prev(1), cur(1), prev(2), cur(2)],
        out_specs=[
            pl.BlockSpec(blk, lambda r, nn: (r, nn, 0)),
            pl.BlockSpec((1, rows, ATTN_HEAD_DIM), lambda r, nn: (r, nn, 0)),
        ],
        out_shape=[
            jax.ShapeDtypeStruct((nseq, length, ATTN_WIDTH), BF16),
            jax.ShapeDtypeStruct((nseq, length, ATTN_HEAD_DIM), F32),
        ],
        compiler_params=_params("parallel", "parallel"),
        name="dilated_attn",
    )(qkv, qkv, qkv, qkv, qkv)


def _attn_out_kernel(*refs):
    ng = len(ATTN_GROUPS)
    o_refs, s_refs = refs[:ng], refs[ng:2 * ng]
    w_ref, r_ref, out_ref, om_ref, su_ref = refs[2 * ng:]

    @pl.when(pl.program_id(2) == 0)
    def _():
        tm = om_ref.shape[0]
        tok = lax.broadcasted_iota(jnp.int32, (tm, tm), 0)
        src = lax.broadcasted_iota(jnp.int32, (tm, tm), 1)
        outs = []
        for g in range(ng):
            d = o_refs[g].shape[1]
            for r in range(d):
                su_ref[g, pl.ds(r, tm // d, stride=d), :] = s_refs[g][0, r]
            o = o_refs[g][0].reshape(tm, ATTN_WIDTH)
            if d > 1:
                to_token_order = (src == (tok % d) * (tm // d) + tok // d).astype(BF16)
                o = _dot(to_token_order, o)
            outs.append(o)
        lse = [su_ref[g] for g in range(ng)]
        m = functools.reduce(jnp.maximum, lse)
        e = [jnp.exp(l - m) for l in lse]
        inv = 1.0 / sum(e)
        wts = [e[g] * inv for g in range(1, ng)]
        for h in range(ATTN_HEADS):
            sl = slice(h * ATTN_HEAD_DIM, (h + 1) * ATTN_HEAD_DIM)
            base = outs[0][:, sl].astype(F32)
            om = base + sum(wt[:, h:h + 1] * (o[:, sl].astype(F32) - base) for wt, o in zip(wts, outs[1:]))
            om_ref[:, sl] = om.astype(BF16)

    out_ref[0] = r_ref[0] + _dot(om_ref[...], w_ref[...])


def attention_out(parts, w_out, residual, tm=512, tn=2048):
    b, s, n = residual.shape
    k = w_out.shape[0]
    tm, tn = _tile(s, tm), _tile(n, tn)
    ng = len(parts)

    def by_residue(arr):
        d, width = arr.shape[1], arr.shape[3]
        return pl.BlockSpec((1, d, tm // d, width), lambda bi, i, j: (bi, 0, i, 0))

    res = pl.BlockSpec((1, tm, tn), lambda bi, i, j: (bi, i, j))
    return pl.pallas_call(
        _attn_out_kernel,
        grid=(b, s // tm, n // tn),
        in_specs=[by_residue(o) for o, _ in parts] + [by_residue(st) for _, st in parts]
        + [pl.BlockSpec((k, tn), lambda bi, i, j: (0, j)), res],
        out_specs=res,
        out_shape=jax.ShapeDtypeStruct((b, s, n), F32),
        scratch_shapes=[pltpu.VMEM((tm, k), BF16),
                        pltpu.VMEM((ng, tm, ATTN_HEAD_DIM), F32)],
        compiler_params=_params("parallel", "parallel", "arbitrary"),
        name="attn_merge_out",
    )(*[o for o, _ in parts], *[st for _, st in parts], w_out, residual)


def _by_residue(t, dilation):
    b, s = t.shape[:2]
    return jnp.swapaxes(t.reshape(b, s // dilation, dilation, *t.shape[2:]), 1, 2)


def _rope_layout(w_in):
    half = ROPE_DIM // 2
    d = w_in.shape[0]
    w = w_in.reshape(d, len(ATTN_GROUPS), 3, ATTN_HEADS, ATTN_HEAD_DIM)
    qk, v = w[:, :, :2], w[:, :, 2:]
    mid = ATTN_HEAD_DIM // 2 + half
    qk = jnp.concatenate([qk[..., :half], qk[..., ROPE_DIM:mid], qk[..., half:ROPE_DIM], qk[..., mid:]], axis=-1)
    return jnp.concatenate([qk, v], axis=2).reshape(w_in.shape)


def _rope_tables(positions):
    half = ROPE_DIM // 2
    inv_freq = ROPE_THETA ** (-jnp.arange(half, dtype=F32) * 2.0 / ROPE_DIM)
    lane = jnp.arange(ATTN_HEAD_DIM)
    rotary = lane % (ATTN_HEAD_DIM // 2) < half
    freq = jnp.where(rotary, inv_freq[lane % half], 0.0)
    sign = jnp.where(lane < half, -1.0, 1.0).astype(F32)
    tables = []
    for _, dil in ATTN_GROUPS:
        ang = _by_residue(positions, dil).astype(F32).reshape(-1, 1) * freq
        tables.append((jnp.cos(ang), sign * jnp.sin(ang)))
    return tables


def attention_layer(x, b, s, rope, gain, w_in, w_out):
    t, d = x.shape
    w_in = _rope_layout(w_in).astype(BF16)
    dils = [dil for _, dil in ATTN_GROUPS]
    hns = norm_by_residue(x.reshape(b, s, d), gain, dils)
    parts = []
    for gi, (dil, hn, (cos_t, sin_t)) in enumerate(zip(dils, hns, rope)):
        qkv = qkv_projection(hn.reshape(t, d), w_in, gi, cos_t, sin_t)
        o, st = attention_group(qkv.reshape(b * dil, s // dil, -1))
        parts.append((o.reshape(b, dil, s // dil, -1), st.reshape(b, dil, s // dil, -1)))
    return attention_out(parts, w_out.astype(BF16), x.reshape(b, s, d)).reshape(t, d)


def _gelu_tanh(y):
    return 0.5 * y * (1.0 + jnp.tanh(math.sqrt(2.0 / math.pi) * (y + 0.044715 * (y * y * y))))


def _s5_kernel(u_ref, perm_ref, toep_ref, wp_ref, q_ref, c1_ref, c2_ref, z_ref):
    lc = S5_CHUNK
    groups = toep_ref.shape[0]
    nch = u_ref.shape[1] // lc
    width = lc * S5_GROUP
    nsteps = c1_ref.shape[1]
    perm = perm_ref[...]
    by_pos = jnp.concatenate([u_ref[0, pl.ds(s, nch, stride=lc), :].astype(BF16) for s in range(lc)], axis=1)
    by_group = _dot(by_pos, perm).astype(BF16)
    row = lax.broadcasted_iota(jnp.int32, (nch, 2 * S5_STATE), 0)
    zs = []
    for g in range(groups):
        u = by_group[:, g * width:(g + 1) * width]
        y = _dot(u, toep_ref[g])
        x = _dot(u, wp_ref[g])
        for i in range(nsteps):
            m = 1 << i
            sh = jnp.where(row >= m, pltpu.roll(x, m, 0), 0.0)
            x = x + c1_ref[g, i:i + 1, :] * sh + c2_ref[g, i:i + 1, :] * pltpu.roll(sh, S5_STATE, 1)
        xprev = jnp.where(row >= 1, pltpu.roll(x, 1, 0), 0.0)
        y = y + _dot(xprev.astype(BF16), q_ref[g])
        zs.append(_gelu_tanh(y).astype(BF16))
    out = _nt(jnp.concatenate(zs, axis=1), perm)
    for s in range(lc):
        z_ref[0, pl.ds(s, nch, stride=lc), :] = out[:, s * LANES:(s + 1) * LANES]


def _s5_operators(a_re, a_im, log_dt, b_re, b_im, c_re, c_im, d_skip, chunks_per_seq):
    lc = S5_CHUNK
    g, p = a_re.shape
    dt = jnp.exp(log_dt)[:, None]
    mag = jnp.exp(dt * a_re)
    ab_re = mag * jnp.cos(dt * a_im)
    ab_im = mag * jnp.sin(dt * a_im)
    inv = 1.0 / (a_re * a_re + a_im * a_im)
    f_re = ((ab_re - 1.0) * a_re + ab_im * a_im) * inv
    f_im = (ab_im * a_re - (ab_re - 1.0) * a_im) * inv
    bb_re = f_re[..., None] * b_re - f_im[..., None] * b_im
    bb_im = f_re[..., None] * b_im + f_im[..., None] * b_re

    def power(j):
        jf = j.astype(F32)[:, None, None]
        m = jnp.exp(jf * (dt * a_re))
        return m * jnp.cos(jf * (dt * a_im)), m * jnp.sin(jf * (dt * a_im))

    pr, pi = power(jnp.arange(lc + 1))
    ba_re = pr[..., None] * bb_re - pi[..., None] * bb_im
    ba_im = pr[..., None] * bb_im + pi[..., None] * bb_re
    kern = (jnp.einsum("jgpa,gcp->jgac", ba_re, c_re) - jnp.einsum("jgpa,gcp->jgac", ba_im, c_im))
    ti = jnp.arange(lc)
    lag = ti[None, :] - ti[:, None]
    toep = jnp.where((lag >= 0)[None, :, None, :, None],
                     jnp.transpose(kern[jnp.clip(lag, 0, lc)], (2, 0, 3, 1, 4)), 0.0)
    eye = jnp.eye(lc, dtype=F32)[:, None, :, None] * jnp.eye(S5_GROUP, dtype=F32)[None, :, None, :]
    toep = toep + d_skip.reshape(g, 1, S5_GROUP, 1, 1) * eye[None]
    toep = toep.reshape(g, lc * S5_GROUP, lc * S5_GROUP)
    rev = lc - 1 - ti
    wp = jnp.concatenate([ba_re[rev], ba_im[rev]], axis=2)
    wp = jnp.transpose(wp, (1, 0, 3, 2)).reshape(g, lc * S5_GROUP, 2 * p)
    qr, qi = pr[1:], pi[1:]
    q_top = (jnp.einsum("tgp,gcp->gptc", qr, c_re) - jnp.einsum("tgp,gcp->gptc", qi, c_im))
    q_bot = (-jnp.einsum("tgp,gcp->gptc", qi, c_re) - jnp.einsum("tgp,gcp->gptc", qr, c_im))
    q = jnp.concatenate([q_top, q_bot], axis=1).reshape(g, 2 * p, lc * S5_GROUP)
    nsteps = chunks_per_seq.bit_length() - 1
    sr, si = power(lc * (2 ** jnp.arange(nsteps)))
    c1 = jnp.transpose(jnp.concatenate([sr, sr], axis=2), (1, 0, 2))
    c2 = jnp.transpose(jnp.concatenate([-si, si], axis=2), (1, 0, 2))
    return toep.astype(BF16), wp.astype(BF16), q.astype(BF16), c1, c2


def s5_core(u, ops):
    b, s, d = u.shape
    lc = S5_CHUNK
    cps = s // lc
    assert cps & (cps - 1) == 0 and d % LANES == 0
    gs = LANES // S5_GROUP
    width = lc * S5_GROUP
    toep, wp, q, c1, c2 = ops
    nsteps = c1.shape[1]
    i = jnp.arange(lc * LANES)
    dest = (i % LANES // S5_GROUP) * width + (i // LANES) * S5_GROUP + i % S5_GROUP
    perm = (dest[:, None] == i[None, :]).astype(BF16)
    per_tile = lambda shape: pl.BlockSpec((gs,) + shape, lambda ti, bi: (ti, 0, 0))
    act = pl.BlockSpec((1, s, LANES), lambda ti, bi: (bi, 0, ti))
    return pl.pallas_call(
        _s5_kernel,
        grid=(d // LANES, b),
        in_specs=[
            act,
            pl.BlockSpec((lc * LANES, lc * LANES), lambda ti, bi: (0, 0), pipeline_mode=pl.Buffered(1)),
            per_tile((width, width)),
            per_tile((width, 2 * S5_STATE)),
            per_tile((2 * S5_STATE, width)),
            per_tile((nsteps, 2 * S5_STATE)),
            per_tile((nsteps, 2 * S5_STATE)),
        ],
        out_specs=act,
        out_shape=jax.ShapeDtypeStruct((b, s, d), F32),
        compiler_params=_params("parallel", "parallel"),
        name="s5_chunk_scan",
    )(u, perm, toep, wp, q, c1, c2)


def _glu_kernel(z_ref, wv_ref, wg_ref, r_ref, o_ref):
    z = z_ref[...].astype(BF16)
    val = _dot(z, wv_ref[...])
    gate = _dot(z, wg_ref[...])
    o_ref[...] = r_ref[...] + val * jax.nn.sigmoid(gate)


def glu_out(z, w_glu, residual, tm=1024, tn=1024):
    t, k = z.shape
    n = w_glu.shape[1] // 2
    tm, tn = _tile(t, tm), _tile(n, tn)
    nn = n // tn
    return pl.pallas_call(
        _glu_kernel,
        grid=(t // tm, nn),
        in_specs=[
            pl.BlockSpec((tm, k), lambda i, j: (i, 0)),
            pl.BlockSpec((k, tn), lambda i, j: (0, j)),
            pl.BlockSpec((k, tn), lambda i, j: (0, j + nn)),
            pl.BlockSpec((tm, tn), lambda i, j: (i, j)),
        ],
        out_specs=pl.BlockSpec((tm, tn), lambda i, j: (i, j)),
        out_shape=jax.ShapeDtypeStruct((t, n), F32),
        compiler_params=_params("parallel", "parallel"),
        name="glu_out",
    )(z, w_glu, w_glu, residual)


def s5_layer(x, b, s, gain, a_re, a_im, log_dt, b_re, b_im, c_re, c_im, d_skip, w_glu):
    u = rmsnorm(x, gain)
    ops = _s5_operators(a_re, a_im, log_dt, b_re, b_im, c_re, c_im, d_skip, s // S5_CHUNK)
    z = s5_core(u.reshape(b, s, -1), ops)
    return glu_out(z.reshape(x.shape), w_glu.astype(BF16), x)


def _split2(x):
    hi = x.astype(BF16)
    lo = (x - hi.astype(F32)).astype(BF16)
    return hi, lo


def _rwkv_kernel(r_ref, k_ref, v_ref, hw_ref, ha_ref, hg_ref, ww2_ref, aw2_ref, gw2_ref, w0_ref, a0_ref,
                 kk_ref, ka_ref, rk_ref, lnw_ref, lnb_ref, o_ref, state_ref):
    nb, c, width = r_ref.shape
    unit = RWKV_UNIT
    nu = width // unit
    reps = unit // c

    @pl.when(pl.program_id(1) == 0)
    def _():
        state_ref[...] = jnp.zeros_like(state_ref)

    ri = lax.broadcasted_iota(jnp.int32, (unit, unit), 0)
    ci = lax.broadcasted_iota(jnp.int32, (unit, unit), 1)
    head_bd = (ri // RWKV_HEAD) == (ci // RWKV_HEAD)
    stack_bd = (ri // c) == (ci // RWKV_HEAD)
    chunk_bd = (ri // c) == (ci // c)
    ones_bd = head_bd.astype(BF16)
    tr = lax.broadcasted_iota(jnp.int32, (c, unit), 0)
    tc = lax.broadcasted_iota(jnp.int32, (c, unit), 1) % c
    eye_cat = (tc == tr).astype(F32)
    strict = tc < tr
    incl = tc <= tr
    li = lax.broadcasted_iota(jnp.int32, (c, c), 0)
    lj = lax.broadcasted_iota(jnp.int32, (c, c), 1)
    tri = (lj <= li).astype(BF16)

    def headsums(*cols):
        pieces = [p for xs in zip(*cols) for x in xs for p in _split2(x)]
        parts = _dot(jnp.concatenate(pieces, axis=0), ones_bd)
        sums = [parts[2 * i * c:(2 * i + 1) * c] + parts[(2 * i + 1) * c:(2 * i + 2) * c]
                for i in range(len(pieces) // 2)]
        return [sums[i::len(cols)] for i in range(len(cols))]

    def stack(x, mask):
        return jnp.where(mask, jnp.concatenate([x] * reps, axis=0), 0.0).astype(BF16)

    units = [(b, slice(q * unit, (q + 1) * unit), q) for b in range(nb) for q in range(nu)]
    each = lambda fn, *cols: [fn(*vals) for vals in zip(*cols)]

    def load(ref):
        return [ref[b, :, sl] for b, sl, _ in units]

    def row(ref):
        return [ref[:, sl] for _, sl, _ in units]

    r, k, v = (load(ref) for ref in (r_ref, k_ref, v_ref))
    k_k, k_a, r_k, ln_w, ln_b = (row(ref) for ref in (kk_ref, ka_ref, rk_ref, lnw_ref, lnb_ref))

    def low_rank(h_ref, w2_ref):
        wide = [_dot(h_ref[b], w2_ref[...]) for b in range(nb)]
        return [wide[b][:, sl] for b, sl, _ in units]

    wz = each(lambda z, bias: z + bias, low_rank(hw_ref, ww2_ref), row(w0_ref))
    az = each(lambda z, bias: z + bias, low_rank(ha_ref, aw2_ref), row(a0_ref))
    gate = low_rank(hg_ref, gw2_ref)

    logw = each(lambda z: -jnp.exp(-jax.nn.softplus(-z) - 0.5), wz)
    a = each(jax.nn.sigmoid, az)
    kk0 = each(lambda x, y: x * y, k, k_k)
    k2 = each(lambda x, al, ka: x * (1.0 + (al - 1.0) * ka), k, a, k_a)
    ssq, rk_sum = headsums(each(lambda x: x * x, kk0), each(lambda rr, kv, rk: rr * kv * rk, r, k2, r_k))
    kk = each(lambda x, s: x * lax.rsqrt(jnp.maximum(s, 1e-24)), kk0, ssq)
    bonus = each(lambda s, vv: s * vv, rk_sum, v)
    bvec = each(lambda x, al: x * al, kk, a)

    def cumsum(lw):
        p0 = lw.astype(BF16)
        r1 = lw - p0.astype(F32)
        p1 = r1.astype(BF16)
        p2 = (r1 - p1.astype(F32)).astype(BF16)
        return _dot(tri, p0) + _dot(tri, p1) + _dot(tri, p2)

    cum = each(cumsum, logw)
    cend = each(lambda x: x[c - 1:c, :], cum)
    a_t = each(lambda x, cu, lw: (-x * jnp.exp(cu - lw)).astype(BF16), kk, cum, logw)
    r_t = each(lambda x, cu: (x * jnp.exp(cu)).astype(BF16), r, cum)
    w_inv = each(lambda cu: jnp.exp(-cu), cum)
    w_rest = each(lambda ce, cu: jnp.exp(ce - cu), cend, cum)
    bs = each(lambda x, w: stack(x * w, stack_bd), bvec, w_inv)
    ks = each(lambda x, w: stack(x * w, stack_bd), k2, w_inv)
    vs = each(lambda x: stack(x, stack_bd), v)

    scores = each(lambda at, rt, b_, k_: _nt(jnp.concatenate([at, rt], axis=0), jnp.concatenate([b_, k_], axis=0)),
                  a_t, r_t, bs, ks)
    s_k = each(lambda s: jnp.concatenate([jnp.where(strict, s[:c, unit:], 0.0), jnp.where(incl, s[c:, unit:], 0.0)],
                                         axis=0).astype(BF16), scores)
    s_rb = each(lambda s: jnp.where(incl, s[c:, :unit], 0.0).astype(BF16), scores)

    pw = each(lambda s: jnp.where(strict, s[:c, :unit], 0.0), scores)
    t_cat = each(lambda p: eye_cat + p, pw)
    pw = each(lambda p: _dot(p.astype(BF16), stack(p, chunk_bd)), pw)
    for _ in range(c.bit_length() - 3):
        both = each(lambda t, p: _dot(jnp.concatenate([t, p], axis=0).astype(BF16), stack(p, chunk_bd)), t_cat, pw)
        t_cat = each(lambda t, r_: t + r_[:c], t_cat, both)
        pw = each(lambda r_: r_[c:], both)
    t_cat = each(lambda t, p: t + _dot(t.astype(BF16), stack(p, chunk_bd)), t_cat, pw)

    state = [state_ref[b, q] for b, _, q in units]
    state_b = each(lambda s: s.astype(BF16), state)
    base = each(lambda at, rt, sb, sk, vv: _nt(jnp.concatenate([at, rt], axis=0), sb) + _dot(sk, vv),
                a_t, r_t, state_b, s_k, vs)
    u = each(lambda t, bb: _dot(t.astype(BF16), stack(bb[:c], stack_bd)), t_cat, base)
    y = each(lambda bb, srb, uu: bb[c:] + _dot(srb, stack(uu, stack_bd)), base, s_rb, u)
    upd = each(lambda uu, vv, bv, kv, wr: _tn(jnp.concatenate([uu, vv], axis=0).astype(BF16),
                                               jnp.concatenate([bv * wr, kv * wr], axis=0).astype(BF16)),
               u, v, bvec, k2, w_rest)
    for (b, _, q), s, ce, up in zip(units, state, cend, upd):
        state_ref[b, q] = jnp.where(head_bd, s * jnp.exp(ce) + up, 0.0)

    inv_n = 1.0 / RWKV_HEAD
    (y_sum,) = headsums(y)
    yc = each(lambda yy, m: yy - m * inv_n, y, y_sum)
    (sq_sum,) = headsums(each(lambda z: z * z, yc))
    var = each(lambda s: s * inv_n, sq_sum)
    for (b, sl, _), z, vr, lw, lb, bo, gt in zip(units, yc, var, ln_w, ln_b, bonus, gate):
        o_ref[b, :, sl] = ((z * lax.rsqrt(vr + RWKV_GN_EPS) * lw + lb + bo) * gt).astype(o_ref.dtype)


def rwkv_core(r, k, v, low, low_w2, w0, a0, k_k, k_a, r_k, ln_w, ln_b, units_per_step=4):
    b, s, d = r.shape
    c = RWKV_CHUNK
    assert s % c == 0 and RWKV_UNIT % c == 0
    width = _tile(d, RWKV_UNIT * units_per_step)
    nu = width // RWKV_UNIT
    act = pl.BlockSpec((b, c, width), lambda i, j: (0, j, i))
    row = pl.BlockSpec((1, width), lambda i, j: (0, i))
    rows = [t.reshape(1, d).astype(F32) for t in (w0, a0, k_k, k_a, r_k, ln_w, ln_b)]
    return pl.pallas_call(
        _rwkv_kernel,
        grid=(d // width, s // c),
        in_specs=[act] * 3
        + [pl.BlockSpec((b, c, h.shape[2]), lambda i, j: (0, j, 0)) for h in low]
        + [pl.BlockSpec((w.shape[0], width), lambda i, j: (0, i)) for w in low_w2]
        + [row] * len(rows),
        out_specs=act,
        out_shape=jax.ShapeDtypeStruct((b, s, d), BF16),
        scratch_shapes=[pltpu.VMEM((b, nu, RWKV_UNIT, RWKV_UNIT), F32)],
        compiler_params=_params("parallel", "arbitrary"),
        name="rwkv7_chunk",
    )(r, k, v, *low, *low_w2, *rows)


def _rwkv_mix_kernel(x_ref, xp_ref, g_ref, mu_ref, ww1_ref, aw1_ref, gw1_ref,
                     xr_ref, xk_ref, xv_ref, hw_ref, ha_ref, hg_ref):
    gain = g_ref[...]
    hn = _rms(x_ref[0], gain)
    tail = xp_ref.shape[1]
    prev_last = _rms(xp_ref[0, tail - 1:tail, :], gain)
    prev_last = jnp.where(pl.program_id(1) > 0, prev_last, 0.0)
    row = lax.broadcasted_iota(jnp.int32, hn.shape, 0)
    xx = jnp.where(row == 0, prev_last, pltpu.roll(hn, 1, 0)) - hn
    mix = lambda i: (hn + xx * mu_ref[i:i + 1, :]).astype(BF16)
    xr_ref[0] = mix(0)
    xk_ref[0] = mix(2)
    xv_ref[0] = mix(3)
    hw_ref[0] = jnp.tanh(_dot(mix(1), ww1_ref[...])).astype(BF16)
    ha_ref[0] = _dot(mix(4), aw1_ref[...]).astype(BF16)
    hg_ref[0] = jax.nn.sigmoid(_dot(mix(5), gw1_ref[...])).astype(BF16)


def rwkv_mix(x, gain, mu, w_w1, a_w1, g_w1, tm=512, tail=8):
    b, s, d = x.shape
    tm = _tile(s, tm)
    tok = lambda width: pl.BlockSpec((1, tm, width), lambda bi, i: (bi, i, 0))
    whole = lambda arr: pl.BlockSpec(arr.shape, lambda bi, i: (0, 0))
    ranks = [w.shape[1] for w in (w_w1, a_w1, g_w1)]
    mu = mu.astype(F32)
    return pl.pallas_call(
        _rwkv_mix_kernel,
        grid=(b, s // tm),
        in_specs=[tok(d),
                  pl.BlockSpec((1, tail, d), lambda bi, i: (bi, jnp.maximum(i * (tm // tail) - 1, 0), 0)),
                  pl.BlockSpec((1, d), lambda bi, i: (0, 0)),
                  whole(mu), whole(w_w1), whole(a_w1), whole(g_w1)],
        out_specs=[tok(d)] * 3 + [tok(rk) for rk in ranks],
        out_shape=[jax.ShapeDtypeStruct((b, s, d), BF16)] * 3
        + [jax.ShapeDtypeStruct((b, s, rk), BF16) for rk in ranks],
        compiler_params=_params("parallel", "parallel"),
        name="rwkv_mix",
    )(x, x, gain.reshape(1, d), mu, w_w1, a_w1, g_w1)


def _pad_cols(w, mult=128):
    n = w.shape[1]
    return jnp.pad(w, ((0, 0), (0, (-n) % mult)))


def _pad_rows(w, mult=128):
    n = w.shape[0]
    return jnp.pad(w, ((0, (-n) % mult), (0, 0)))


def rwkv_layer(x, b, s, gain, mu, w_r, w_k, w_v, w0, w_w1, w_w2, a0, a_w1, a_w2, g_w1, g_w2,
               k_k, k_a, r_k, ln_w, ln_b, w_o):
    t, d = x.shape
    bf = lambda w: w.astype(BF16)
    xr, xk, xv, *low = rwkv_mix(x.reshape(b, s, d), gain, mu,
                                bf(_pad_cols(w_w1)), bf(_pad_cols(a_w1)), bf(_pad_cols(g_w1)))
    r, k, v = (matmul(m.reshape(t, d), bf(w)).reshape(b, s, d) for m, w in ((xr, w_r), (xk, w_k), (xv, w_v)))
    low_w2 = [bf(_pad_rows(w)) for w in (w_w2, a_w2, g_w2)]
    y = rwkv_core(r, k, v, low, low_w2, w0, a0, k_k, k_a, r_k, ln_w, ln_b)
    return matmul(y.reshape(t, d), bf(w_o), residual=x)


def kernel(x, positions, l0_norm_mix, l0_attn_w_in, l0_attn_w_out, l0_norm_ffn, l0_ffn_w_in, l0_ffn_w_out, l1_norm_mix, l1_s5_a_re, l1_s5_a_im, l1_s5_log_dt, l1_s5_b_re, l1_s5_b_im, l1_s5_c_re, l1_s5_c_im, l1_s5_d, l1_s5_w_glu, l1_norm_ffn, l1_ffn_w_in, l1_ffn_w_out, l2_norm_mix, l2_rwkv_mu, l2_rwkv_w_r, l2_rwkv_w_k, l2_rwkv_w_v, l2_rwkv_w0, l2_rwkv_w_w1, l2_rwkv_w_w2, l2_rwkv_a0, l2_rwkv_a_w1, l2_rwkv_a_w2, l2_rwkv_g_w1, l2_rwkv_g_w2, l2_rwkv_k_k, l2_rwkv_k_a, l2_rwkv_r_k, l2_rwkv_ln_w, l2_rwkv_ln_b, l2_rwkv_w_o, l2_norm_ffn, l2_ffn_w_in, l2_ffn_w_out, l3_norm_mix, l3_attn_w_in, l3_attn_w_out, l3_norm_ffn, l3_ffn_w_in, l3_ffn_w_out, final_norm):
    b, s, d = x.shape
    h = x.reshape(b * s, d)
    rope = _rope_tables(positions)

    def channel_mixer(h, gain, w_in, w_out, final_gain=None):
        return ffn(h, gain, w_in.astype(BF16), w_out.astype(BF16), final_gain)

    h = attention_layer(h, b, s, rope, l0_norm_mix, l0_attn_w_in, l0_attn_w_out)
    h = channel_mixer(h, l0_norm_ffn, l0_ffn_w_in, l0_ffn_w_out)
    h = s5_layer(h, b, s, l1_norm_mix, l1_s5_a_re, l1_s5_a_im, l1_s5_log_dt, l1_s5_b_re, l1_s5_b_im,
                 l1_s5_c_re, l1_s5_c_im, l1_s5_d, l1_s5_w_glu)
    h = channel_mixer(h, l1_norm_ffn, l1_ffn_w_in, l1_ffn_w_out)
    h = rwkv_layer(h, b, s, l2_norm_mix, l2_rwkv_mu, l2_rwkv_w_r, l2_rwkv_w_k, l2_rwkv_w_v, l2_rwkv_w0,
                   l2_rwkv_w_w1, l2_rwkv_w_w2, l2_rwkv_a0, l2_rwkv_a_w1, l2_rwkv_a_w2, l2_rwkv_g_w1,
                   l2_rwkv_g_w2, l2_rwkv_k_k, l2_rwkv_k_a, l2_rwkv_r_k, l2_rwkv_ln_w, l2_rwkv_ln_b,
                   l2_rwkv_w_o)
    h = channel_mixer(h, l2_norm_ffn, l2_ffn_w_in, l2_ffn_w_out)
    h = attention_layer(h, b, s, rope, l3_norm_mix, l3_attn_w_in, l3_attn_w_out)
    h = channel_mixer(h, l3_norm_ffn, l3_ffn_w_in, l3_ffn_w_out, final_gain=final_norm)
    return h.reshape(b, s, d)
```

```python
import functools
import math

import jax
import jax.numpy as jnp
from jax import lax
from jax.experimental import pallas as pl
from jax.experimental.pallas import tpu as pltpu

F32 = jnp.float32
BF16 = jnp.bfloat16

RMS_EPS = 1e-6
LANES = 128
VMEM_LIMIT_BYTES = 56 * 1024 * 1024

ATTN_HEAD_DIM = 128
ATTN_HEADS = 8
ATTN_GROUPS = ((128, 1), (512, 4), (2048, 16))
ATTN_BLOCK = 128
ATTN_BLOCKS_PER_STEP = 4
ROPE_THETA = 500000.0
ROPE_DIM = ATTN_HEAD_DIM // 4
ATTN_WIDTH = ATTN_HEADS * ATTN_HEAD_DIM
QKV_COL_CHUNK = 256
NEG_BIG = -1e30

S5_GROUP = 16
S5_STATE = 64
S5_CHUNK = 16

RWKV_HEAD = 64
RWKV_CHUNK = 64
RWKV_HEADS_PER_UNIT = 4
RWKV_UNIT = RWKV_HEADS_PER_UNIT * RWKV_HEAD
RWKV_GN_EPS = 64e-5


def _params(*sem):
    return pltpu.CompilerParams(dimension_semantics=sem, vmem_limit_bytes=VMEM_LIMIT_BYTES)


def _tile(n, pref):
    t = min(n, pref)
    assert n % t == 0, (n, pref)
    return t


def _rms(x, g):
    return x * lax.rsqrt(jnp.mean(x * x, axis=-1, keepdims=True) + RMS_EPS) * g


def _nt(a, b):
    return lax.dot_general(a, b, (((1,), (1,)), ((), ())), preferred_element_type=F32)


def _tn(a, b):
    return lax.dot_general(a, b, (((0,), (0,)), ((), ())), preferred_element_type=F32)


def _dot(a, b):
    return jnp.dot(a, b, preferred_element_type=F32)


def _rmsnorm_kernel(x_ref, g_ref, o_ref):
    o_ref[...] = _rms(x_ref[...], g_ref[...]).astype(o_ref.dtype)


def rmsnorm(x, gain, out_dtype=F32, tm=512):
    t, d = x.shape
    tm = _tile(t, tm)
    return pl.pallas_call(
        _rmsnorm_kernel,
        grid=(t // tm,),
        in_specs=[pl.BlockSpec((tm, d), lambda i: (i, 0)), pl.BlockSpec((1, d), lambda i: (0, 0))],
        out_specs=pl.BlockSpec((tm, d), lambda i: (i, 0)),
        out_shape=jax.ShapeDtypeStruct((t, d), out_dtype),
        compiler_params=_params("parallel"),
        name="rmsnorm",
    )(x, gain.reshape(1, d))


def _mm_kernel(x_ref, w_ref, *refs):
    *r_ref, o_ref = refs
    acc = _dot(x_ref[...], w_ref[...])
    if r_ref:
        acc = acc + r_ref[0][...]
    o_ref[...] = acc.astype(o_ref.dtype)


def matmul(x, w, *, residual=None, out_dtype=F32, tm=1024, tn=1024):
    t, k = x.shape
    n = w.shape[1]
    tm, tn = _tile(t, tm), _tile(n, tn)
    ins = [x, w]
    specs = [pl.BlockSpec((tm, k), lambda i, j: (i, 0)), pl.BlockSpec((k, tn), lambda i, j: (0, j))]
    if residual is not None:
        ins.append(residual)
        specs.append(pl.BlockSpec((tm, tn), lambda i, j: (i, j)))
    return pl.pallas_call(
        _mm_kernel,
        grid=(t // tm, n // tn),
        in_specs=specs,
        out_specs=pl.BlockSpec((tm, tn), lambda i, j: (i, j)),
        out_shape=jax.ShapeDtypeStruct((t, n), out_dtype),
        compiler_params=_params("parallel", "parallel"),
        name="matmul",
    )(*ins)


def _ffn_kernel(x_ref, g_ref, wg_ref, wu_ref, wo_ref, fg_ref, o_ref, xn_ref, *, final_norm):
    j = pl.program_id(1)

    @pl.when(j == 0)
    def _():
        x = x_ref[...]
        xn_ref[...] = _rms(x, g_ref[...]).astype(BF16)
        o_ref[...] = x

    xn = xn_ref[...]
    gate = _dot(xn, wg_ref[...])
    up = _dot(xn, wu_ref[...])
    h = (gate * jax.nn.sigmoid(gate) * up).astype(BF16)
    o_ref[...] += _dot(h, wo_ref[...])

    if final_norm:
        @pl.when(j == pl.num_programs(1) - 1)
        def _():
            o_ref[...] = _rms(o_ref[...], fg_ref[...])


def ffn(x, gain, w_in, w_out, final_gain=None, tm=512, tf=512):
    t, d = x.shape
    f = w_out.shape[0]
    tm, tf = _tile(t, tm), _tile(f, tf)
    nf = f // tf
    fg = gain if final_gain is None else final_gain
    return pl.pallas_call(
        functools.partial(_ffn_kernel, final_norm=final_gain is not None),
        grid=(t // tm, nf),
        in_specs=[
            pl.BlockSpec((tm, d), lambda i, j: (i, 0)),
            pl.BlockSpec((1, d), lambda i, j: (0, 0)),
            pl.BlockSpec((d, tf), lambda i, j: (0, j)),
            pl.BlockSpec((d, tf), lambda i, j: (0, j + nf)),
            pl.BlockSpec((tf, d), lambda i, j: (j, 0)),
            pl.BlockSpec((1, d), lambda i, j: (0, 0)),
        ],
        out_specs=pl.BlockSpec((tm, d), lambda i, j: (i, 0)),
        out_shape=jax.ShapeDtypeStruct((t, d), F32),
        scratch_shapes=[pltpu.VMEM((tm, d), BF16)],
        compiler_params=_params("parallel", "arbitrary"),
        name="ffn",
    )(x, gain.reshape(1, d), w_in, w_in, w_out, fg.reshape(1, d))


def _norm_residues_kernel(x_ref, g_ref, *refs):
    out_refs, hn_ref = refs[:-1], refs[-1]
    hn = _rms(x_ref[0], g_ref[...])
    ncol, tm, lanes = hn_ref.shape
    for c in range(ncol):
        hn_ref[c] = hn[:, c * lanes:(c + 1) * lanes]
    for o_ref in out_refs:
        d = o_ref.shape[1]
        if d == 1:
            o_ref[0, 0] = hn.astype(o_ref.dtype)
            continue
        for r in range(d):
            for c in range(ncol):
                o_ref[0, r, :, c * lanes:(c + 1) * lanes] = (
                    hn_ref[c, pl.ds(r, tm // d, stride=d), :].astype(o_ref.dtype))


def norm_by_residue(x, gain, dilations, tm=512):
    b, s, d_model = x.shape
    tm = _tile(s, tm)
    return pl.pallas_call(
        _norm_residues_kernel,
        grid=(b, s // tm),
        in_specs=[pl.BlockSpec((1, tm, d_model), lambda bi, i: (bi, i, 0)),
                  pl.BlockSpec((1, d_model), lambda bi, i: (0, 0))],
        out_specs=[pl.BlockSpec((1, d, tm // d, d_model), lambda bi, i: (bi, 0, i, 0)) for d in dilations],
        out_shape=[jax.ShapeDtypeStruct((b, d, s // d, d_model), BF16) for d in dilations],
        scratch_shapes=[pltpu.VMEM((d_model // LANES, tm, LANES), F32)],
        compiler_params=_params("parallel", "parallel"),
        name="norm_by_residue",
    )(x, gain.reshape(1, d_model))


def _qkv_kernel(x_ref, w_ref, cos_ref, sin_ref, o_ref):
    kind = pl.program_id(1)
    scale = jnp.where(kind == 0, ATTN_HEAD_DIM ** -0.5, 1.0).astype(F32)
    is_v = kind == 2
    cos = jnp.where(is_v, 1.0, cos_ref[...] * scale)
    sin = jnp.where(is_v, 0.0, sin_ref[...] * scale)
    x = x_ref[...]
    for c0 in range(0, o_ref.shape[1], QKV_COL_CHUNK):
        acc = _dot(x, w_ref[:, c0:c0 + QKV_COL_CHUNK])
        for h0 in range(0, QKV_COL_CHUNK, ATTN_HEAD_DIM):
            xh = acc[:, h0:h0 + ATTN_HEAD_DIM]
            rot = pltpu.roll(xh, ATTN_HEAD_DIM // 2, 1)
            o_ref[:, c0 + h0:c0 + h0 + ATTN_HEAD_DIM] = (xh * cos + rot * sin).astype(o_ref.dtype)


def qkv_projection(x, w_in, gi, cos_t, sin_t, tm=1024):
    t, d = x.shape
    tm = _tile(t, tm)
    tn = ATTN_WIDTH
    return pl.pallas_call(
        _qkv_kernel,
        grid=(t // tm, 3),
        in_specs=[
            pl.BlockSpec((tm, d), lambda i, j: (i, 0)),
            pl.BlockSpec((d, tn), lambda i, j: (0, gi * 3 + j)),
            pl.BlockSpec((tm, ATTN_HEAD_DIM), lambda i, j: (i, 0)),
            pl.BlockSpec((tm, ATTN_HEAD_DIM), lambda i, j: (i, 0)),
        ],
        out_specs=pl.BlockSpec((tm, tn), lambda i, j: (i, j)),
        out_shape=jax.ShapeDtypeStruct((t, 3 * tn), BF16),
        compiler_params=_params("parallel", "parallel"),
        name="qkv_rope",
    )(x, w_in, cos_t, sin_t)


def _attn_kernel(q_ref, kp_ref, kc_ref, vp_ref, vc_ref, o_ref, st_ref):
    n = pl.program_id(1)
    blk = ATTN_BLOCK
    qi = lax.broadcasted_iota(jnp.int32, (blk, 2 * blk), 0)
    kj = lax.broadcasted_iota(jnp.int32, (blk, 2 * blk), 1)
    dist = qi + blk - kj
    band = (dist >= 0) & (dist <= blk)
    first = band & ((kj >= blk) | (n > 0))
    lane = lax.broadcasted_iota(jnp.int32, (blk, ATTN_HEAD_DIM), 1)
    stats = [jnp.zeros((blk, ATTN_HEAD_DIM), F32) for _ in range(ATTN_BLOCKS_PER_STEP)]
    for h in range(ATTN_HEADS):
        sl = slice(h * ATTN_HEAD_DIM, (h + 1) * ATTN_HEAD_DIM)
        k = jnp.concatenate([kp_ref[0, :, sl], kc_ref[0, :, sl]], axis=0)
        v = jnp.concatenate([vp_ref[0, :, sl], vc_ref[0, :, sl]], axis=0)
        for j in range(ATTN_BLOCKS_PER_STEP):
            rows = slice(j * blk, (j + 1) * blk)
            keys = slice(j * blk, (j + 2) * blk)
            s = jnp.where(first if j == 0 else band, _nt(q_ref[0, rows, sl], k[keys]), NEG_BIG)
            mx = jnp.max(s, axis=-1, keepdims=True)
            p = jnp.exp(s - mx)
            den = jnp.sum(p, axis=-1, keepdims=True)
            num = _dot(p.astype(BF16), v[keys])
            o_ref[0, rows, sl] = (num / den).astype(o_ref.dtype)
            stats[j] = jnp.where(lane == h, mx + jnp.log(den), stats[j])
    for j in range(ATTN_BLOCKS_PER_STEP):
        st_ref[0, j * blk:(j + 1) * blk, :] = stats[j]


def attention_group(qkv):
    nseq, length, _ = qkv.shape
    per = ATTN_BLOCKS_PER_STEP
    rows = per * ATTN_BLOCK
    assert length % rows == 0
    nb = length // rows
    blk = (1, rows, ATTN_WIDTH)

    def cur(off):
        return pl.BlockSpec(blk, lambda r, nn: (r, nn, off))

    def prev(off):
        return pl.BlockSpec((1, ATTN_BLOCK, ATTN_WIDTH), lambda r, nn: (r, jnp.maximum(nn * per - 1, 0), off))

    return pl.pallas_call(
        _attn_kernel,
        grid=(nseq, nb),
        in_specs=[cur(0), prev(1), cur(1), prev(2), cur(2)],
        out_specs=[
            pl.BlockSpec(blk, lambda r, nn: (r, nn, 0)),
            pl.BlockSpec((1, rows, ATTN_HEAD_DIM), lambda r, nn: (r, nn, 0)),
        ],
        out_shape=[
            jax.ShapeDtypeStruct((nseq, length, ATTN_WIDTH), BF16),
            jax.ShapeDtypeStruct((nseq, length, ATTN_HEAD_DIM), F32),
        ],
        compiler_params=_params("parallel", "parallel"),
        name="dilated_attn",
    )(qkv, qkv, qkv, qkv, qkv)


def _attn_out_kernel(*refs):
    ng = len(ATTN_GROUPS)
    o_refs, s_refs = refs[:ng], refs[ng:2 * ng]
    w_ref, r_ref, out_ref, om_ref, su_ref = refs[2 * ng:]

    @pl.when(pl.program_id(2) == 0)
    def _():
        tm = om_ref.shape[0]
        tok = lax.broadcasted_iota(jnp.int32, (tm, tm), 0)
        src = lax.broadcasted_iota(jnp.int32, (tm, tm), 1)
        outs = []
        for g in range(ng):
            d = o_refs[g].shape[1]
            for r in range(d):
                su_ref[g, pl.ds(r, tm // d, stride=d), :] = s_refs[g][0, r]
            o = o_refs[g][0].reshape(tm, ATTN_WIDTH)
            if d > 1:
                to_token_order = (src == (tok % d) * (tm // d) + tok // d).astype(BF16)
                o = _dot(to_token_order, o)
            outs.append(o)
        lse = [su_ref[g] for g in range(ng)]
        m = functools.reduce(jnp.maximum, lse)
        e = [jnp.exp(l - m) for l in lse]
        inv = 1.0 / sum(e)
        wts = [e[g] * inv for g in range(1, ng)]
        for h in range(ATTN_HEADS):
            sl = slice(h * ATTN_HEAD_DIM, (h + 1) * ATTN_HEAD_DIM)
            base = outs[0][:, sl].astype(F32)
            om = base + sum(wt[:, h:h + 1] * (o[:, sl].astype(F32) - base) for wt, o in zip(wts, outs[1:]))
            om_ref[:, sl] = om.astype(BF16)

    out_ref[0] = r_ref[0] + _dot(om_ref[...], w_ref[...])


def attention_out(parts, w_out, residual, tm=512, tn=2048):
    b, s, n = residual.shape
    k = w_out.shape[0]
    tm, tn = _tile(s, tm), _tile(n, tn)
    ng = len(parts)

    def by_residue(arr):
        d, width = arr.shape[1], arr.shape[3]
        return pl.BlockSpec((1, d, tm // d, width), lambda bi, i, j: (bi, 0, i, 0))

    res = pl.BlockSpec((1, tm, tn), lambda bi, i, j: (bi, i, j))
    return pl.pallas_call(
        _attn_out_kernel,
        grid=(b, s // tm, n // tn),
        in_specs=[by_residue(o) for o, _ in parts] + [by_residue(st) for _, st in parts]
        + [pl.BlockSpec((k, tn), lambda bi, i, j: (0, j)), res],
        out_specs=res,
        out_shape=jax.ShapeDtypeStruct((b, s, n), F32),
        scratch_shapes=[pltpu.VMEM((tm, k), BF16),
                        pltpu.VMEM((ng, tm, ATTN_HEAD_DIM), F32)],
        compiler_params=_params("parallel", "parallel", "arbitrary"),
        name="attn_merge_out",
    )(*[o for o, _ in parts], *[st for _, st in parts], w_out, residual)


def _by_residue(t, dilation):
    b, s = t.shape[:2]
    return jnp.swapaxes(t.reshape(b, s // dilation, dilation, *t.shape[2:]), 1, 2)


def _qkv_weight_kernel(w_ref, o_ref):
    w = w_ref[...].astype(BF16)
    kind = pl.program_id(0) % 3

    @pl.when(kind == 2)
    def _():
        o_ref[...] = w

    @pl.when(kind != 2)
    def _():
        half = ROPE_DIM // 2
        mid = ATTN_HEAD_DIM // 2
        src = lax.broadcasted_iota(jnp.int32, (ATTN_HEAD_DIM, ATTN_HEAD_DIM), 0)
        dst = lax.broadcasted_iota(jnp.int32, (ATTN_HEAD_DIM, ATTN_HEAD_DIM), 1)
        from_lane = jnp.where(dst < half, dst,
                              jnp.where(dst < mid, dst + half, jnp.where(dst < mid + half, dst - mid + half, dst)))
        reorder = (src == from_lane).astype(BF16)
        for h in range(ATTN_HEADS):
            sl = slice(h * ATTN_HEAD_DIM, (h + 1) * ATTN_HEAD_DIM)
            o_ref[:, sl] = _dot(w[:, sl], reorder).astype(BF16)


def _rope_layout(w_in):
    d, n = w_in.shape
    return pl.pallas_call(
        _qkv_weight_kernel,
        grid=(n // ATTN_WIDTH,),
        in_specs=[pl.BlockSpec((d, ATTN_WIDTH), lambda j: (0, j))],
        out_specs=pl.BlockSpec((d, ATTN_WIDTH), lambda j: (0, j)),
        out_shape=jax.ShapeDtypeStruct((d, n), BF16),
        compiler_params=_params("parallel"),
        name="qkv_weight_layout",
    )(w_in)


def _rope_tables(positions):
    half = ROPE_DIM // 2
    inv_freq = ROPE_THETA ** (-jnp.arange(half, dtype=F32) * 2.0 / ROPE_DIM)
    lane = jnp.arange(ATTN_HEAD_DIM)
    rotary = lane % (ATTN_HEAD_DIM // 2) < half
    freq = jnp.where(rotary, inv_freq[lane % half], 0.0)
    sign = jnp.where(lane < half, -1.0, 1.0).astype(F32)
    tables = []
    for _, dil in ATTN_GROUPS:
        ang = _by_residue(positions, dil).astype(F32).reshape(-1, 1) * freq
        tables.append((jnp.cos(ang), sign * jnp.sin(ang)))
    return tables


def attention_layer(x, b, s, rope, gain, w_in, w_out):
    t, d = x.shape
    w_in = _rope_layout(w_in)
    dils = [dil for _, dil in ATTN_GROUPS]
    hns = norm_by_residue(x.reshape(b, s, d), gain, dils)
    parts = []
    for gi, (dil, hn, (cos_t, sin_t)) in enumerate(zip(dils, hns, rope)):
        qkv = qkv_projection(hn.reshape(t, d), w_in, gi, cos_t, sin_t)
        o, st = attention_group(qkv.reshape(b * dil, s // dil, -1))
        parts.append((o.reshape(b, dil, s // dil, -1), st.reshape(b, dil, s // dil, -1)))
    return attention_out(parts, w_out.astype(BF16), x.reshape(b, s, d)).reshape(t, d)


def _gelu_tanh(y):
    return 0.5 * y * (1.0 + jnp.tanh(math.sqrt(2.0 / math.pi) * (y + 0.044715 * (y * y * y))))


def _s5_kernel(u_ref, perm_ref, toep_ref, wp_ref, q_ref, c1_ref, c2_ref, z_ref):
    lc = S5_CHUNK
    groups = toep_ref.shape[0]
    nch = u_ref.shape[1] // lc
    width = lc * S5_GROUP
    nsteps = c1_ref.shape[1]
    by_pos = jnp.concatenate([u_ref[0, pl.ds(s, nch, stride=lc), :].astype(BF16) for s in range(lc)], axis=1)
    regroup = lambda g: _dot(by_pos, perm_ref[:, g * width:(g + 1) * width]).astype(BF16)
    row = lax.broadcasted_iota(jnp.int32, (nch, 2 * S5_STATE), 0)
    zs = []
    out = None
    half = groups // 2
    u_next = regroup(0)
    for g in range(groups):
        if g == half:
            out = _nt(jnp.concatenate(zs, axis=1), perm_ref[:, :half * width])
        u, u_next = u_next, (regroup(g + 1) if g + 1 < groups else None)
        y = _dot(u, toep_ref[g])
        x = _dot(u, wp_ref[g])
        for i in range(nsteps):
            m = 1 << i
            sh = jnp.where(row >= m, pltpu.roll(x, m, 0), 0.0)
            x = x + c1_ref[g, i:i + 1, :] * sh + c2_ref[g, i:i + 1, :] * pltpu.roll(sh, S5_STATE, 1)
        xprev = jnp.where(row >= 1, pltpu.roll(x, 1, 0), 0.0)
        y = y + _dot(xprev.astype(BF16), q_ref[g])
        zs.append(_gelu_tanh(y).astype(BF16))
    out = out + _nt(jnp.concatenate(zs[half:], axis=1), perm_ref[:, half * width:])
    for s in range(lc):
        z_ref[0, pl.ds(s, nch, stride=lc), :] = out[:, s * LANES:(s + 1) * LANES]


def _s5_operators(a_re, a_im, log_dt, b_re, b_im, c_re, c_im, d_skip, chunks_per_seq):
    lc = S5_CHUNK
    g, p = a_re.shape
    dt = jnp.exp(log_dt)[:, None]
    mag = jnp.exp(dt * a_re)
    ab_re = mag * jnp.cos(dt * a_im)
    ab_im = mag * jnp.sin(dt * a_im)
    inv = 1.0 / (a_re * a_re + a_im * a_im)
    f_re = ((ab_re - 1.0) * a_re + ab_im * a_im) * inv
    f_im = (ab_im * a_re - (ab_re - 1.0) * a_im) * inv
    bb_re = f_re[..., None] * b_re - f_im[..., None] * b_im
    bb_im = f_re[..., None] * b_im + f_im[..., None] * b_re

    def power(j):
        jf = j.astype(F32)[:, None, None]
        m = jnp.exp(jf * (dt * a_re))
        return m * jnp.cos(jf * (dt * a_im)), m * jnp.sin(jf * (dt * a_im))

    pr, pi = power(jnp.arange(lc + 1))
    ba_re = pr[..., None] * bb_re - pi[..., None] * bb_im
    ba_im = pr[..., None] * bb_im + pi[..., None] * bb_re
    kern = (jnp.einsum("jgpa,gcp->jgac", ba_re, c_re) - jnp.einsum("jgpa,gcp->jgac", ba_im, c_im))
    ti = jnp.arange(lc)
    lag = ti[None, :] - ti[:, None]
    toep = jnp.where((lag >= 0)[None, :, None, :, None],
                     jnp.transpose(kern[jnp.clip(lag, 0, lc)], (2, 0, 3, 1, 4)), 0.0)
    eye = jnp.eye(lc, dtype=F32)[:, None, :, None] * jnp.eye(S5_GROUP, dtype=F32)[None, :, None, :]
    toep = toep + d_skip.reshape(g, 1, S5_GROUP, 1, 1) * eye[None]
    toep = toep.reshape(g, lc * S5_GROUP, lc * S5_GROUP)
    rev = lc - 1 - ti
    wp = jnp.concatenate([ba_re[rev], ba_im[rev]], axis=2)
    wp = jnp.transpose(wp, (1, 0, 3, 2)).reshape(g, lc * S5_GROUP, 2 * p)
    qr, qi = pr[1:], pi[1:]
    q_top = (jnp.einsum("tgp,gcp->gptc", qr, c_re) - jnp.einsum("tgp,gcp->gptc", qi, c_im))
    q_bot = (-jnp.einsum("tgp,gcp->gptc", qi, c_re) - jnp.einsum("tgp,gcp->gptc", qr, c_im))
    q = jnp.concatenate([q_top, q_bot], axis=1).reshape(g, 2 * p, lc * S5_GROUP)
    nsteps = chunks_per_seq.bit_length() - 1
    sr, si = power(lc * (2 ** jnp.arange(nsteps)))
    c1 = jnp.transpose(jnp.concatenate([sr, sr], axis=2), (1, 0, 2))
    c2 = jnp.transpose(jnp.concatenate([-si, si], axis=2), (1, 0, 2))
    return toep.astype(BF16), wp.astype(BF16), q.astype(BF16), c1, c2


def s5_core(u, ops):
    b, s, d = u.shape
    lc = S5_CHUNK
    cps = s // lc
    assert cps & (cps - 1) == 0 and d % LANES == 0
    gs = LANES // S5_GROUP
    width = lc * S5_GROUP
    toep, wp, q, c1, c2 = ops
    nsteps = c1.shape[1]
    i = jnp.arange(lc * LANES)
    dest = (i % LANES // S5_GROUP) * width + (i // LANES) * S5_GROUP + i % S5_GROUP
    perm = (dest[:, None] == i[None, :]).astype(BF16)
    per_tile = lambda shape: pl.BlockSpec((gs,) + shape, lambda ti, bi: (ti, 0, 0))
    act = pl.BlockSpec((1, s, LANES), lambda ti, bi: (bi, 0, ti))
    return pl.pallas_call(
        _s5_kernel,
        grid=(d // LANES, b),
        in_specs=[
            act,
            pl.BlockSpec((lc * LANES, lc * LANES), lambda ti, bi: (0, 0), pipeline_mode=pl.Buffered(1)),
            per_tile((width, width)),
            per_tile((width, 2 * S5_STATE)),
            per_tile((2 * S5_STATE, width)),
            per_tile((nsteps, 2 * S5_STATE)),
            per_tile((nsteps, 2 * S5_STATE)),
        ],
        out_specs=act,
        out_shape=jax.ShapeDtypeStruct((b, s, d), F32),
        compiler_params=_params("parallel", "parallel"),
        name="s5_chunk_scan",
    )(u, perm, toep, wp, q, c1, c2)


def _glu_kernel(z_ref, wv_ref, wg_ref, r_ref, o_ref):
    z = z_ref[...].astype(BF16)
    val = _dot(z, wv_ref[...])
    gate = _dot(z, wg_ref[...])
    o_ref[...] = r_ref[...] + val * jax.nn.sigmoid(gate)


def glu_out(z, w_glu, residual, tm=1024, tn=1024):
    t, k = z.shape
    n = w_glu.shape[1] // 2
    tm, tn = _tile(t, tm), _tile(n, tn)
    nn = n // tn
    return pl.pallas_call(
        _glu_kernel,
        grid=(t // tm, nn),
        in_specs=[
            pl.BlockSpec((tm, k), lambda i, j: (i, 0)),
            pl.BlockSpec((k, tn), lambda i, j: (0, j)),
            pl.BlockSpec((k, tn), lambda i, j: (0, j + nn)),
            pl.BlockSpec((tm, tn), lambda i, j: (i, j)),
        ],
        out_specs=pl.BlockSpec((tm, tn), lambda i, j: (i, j)),
        out_shape=jax.ShapeDtypeStruct((t, n), F32),
        compiler_params=_params("parallel", "parallel"),
        name="glu_out",
    )(z, w_glu, w_glu, residual)


def s5_layer(x, b, s, gain, a_re, a_im, log_dt, b_re, b_im, c_re, c_im, d_skip, w_glu):
    u = rmsnorm(x, gain)
    ops = _s5_operators(a_re, a_im, log_dt, b_re, b_im, c_re, c_im, d_skip, s // S5_CHUNK)
    z = s5_core(u.reshape(b, s, -1), ops)
    return glu_out(z.reshape(x.shape), w_glu.astype(BF16), x)


def _split2(x):
    hi = x.astype(BF16)
    lo = (x - hi.astype(F32)).astype(BF16)
    return hi, lo


def _rwkv_kernel(r_ref, k_ref, v_ref, hw_ref, ha_ref, hg_ref, ww2_ref, aw2_ref, gw2_ref, w0_ref, a0_ref,
                 kk_ref, ka_ref, rk_ref, lnw_ref, lnb_ref, o_ref, state_ref):
    nb, c, width = r_ref.shape
    unit = RWKV_UNIT
    nu = width // unit
    reps = unit // c

    @pl.when(pl.program_id(1) == 0)
    def _():
        state_ref[...] = jnp.zeros_like(state_ref)

    ri = lax.broadcasted_iota(jnp.int32, (unit, unit), 0)
    ci = lax.broadcasted_iota(jnp.int32, (unit, unit), 1)
    head_bd = (ri // RWKV_HEAD) == (ci // RWKV_HEAD)
    stack_bd = (ri // c) == (ci // RWKV_HEAD)
    chunk_bd = (ri // c) == (ci // c)
    ones_bd = head_bd.astype(BF16)
    tr = lax.broadcasted_iota(jnp.int32, (c, unit), 0)
    tc = lax.broadcasted_iota(jnp.int32, (c, unit), 1) % c
    eye_cat = (tc == tr).astype(F32)
    strict = tc < tr
    incl = tc <= tr
    li = lax.broadcasted_iota(jnp.int32, (c, c), 0)
    lj = lax.broadcasted_iota(jnp.int32, (c, c), 1)
    tri = (lj <= li).astype(BF16)

    def headsums(*cols):
        pieces = [p for xs in zip(*cols) for x in xs for p in _split2(x)]
        parts = _dot(jnp.concatenate(pieces, axis=0), ones_bd)
        sums = [parts[2 * i * c:(2 * i + 1) * c] + parts[(2 * i + 1) * c:(2 * i + 2) * c]
                for i in range(len(pieces) // 2)]
        return [sums[i::len(cols)] for i in range(len(cols))]

    def stack(x, mask):
        return jnp.where(mask, jnp.concatenate([x] * reps, axis=0), 0.0).astype(BF16)

    units = [(b, slice(q * unit, (q + 1) * unit), q) for b in range(nb) for q in range(nu)]
    each = lambda fn, *cols: [fn(*vals) for vals in zip(*cols)]

    def load(ref):
        return [ref[b, :, sl] for b, sl, _ in units]

    def row(ref):
        return [ref[:, sl] for _, sl, _ in units]

    r, k, v = (load(ref) for ref in (r_ref, k_ref, v_ref))
    k_k, k_a, r_k, ln_w, ln_b = (row(ref) for ref in (kk_ref, ka_ref, rk_ref, lnw_ref, lnb_ref))

    def low_rank(h_ref, w2_ref):
        wide = [_dot(h_ref[b], w2_ref[...]) for b in range(nb)]
        return [wide[b][:, sl] for b, sl, _ in units]

    wz = each(lambda z, bias: z + bias, low_rank(hw_ref, ww2_ref), row(w0_ref))
    az = each(lambda z, bias: z + bias, low_rank(ha_ref, aw2_ref), row(a0_ref))
    gate = low_rank(hg_ref, gw2_ref)

    logw = each(lambda z: -jnp.exp(-jax.nn.softplus(-z) - 0.5), wz)
    a = each(jax.nn.sigmoid, az)
    kk0 = each(lambda x, y: x * y, k, k_k)
    k2 = each(lambda x, al, ka: x * (1.0 + (al - 1.0) * ka), k, a, k_a)
    ssq, rk_sum = headsums(each(lambda x: x * x, kk0), each(lambda rr, kv, rk: rr * kv * rk, r, k2, r_k))
    kk = each(lambda x, s: x * lax.rsqrt(jnp.maximum(s, 1e-24)), kk0, ssq)
    bonus = each(lambda s, vv: s * vv, rk_sum, v)
    bvec = each(lambda x, al: x * al, kk, a)

    def cumsum(lw):
        p0 = lw.astype(BF16)
        r1 = lw - p0.astype(F32)
        p1 = r1.astype(BF16)
        p2 = (r1 - p1.astype(F32)).astype(BF16)
        return _dot(tri, p0) + _dot(tri, p1) + _dot(tri, p2)

    cum = each(cumsum, logw)
    cend = each(lambda x: x[c - 1:c, :], cum)
    a_t = each(lambda x, cu, lw: (-x * jnp.exp(cu - lw)).astype(BF16), kk, cum, logw)
    r_t = each(lambda x, cu: (x * jnp.exp(cu)).astype(BF16), r, cum)
    w_inv = each(lambda cu: jnp.exp(-cu), cum)
    w_rest = each(lambda ce, cu: jnp.exp(ce - cu), cend, cum)
    bs = each(lambda x, w: stack(x * w, stack_bd), bvec, w_inv)
    ks = each(lambda x, w: stack(x * w, stack_bd), k2, w_inv)
    vs = each(lambda x: stack(x, stack_bd), v)

    scores = each(lambda at, rt, b_, k_: _nt(jnp.concatenate([at, rt], axis=0), jnp.concatenate([b_, k_], axis=0)),
                  a_t, r_t, bs, ks)
    s_k = each(lambda s: jnp.concatenate([jnp.where(strict, s[:c, unit:], 0.0), jnp.where(incl, s[c:, unit:], 0.0)],
                                         axis=0).astype(BF16), scores)
    s_rb = each(lambda s: jnp.where(incl, s[c:, :unit], 0.0).astype(BF16), scores)

    pw = each(lambda s: jnp.where(strict, s[:c, :unit], 0.0), scores)
    t_cat = each(lambda p: eye_cat + p, pw)
    pw = each(lambda p: _dot(p.astype(BF16), stack(p, chunk_bd)), pw)
    for _ in range(c.bit_length() - 3):
        both = each(lambda t, p: _dot(jnp.concatenate([t, p], axis=0).astype(BF16), stack(p, chunk_bd)), t_cat, pw)
        t_cat = each(lambda t, r_: t + r_[:c], t_cat, both)
        pw = each(lambda r_: r_[c:], both)
    t_cat = each(lambda t, p: t + _dot(t.astype(BF16), stack(p, chunk_bd)), t_cat, pw)

    state = [state_ref[b, q] for b, _, q in units]
    state_b = each(lambda s: s.astype(BF16), state)
    base = each(lambda at, rt, sb, sk, vv: _nt(jnp.concatenate([at, rt], axis=0), sb) + _dot(sk, vv),
                a_t, r_t, state_b, s_k, vs)
    u = each(lambda t, bb: _dot(t.astype(BF16), stack(bb[:c], stack_bd)), t_cat, base)
    y = each(lambda bb, srb, uu: bb[c:] + _dot(srb, stack(uu, stack_bd)), base, s_rb, u)
    upd = each(lambda uu, vv, bv, kv, wr: _tn(jnp.concatenate([uu, vv], axis=0).astype(BF16),
                                               jnp.concatenate([bv * wr, kv * wr], axis=0).astype(BF16)),
               u, v, bvec, k2, w_rest)
    for (b, _, q), s, ce, up in zip(units, state, cend, upd):
        state_ref[b, q] = jnp.where(head_bd, s * jnp.exp(ce) + up, 0.0)

    inv_n = 1.0 / RWKV_HEAD
    (y_sum,) = headsums(y)
    yc = each(lambda yy, m: yy - m * inv_n, y, y_sum)
    (sq_sum,) = headsums(each(lambda z: z * z, yc))
    var = each(lambda s: s * inv_n, sq_sum)
    for (b, sl, _), z, vr, lw, lb, bo, gt in zip(units, yc, var, ln_w, ln_b, bonus, gate):
        o_ref[b, :, sl] = ((z * lax.rsqrt(vr + RWKV_GN_EPS) * lw + lb + bo) * gt).astype(o_ref.dtype)


def rwkv_core(r, k, v, low, low_w2, w0, a0, k_k, k_a, r_k, ln_w, ln_b, units_per_step=4):
    b, s, d = r.shape
    c = RWKV_CHUNK
    assert s % c == 0 and RWKV_UNIT % c == 0
    width = _tile(d, RWKV_UNIT * units_per_step)
    nu = width // RWKV_UNIT
    act = pl.BlockSpec((b, c, width), lambda i, j: (0, j, i))
    row = pl.BlockSpec((1, width), lambda i, j: (0, i))
    rows = [t.reshape(1, d).astype(F32) for t in (w0, a0, k_k, k_a, r_k, ln_w, ln_b)]
    return pl.pallas_call(
        _rwkv_kernel,
        grid=(d // width, s // c),
        in_specs=[act] * 3
        + [pl.BlockSpec((b, c, h.shape[2]), lambda i, j: (0, j, 0)) for h in low]
        + [pl.BlockSpec((w.shape[0], width), lambda i, j: (0, i)) for w in low_w2]
        + [row] * len(rows),
        out_specs=act,
        out_shape=jax.ShapeDtypeStruct((b, s, d), BF16),
        scratch_shapes=[pltpu.VMEM((b, nu, RWKV_UNIT, RWKV_UNIT), F32)],
        compiler_params=_params("parallel", "arbitrary"),
        name="rwkv7_chunk",
    )(r, k, v, *low, *low_w2, *rows)


def _rwkv_mix_kernel(x_ref, xp_ref, g_ref, mu_ref, ww1_ref, aw1_ref, gw1_ref,
                     xr_ref, xk_ref, xv_ref, hw_ref, ha_ref, hg_ref):
    gain = g_ref[...]
    hn = _rms(x_ref[0], gain)
    tail = xp_ref.shape[1]
    prev_last = _rms(xp_ref[0, tail - 1:tail, :], gain)
    prev_last = jnp.where(pl.program_id(1) > 0, prev_last, 0.0)
    row = lax.broadcasted_iota(jnp.int32, hn.shape, 0)
    xx = jnp.where(row == 0, prev_last, pltpu.roll(hn, 1, 0)) - hn
    mix = lambda i: (hn + xx * mu_ref[i:i + 1, :]).astype(BF16)
    xr_ref[0] = mix(0)
    xk_ref[0] = mix(2)
    xv_ref[0] = mix(3)
    hw_ref[0] = jnp.tanh(_dot(mix(1), ww1_ref[...])).astype(BF16)
    ha_ref[0] = _dot(mix(4), aw1_ref[...]).astype(BF16)
    hg_ref[0] = jax.nn.sigmoid(_dot(mix(5), gw1_ref[...])).astype(BF16)


def rwkv_mix(x, gain, mu, w_w1, a_w1, g_w1, tm=512, tail=8):
    b, s, d = x.shape
    tm = _tile(s, tm)
    tok = lambda width: pl.BlockSpec((1, tm, width), lambda bi, i: (bi, i, 0))
    whole = lambda arr: pl.BlockSpec(arr.shape, lambda bi, i: (0, 0))
    ranks = [w.shape[1] for w in (w_w1, a_w1, g_w1)]
    mu = mu.astype(F32)
    return pl.pallas_call(
        _rwkv_mix_kernel,
        grid=(b, s // tm),
        in_specs=[tok(d),
                  pl.BlockSpec((1, tail, d), lambda bi, i: (bi, jnp.maximum(i * (tm // tail) - 1, 0), 0)),
                  pl.BlockSpec((1, d), lambda bi, i: (0, 0)),
                  whole(mu), whole(w_w1), whole(a_w1), whole(g_w1)],
        out_specs=[tok(d)] * 3 + [tok(rk) for rk in ranks],
        out_shape=[jax.ShapeDtypeStruct((b, s, d), BF16)] * 3
        + [jax.ShapeDtypeStruct((b, s, rk), BF16) for rk in ranks],
        compiler_params=_params("parallel", "parallel"),
        name="rwkv_mix",
    )(x, x, gain.reshape(1, d), mu, w_w1, a_w1, g_w1)


def _pad_cols(w, mult=128):
    n = w.shape[1]
    return jnp.pad(w, ((0, 0), (0, (-n) % mult)))


def _pad_rows(w, mult=128):
    n = w.shape[0]
    return jnp.pad(w, ((0, (-n) % mult), (0, 0)))


def rwkv_layer(x, b, s, gain, mu, w_r, w_k, w_v, w0, w_w1, w_w2, a0, a_w1, a_w2, g_w1, g_w2,
               k_k, k_a, r_k, ln_w, ln_b, w_o):
    t, d = x.shape
    bf = lambda w: w.astype(BF16)
    xr, xk, xv, *low = rwkv_mix(x.reshape(b, s, d), gain, mu,
                                bf(_pad_cols(w_w1)), bf(_pad_cols(a_w1)), bf(_pad_cols(g_w1)))
    r, k, v = (matmul(m.reshape(t, d), bf(w)).reshape(b, s, d) for m, w in ((xr, w_r), (xk, w_k), (xv, w_v)))
    low_w2 = [bf(_pad_rows(w)) for w in (w_w2, a_w2, g_w2)]
    y = rwkv_core(r, k, v, low, low_w2, w0, a0, k_k, k_a, r_k, ln_w, ln_b)
    return matmul(y.reshape(t, d), bf(w_o), residual=x)


def kernel(x, positions, l0_norm_mix, l0_attn_w_in, l0_attn_w_out, l0_norm_ffn, l0_ffn_w_in, l0_ffn_w_out, l1_norm_mix, l1_s5_a_re, l1_s5_a_im, l1_s5_log_dt, l1_s5_b_re, l1_s5_b_im, l1_s5_c_re, l1_s5_c_im, l1_s5_d, l1_s5_w_glu, l1_norm_ffn, l1_ffn_w_in, l1_ffn_w_out, l2_norm_mix, l2_rwkv_mu, l2_rwkv_w_r, l2_rwkv_w_k, l2_rwkv_w_v, l2_rwkv_w0, l2_rwkv_w_w1, l2_rwkv_w_w2, l2_rwkv_a0, l2_rwkv_a_w1, l2_rwkv_a_w2, l2_rwkv_g_w1, l2_rwkv_g_w2, l2_rwkv_k_k, l2_rwkv_k_a, l2_rwkv_r_k, l2_rwkv_ln_w, l2_rwkv_ln_b, l2_rwkv_w_o, l2_norm_ffn, l2_ffn_w_in, l2_ffn_w_out, l3_norm_mix, l3_attn_w_in, l3_attn_w_out, l3_norm_ffn, l3_ffn_w_in, l3_ffn_w_out, final_norm):
    b, s, d = x.shape
    h = x.reshape(b * s, d)
    rope = _rope_tables(positions)

    def channel_mixer(h, gain, w_in, w_out, final_gain=None):
        return ffn(h, gain, w_in.astype(BF16), w_out.astype(BF16), final_gain)

    h = attention_layer(h, b, s, rope, l0_norm_mix, l0_attn_w_in, l0_attn_w_out)
    h = channel_mixer(h, l0_norm_ffn, l0_ffn_w_in, l0_ffn_w_out)
    h = s5_layer(h, b, s, l1_norm_mix, l1_s5_a_re, l1_s5_a_im, l1_s5_log_dt, l1_s5_b_re, l1_s5_b_im,
                 l1_s5_c_re, l1_s5_c_im, l1_s5_d, l1_s5_w_glu)
    h = channel_mixer(h, l1_norm_ffn, l1_ffn_w_in, l1_ffn_w_out)
    h = rwkv_layer(h, b, s, l2_norm_mix, l2_rwkv_mu, l2_rwkv_w_r, l2_rwkv_w_k, l2_rwkv_w_v, l2_rwkv_w0,
                   l2_rwkv_w_w1, l2_rwkv_w_w2, l2_rwkv_a0, l2_rwkv_a_w1, l2_rwkv_a_w2, l2_rwkv_g_w1,
                   l2_rwkv_g_w2, l2_rwkv_k_k, l2_rwkv_k_a, l2_rwkv_r_k, l2_rwkv_ln_w, l2_rwkv_ln_b,
                   l2_rwkv_w_o)
    h = channel_mixer(h, l2_norm_ffn, l2_ffn_w_in, l2_ffn_w_out)
    h = attention_layer(h, b, s, rope, l3_norm_mix, l3_attn_w_in, l3_attn_w_out)
    h = channel_mixer(h, l3_norm_ffn, l3_ffn_w_in, l3_ffn_w_out, final_gain=final_norm)
    return h.reshape(b, s, d)
```

```python
import functools
import math

import jax
import jax.numpy as jnp
from jax import lax
from jax.experimental import pallas as pl
from jax.experimental.pallas import tpu as pltpu

F32 = jnp.float32
BF16 = jnp.bfloat16

RMS_EPS = 1e-6
LANES = 128
SUBLANES = 8
MXU_TILE = 256
VMEM_LIMIT_BYTES = 56 * 1024 * 1024

ATTN_HEAD_DIM = 128
ATTN_HEADS = 8
ATTN_GROUPS = ((128, 1), (512, 4), (2048, 16))
ATTN_BLOCK = 128
ATTN_BLOCKS_PER_STEP = 4
ROPE_THETA = 500000.0
ROPE_DIM = ATTN_HEAD_DIM // 4
ATTN_WIDTH = ATTN_HEADS * ATTN_HEAD_DIM
QKV_COL_CHUNK = MXU_TILE
NEG_BIG = -1e30

S5_GROUP = 16
S5_STATE = 64
S5_CHUNK = MXU_TILE // S5_GROUP

RWKV_HEAD = 64
RWKV_CHUNK = 64
RWKV_UNIT = MXU_TILE
RWKV_GN_EPS = 64e-5


def _params(*sem):
    return pltpu.CompilerParams(dimension_semantics=sem, vmem_limit_bytes=VMEM_LIMIT_BYTES)


def _tile(n, pref):
    t = min(n, pref)
    assert n % t == 0, (n, pref)
    return t


def _rms(x, g):
    return x * lax.rsqrt(jnp.mean(x * x, axis=-1, keepdims=True) + RMS_EPS) * g


def _nt(a, b):
    return lax.dot_general(a, b, (((1,), (1,)), ((), ())), preferred_element_type=F32)


def _tn(a, b):
    return lax.dot_general(a, b, (((0,), (0,)), ((), ())), preferred_element_type=F32)


def _dot(a, b):
    return jnp.dot(a, b, preferred_element_type=F32)


def _rmsnorm_kernel(x_ref, g_ref, o_ref):
    o_ref[...] = _rms(x_ref[...], g_ref[...]).astype(o_ref.dtype)


def rmsnorm(x, gain, tm=512):
    t, d = x.shape
    tm = _tile(t, tm)
    return pl.pallas_call(
        _rmsnorm_kernel,
        grid=(t // tm,),
        in_specs=[pl.BlockSpec((tm, d), lambda i: (i, 0)), pl.BlockSpec((1, d), lambda i: (0, 0))],
        out_specs=pl.BlockSpec((tm, d), lambda i: (i, 0)),
        out_shape=jax.ShapeDtypeStruct((t, d), F32),
        compiler_params=_params("parallel"),
        name="rmsnorm",
    )(x, gain.reshape(1, d))


def _mm_kernel(x_ref, w_ref, *refs):
    *r_ref, o_ref = refs
    acc = _dot(x_ref[...], w_ref[...])
    if r_ref:
        acc = acc + r_ref[0][...]
    o_ref[...] = acc.astype(o_ref.dtype)


def matmul(x, w, *, residual=None, tm=1024, tn=1024):
    t, k = x.shape
    n = w.shape[1]
    tm, tn = _tile(t, tm), _tile(n, tn)
    ins = [x, w]
    specs = [pl.BlockSpec((tm, k), lambda i, j: (i, 0)), pl.BlockSpec((k, tn), lambda i, j: (0, j))]
    if residual is not None:
        ins.append(residual)
        specs.append(pl.BlockSpec((tm, tn), lambda i, j: (i, j)))
    return pl.pallas_call(
        _mm_kernel,
        grid=(t // tm, n // tn),
        in_specs=specs,
        out_specs=pl.BlockSpec((tm, tn), lambda i, j: (i, j)),
        out_shape=jax.ShapeDtypeStruct((t, n), F32),
        compiler_params=_params("parallel", "parallel"),
        name="matmul",
    )(*ins)


def _ffn_kernel(x_ref, g_ref, wg_ref, wu_ref, wo_ref, fg_ref, o_ref, xn_ref, *, final_norm):
    j = pl.program_id(1)

    @pl.when(j == 0)
    def _():
        x = x_ref[...]
        xn_ref[...] = _rms(x, g_ref[...]).astype(BF16)
        o_ref[...] = x

    xn = xn_ref[...]
    gate = _dot(xn, wg_ref[...])
    up = _dot(xn, wu_ref[...])
    h = (gate * jax.nn.sigmoid(gate) * up).astype(BF16)
    o_ref[...] += _dot(h, wo_ref[...])

    if final_norm:
        @pl.when(j == pl.num_programs(1) - 1)
        def _():
            o_ref[...] = _rms(o_ref[...], fg_ref[...])


def ffn(x, gain, w_in, w_out, final_gain=None, tm=512, tf=512):
    t, d = x.shape
    f = w_out.shape[0]
    tm, tf = _tile(t, tm), _tile(f, tf)
    nf = f // tf
    fg = gain if final_gain is None else final_gain
    return pl.pallas_call(
        functools.partial(_ffn_kernel, final_norm=final_gain is not None),
        grid=(t // tm, nf),
        in_specs=[
            pl.BlockSpec((tm, d), lambda i, j: (i, 0)),
            pl.BlockSpec((1, d), lambda i, j: (0, 0)),
            pl.BlockSpec((d, tf), lambda i, j: (0, j)),
            pl.BlockSpec((d, tf), lambda i, j: (0, j + nf)),
            pl.BlockSpec((tf, d), lambda i, j: (j, 0)),
            pl.BlockSpec((1, d), lambda i, j: (0, 0)),
        ],
        out_specs=pl.BlockSpec((tm, d), lambda i, j: (i, 0)),
        out_shape=jax.ShapeDtypeStruct((t, d), F32),
        scratch_shapes=[pltpu.VMEM((tm, d), BF16)],
        compiler_params=_params("parallel", "arbitrary"),
        name="ffn",
    )(x, gain.reshape(1, d), w_in, w_in, w_out, fg.reshape(1, d))


def _norm_residues_kernel(x_ref, g_ref, *refs):
    out_refs, hn_ref = refs[:-1], refs[-1]
    hn = _rms(x_ref[0], g_ref[...])
    ncol, tm, lanes = hn_ref.shape
    for c in range(ncol):
        hn_ref[c] = hn[:, c * lanes:(c + 1) * lanes]
    for o_ref in out_refs:
        d = o_ref.shape[1]
        if d == 1:
            o_ref[0, 0] = hn.astype(o_ref.dtype)
            continue
        for r in range(d):
            for c in range(ncol):
                o_ref[0, r, :, c * lanes:(c + 1) * lanes] = (
                    hn_ref[c, pl.ds(r, tm // d, stride=d), :].astype(o_ref.dtype))


def norm_by_residue(x, gain, dilations, tm=512):
    b, s, d_model = x.shape
    tm = _tile(s, tm)
    return pl.pallas_call(
        _norm_residues_kernel,
        grid=(b, s // tm),
        in_specs=[pl.BlockSpec((1, tm, d_model), lambda bi, i: (bi, i, 0)),
                  pl.BlockSpec((1, d_model), lambda bi, i: (0, 0))],
        out_specs=[pl.BlockSpec((1, d, tm // d, d_model), lambda bi, i: (bi, 0, i, 0)) for d in dilations],
        out_shape=[jax.ShapeDtypeStruct((b, d, s // d, d_model), BF16) for d in dilations],
        scratch_shapes=[pltpu.VMEM((d_model // LANES, tm, LANES), F32)],
        compiler_params=_params("parallel", "parallel"),
        name="norm_by_residue",
    )(x, gain.reshape(1, d_model))


def _qkv_kernel(x_ref, w_ref, cos_ref, sin_ref, o_ref):
    kind = pl.program_id(1)
    scale = jnp.where(kind == 0, ATTN_HEAD_DIM ** -0.5, 1.0).astype(F32)
    is_v = kind == 2
    cos = jnp.where(is_v, 1.0, cos_ref[...] * scale)
    sin = jnp.where(is_v, 0.0, sin_ref[...] * scale)
    x = x_ref[...]
    for c0 in range(0, o_ref.shape[1], QKV_COL_CHUNK):
        acc = _dot(x, w_ref[:, c0:c0 + QKV_COL_CHUNK])
        for h0 in range(0, QKV_COL_CHUNK, ATTN_HEAD_DIM):
            xh = acc[:, h0:h0 + ATTN_HEAD_DIM]
            rot = pltpu.roll(xh, ATTN_HEAD_DIM // 2, 1)
            o_ref[:, c0 + h0:c0 + h0 + ATTN_HEAD_DIM] = (xh * cos + rot * sin).astype(o_ref.dtype)


def qkv_projection(x, w_in, gi, cos_t, sin_t, tm=1024):
    t, d = x.shape
    tm = _tile(t, tm)
    tn = ATTN_WIDTH
    return pl.pallas_call(
        _qkv_kernel,
        grid=(t // tm, 3),
        in_specs=[
            pl.BlockSpec((tm, d), lambda i, j: (i, 0)),
            pl.BlockSpec((d, tn), lambda i, j: (0, gi * 3 + j)),
            pl.BlockSpec((tm, ATTN_HEAD_DIM), lambda i, j: (i, 0)),
            pl.BlockSpec((tm, ATTN_HEAD_DIM), lambda i, j: (i, 0)),
        ],
        out_specs=pl.BlockSpec((tm, tn), lambda i, j: (i, j)),
        out_shape=jax.ShapeDtypeStruct((t, 3 * tn), BF16),
        compiler_params=_params("parallel", "parallel"),
        name="qkv_rope",
    )(x, w_in, cos_t, sin_t)


def _attn_kernel(q_ref, kp_ref, kc_ref, vp_ref, vc_ref, o_ref, st_ref):
    n = pl.program_id(1)
    blk = ATTN_BLOCK
    qi = lax.broadcasted_iota(jnp.int32, (blk, 2 * blk), 0)
    kj = lax.broadcasted_iota(jnp.int32, (blk, 2 * blk), 1)
    dist = qi + blk - kj
    band = (dist >= 0) & (dist <= blk)
    first = band & ((kj >= blk) | (n > 0))
    lane = lax.broadcasted_iota(jnp.int32, (blk, ATTN_HEAD_DIM), 1)
    stats = [jnp.zeros((blk, ATTN_HEAD_DIM), F32) for _ in range(ATTN_BLOCKS_PER_STEP)]
    for h in range(ATTN_HEADS):
        sl = slice(h * ATTN_HEAD_DIM, (h + 1) * ATTN_HEAD_DIM)
        k = jnp.concatenate([kp_ref[0, :, sl], kc_ref[0, :, sl]], axis=0)
        v = jnp.concatenate([vp_ref[0, :, sl], vc_ref[0, :, sl]], axis=0)
        for j in range(ATTN_BLOCKS_PER_STEP):
            rows = slice(j * blk, (j + 1) * blk)
            keys = slice(j * blk, (j + 2) * blk)
            s = jnp.where(first if j == 0 else band, _nt(q_ref[0, rows, sl], k[keys]), NEG_BIG)
            mx = jnp.max(s, axis=-1, keepdims=True)
            p = jnp.exp(s - mx)
            den = jnp.sum(p, axis=-1, keepdims=True)
            num = _dot(p.astype(BF16), v[keys])
            o_ref[0, rows, sl] = (num / den).astype(o_ref.dtype)
            stats[j] = jnp.where(lane == h, mx + jnp.log(den), stats[j])
    for j in range(ATTN_BLOCKS_PER_STEP):
        st_ref[0, j * blk:(j + 1) * blk, :] = stats[j]


def attention_group(qkv):
    nseq, length, _ = qkv.shape
    per = ATTN_BLOCKS_PER_STEP
    rows = per * ATTN_BLOCK
    assert length % rows == 0
    nb = length // rows
    blk = (1, rows, ATTN_WIDTH)

    def cur(off):
        return pl.BlockSpec(blk, lambda r, nn: (r, nn, off))

    def prev(off):
        return pl.BlockSpec((1, ATTN_BLOCK, ATTN_WIDTH), lambda r, nn: (r, jnp.maximum(nn * per - 1, 0), off))

    return pl.pallas_call(
        _attn_kernel,
        grid=(nseq, nb),
        in_specs=[cur(0), prev(1), cur(1), prev(2), cur(2)],
        out_specs=[
            pl.BlockSpec(blk, lambda r, nn: (r, nn, 0)),
            pl.BlockSpec((1, rows, ATTN_HEAD_DIM), lambda r, nn: (r, nn, 0)),
        ],
        out_shape=[
            jax.ShapeDtypeStruct((nseq, length, ATTN_WIDTH), BF16),
            jax.ShapeDtypeStruct((nseq, length, ATTN_HEAD_DIM), F32),
        ],
        compiler_params=_params("parallel", "parallel"),
        name="dilated_attn",
    )(qkv, qkv, qkv, qkv, qkv)


def _attn_out_kernel(*refs):
    ng = len(ATTN_GROUPS)
    o_refs, s_refs = refs[:ng], refs[ng:2 * ng]
    w_ref, r_ref, out_ref, om_ref, su_ref = refs[2 * ng:]

    @pl.when(pl.program_id(2) == 0)
    def _():
        tm = om_ref.shape[0]
        tok = lax.broadcasted_iota(jnp.int32, (tm, tm), 0)
        src = lax.broadcasted_iota(jnp.int32, (tm, tm), 1)
        outs = []
        for g in range(ng):
            d = o_refs[g].shape[1]
            for r in range(d):
                su_ref[g, pl.ds(r, tm // d, stride=d), :] = s_refs[g][0, r]
            o = o_refs[g][0].reshape(tm, ATTN_WIDTH)
            if d > 1:
                to_token_order = (src == (tok % d) * (tm // d) + tok // d).astype(BF16)
                o = _dot(to_token_order, o)
            outs.append(o)
        lse = [su_ref[g] for g in range(ng)]
        m = functools.reduce(jnp.maximum, lse)
        e = [jnp.exp(l - m) for l in lse]
        inv = 1.0 / sum(e)
        wts = [e[g] * inv for g in range(1, ng)]
        for h in range(ATTN_HEADS):
            sl = slice(h * ATTN_HEAD_DIM, (h + 1) * ATTN_HEAD_DIM)
            base = outs[0][:, sl].astype(F32)
            om = base + sum(wt[:, h:h + 1] * (o[:, sl].astype(F32) - base) for wt, o in zip(wts, outs[1:]))
            om_ref[:, sl] = om.astype(BF16)

    out_ref[0] = r_ref[0] + _dot(om_ref[...], w_ref[...])


def attention_out(parts, w_out, residual, tm=512, tn=2048):
    b, s, n = residual.shape
    k = w_out.shape[0]
    tm, tn = _tile(s, tm), _tile(n, tn)
    ng = len(parts)

    def by_residue(arr):
        d, width = arr.shape[1], arr.shape[3]
        return pl.BlockSpec((1, d, tm // d, width), lambda bi, i, j: (bi, 0, i, 0))

    res = pl.BlockSpec((1, tm, tn), lambda bi, i, j: (bi, i, j))
    return pl.pallas_call(
        _attn_out_kernel,
        grid=(b, s // tm, n // tn),
        in_specs=[by_residue(o) for o, _ in parts] + [by_residue(st) for _, st in parts]
        + [pl.BlockSpec((k, tn), lambda bi, i, j: (0, j)), res],
        out_specs=res,
        out_shape=jax.ShapeDtypeStruct((b, s, n), F32),
        scratch_shapes=[pltpu.VMEM((tm, k), BF16),
                        pltpu.VMEM((ng, tm, ATTN_HEAD_DIM), F32)],
        compiler_params=_params("parallel", "parallel", "arbitrary"),
        name="attn_merge_out",
    )(*[o for o, _ in parts], *[st for _, st in parts], w_out, residual)


def _by_residue(t, dilation):
    b, s = t.shape[:2]
    return jnp.swapaxes(t.reshape(b, s // dilation, dilation, *t.shape[2:]), 1, 2)


def _qkv_weight_kernel(w_ref, o_ref):
    w = w_ref[...].astype(BF16)
    kind = pl.program_id(0) % 3

    @pl.when(kind == 2)
    def _():
        o_ref[...] = w

    @pl.when(kind != 2)
    def _():
        half = ROPE_DIM // 2
        mid = ATTN_HEAD_DIM // 2
        src = lax.broadcasted_iota(jnp.int32, (ATTN_HEAD_DIM, ATTN_HEAD_DIM), 0)
        dst = lax.broadcasted_iota(jnp.int32, (ATTN_HEAD_DIM, ATTN_HEAD_DIM), 1)
        from_lane = jnp.where(dst < half, dst,
                              jnp.where(dst < mid, dst + half, jnp.where(dst < mid + half, dst - mid + half, dst)))
        reorder = (src == from_lane).astype(BF16)
        for h in range(ATTN_HEADS):
            sl = slice(h * ATTN_HEAD_DIM, (h + 1) * ATTN_HEAD_DIM)
            o_ref[:, sl] = _dot(w[:, sl], reorder).astype(BF16)


def _rope_layout(w_in):
    d, n = w_in.shape
    return pl.pallas_call(
        _qkv_weight_kernel,
        grid=(n // ATTN_WIDTH,),
        in_specs=[pl.BlockSpec((d, ATTN_WIDTH), lambda j: (0, j))],
        out_specs=pl.BlockSpec((d, ATTN_WIDTH), lambda j: (0, j)),
        out_shape=jax.ShapeDtypeStruct((d, n), BF16),
        compiler_params=_params("parallel"),
        name="qkv_weight_layout",
    )(w_in)


def _rope_tables(positions):
    half = ROPE_DIM // 2
    inv_freq = ROPE_THETA ** (-jnp.arange(half, dtype=F32) * 2.0 / ROPE_DIM)
    lane = jnp.arange(ATTN_HEAD_DIM)
    rotary = lane % (ATTN_HEAD_DIM // 2) < half
    freq = jnp.where(rotary, inv_freq[lane % half], 0.0)
    sign = jnp.where(lane < half, -1.0, 1.0).astype(F32)
    tables = []
    for _, dil in ATTN_GROUPS:
        ang = _by_residue(positions, dil).astype(F32).reshape(-1, 1) * freq
        tables.append((jnp.cos(ang), sign * jnp.sin(ang)))
    return tables


def attention_layer(x, b, s, rope, gain, w_in, w_out):
    t, d = x.shape
    w_in = _rope_layout(w_in)
    dils = [dil for _, dil in ATTN_GROUPS]
    hns = norm_by_residue(x.reshape(b, s, d), gain, dils)
    parts = []
    for gi, (dil, hn, (cos_t, sin_t)) in enumerate(zip(dils, hns, rope)):
        qkv = qkv_projection(hn.reshape(t, d), w_in, gi, cos_t, sin_t)
        o, st = attention_group(qkv.reshape(b * dil, s // dil, -1))
        parts.append((o.reshape(b, dil, s // dil, -1), st.reshape(b, dil, s // dil, -1)))
    return attention_out(parts, w_out.astype(BF16), x.reshape(b, s, d)).reshape(t, d)


def _gelu_tanh(y):
    return 0.5 * y * (1.0 + jnp.tanh(math.sqrt(2.0 / math.pi) * (y + 0.044715 * (y * y * y))))


def _s5_kernel(u_ref, perm_ref, toep_ref, wp_ref, q_ref, c1_ref, c2_ref, z_ref):
    lc = S5_CHUNK
    groups = toep_ref.shape[0]
    nch = u_ref.shape[1] // lc
    width = lc * S5_GROUP
    nsteps = c1_ref.shape[1]
    by_pos = jnp.concatenate([u_ref[0, pl.ds(s, nch, stride=lc), :].astype(BF16) for s in range(lc)], axis=1)
    regroup = lambda g: _dot(by_pos, perm_ref[:, g * width:(g + 1) * width]).astype(BF16)
    row = lax.broadcasted_iota(jnp.int32, (nch, 2 * S5_STATE), 0)
    zs = []
    u_next = regroup(0)
    for g in range(groups):
        u, u_next = u_next, (regroup(g + 1) if g + 1 < groups else None)
        y = _dot(u, toep_ref[g])
        x = _dot(u, wp_ref[g])
        for i in range(nsteps):
            m = 1 << i
            sh = jnp.where(row >= m, pltpu.roll(x, m, 0), 0.0)
            x = x + c1_ref[g, i:i + 1, :] * sh + c2_ref[g, i:i + 1, :] * pltpu.roll(sh, S5_STATE, 1)
        xprev = jnp.where(row >= 1, pltpu.roll(x, 1, 0), 0.0)
        y = y + _dot(xprev.astype(BF16), q_ref[g])
        zs.append(_gelu_tanh(y).astype(BF16))
    out = _nt(jnp.concatenate(zs, axis=1), perm_ref[...])
    for s in range(lc):
        z_ref[0, pl.ds(s, nch, stride=lc), :] = out[:, s * LANES:(s + 1) * LANES]


def _s5_operators(a_re, a_im, log_dt, b_re, b_im, c_re, c_im, d_skip, chunks_per_seq):
    lc = S5_CHUNK
    g, p = a_re.shape
    dt = jnp.exp(log_dt)[:, None]
    mag = jnp.exp(dt * a_re)
    ab_re = mag * jnp.cos(dt * a_im)
    ab_im = mag * jnp.sin(dt * a_im)
    inv = 1.0 / (a_re * a_re + a_im * a_im)
    f_re = ((ab_re - 1.0) * a_re + ab_im * a_im) * inv
    f_im = (ab_im * a_re - (ab_re - 1.0) * a_im) * inv
    bb_re = f_re[..., None] * b_re - f_im[..., None] * b_im
    bb_im = f_re[..., None] * b_im + f_im[..., None] * b_re

    def power(j):
        jf = j.astype(F32)[:, None, None]
        m = jnp.exp(jf * (dt * a_re))
        return m * jnp.cos(jf * (dt * a_im)), m * jnp.sin(jf * (dt * a_im))

    pr, pi = power(jnp.arange(lc + 1))
    ba_re = pr[..., None] * bb_re - pi[..., None] * bb_im
    ba_im = pr[..., None] * bb_im + pi[..., None] * bb_re
    kern = (jnp.einsum("jgpa,gcp->jgac", ba_re, c_re) - jnp.einsum("jgpa,gcp->jgac", ba_im, c_im))
    ti = jnp.arange(lc)
    lag = ti[None, :] - ti[:, None]
    toep = jnp.where((lag >= 0)[None, :, None, :, None],
                     jnp.transpose(kern[jnp.clip(lag, 0, lc)], (2, 0, 3, 1, 4)), 0.0)
    eye = jnp.eye(lc, dtype=F32)[:, None, :, None] * jnp.eye(S5_GROUP, dtype=F32)[None, :, None, :]
    toep = toep + d_skip.reshape(g, 1, S5_GROUP, 1, 1) * eye[None]
    toep = toep.reshape(g, lc * S5_GROUP, lc * S5_GROUP)
    rev = lc - 1 - ti
    wp = jnp.concatenate([ba_re[rev], ba_im[rev]], axis=2)
    wp = jnp.transpose(wp, (1, 0, 3, 2)).reshape(g, lc * S5_GROUP, 2 * p)
    qr, qi = pr[1:], pi[1:]
    q_top = (jnp.einsum("tgp,gcp->gptc", qr, c_re) - jnp.einsum("tgp,gcp->gptc", qi, c_im))
    q_bot = (-jnp.einsum("tgp,gcp->gptc", qi, c_re) - jnp.einsum("tgp,gcp->gptc", qr, c_im))
    q = jnp.concatenate([q_top, q_bot], axis=1).reshape(g, 2 * p, lc * S5_GROUP)
    nsteps = chunks_per_seq.bit_length() - 1
    sr, si = power(lc * (2 ** jnp.arange(nsteps)))
    c1 = jnp.transpose(jnp.concatenate([sr, sr], axis=2), (1, 0, 2))
    c2 = jnp.transpose(jnp.concatenate([-si, si], axis=2), (1, 0, 2))
    return toep.astype(BF16), wp.astype(BF16), q.astype(BF16), c1, c2


def s5_core(u, ops):
    b, s, d = u.shape
    lc = S5_CHUNK
    cps = s // lc
    assert cps & (cps - 1) == 0 and d % LANES == 0
    gs = LANES // S5_GROUP
    width = lc * S5_GROUP
    toep, wp, q, c1, c2 = ops
    nsteps = c1.shape[1]
    i = jnp.arange(lc * LANES)
    dest = (i % LANES // S5_GROUP) * width + (i // LANES) * S5_GROUP + i % S5_GROUP
    perm = (dest[:, None] == i[None, :]).astype(BF16)
    per_tile = lambda shape: pl.BlockSpec((gs,) + shape, lambda ti, bi: (ti, 0, 0))
    act = pl.BlockSpec((1, s, LANES), lambda ti, bi: (bi, 0, ti))
    return pl.pallas_call(
        _s5_kernel,
        grid=(d // LANES, b),
        in_specs=[
            act,
            pl.BlockSpec((lc * LANES, lc * LANES), lambda ti, bi: (0, 0), pipeline_mode=pl.Buffered(1)),
            per_tile((width, width)),
            per_tile((width, 2 * S5_STATE)),
            per_tile((2 * S5_STATE, width)),
            per_tile((nsteps, 2 * S5_STATE)),
            per_tile((nsteps, 2 * S5_STATE)),
        ],
        out_specs=act,
        out_shape=jax.ShapeDtypeStruct((b, s, d), F32),
        compiler_params=_params("parallel", "parallel"),
        name="s5_chunk_scan",
    )(u, perm, toep, wp, q, c1, c2)


def _glu_kernel(z_ref, wv_ref, wg_ref, r_ref, o_ref):
    z = z_ref[...].astype(BF16)
    val = _dot(z, wv_ref[...])
    gate = _dot(z, wg_ref[...])
    o_ref[...] = r_ref[...] + val * jax.nn.sigmoid(gate)


def glu_out(z, w_glu, residual, tm=1024, tn=1024):
    t, k = z.shape
    n = w_glu.shape[1] // 2
    tm, tn = _tile(t, tm), _tile(n, tn)
    nn = n // tn
    return pl.pallas_call(
        _glu_kernel,
        grid=(t // tm, nn),
        in_specs=[
            pl.BlockSpec((tm, k), lambda i, j: (i, 0)),
            pl.BlockSpec((k, tn), lambda i, j: (0, j)),
            pl.BlockSpec((k, tn), lambda i, j: (0, j + nn)),
            pl.BlockSpec((tm, tn), lambda i, j: (i, j)),
        ],
        out_specs=pl.BlockSpec((tm, tn), lambda i, j: (i, j)),
        out_shape=jax.ShapeDtypeStruct((t, n), F32),
        compiler_params=_params("parallel", "parallel"),
        name="glu_out",
    )(z, w_glu, w_glu, residual)


def s5_layer(x, b, s, gain, a_re, a_im, log_dt, b_re, b_im, c_re, c_im, d_skip, w_glu):
    u = rmsnorm(x, gain)
    ops = _s5_operators(a_re, a_im, log_dt, b_re, b_im, c_re, c_im, d_skip, s // S5_CHUNK)
    z = s5_core(u.reshape(b, s, -1), ops)
    return glu_out(z.reshape(x.shape), w_glu.astype(BF16), x)


def _split2(x):
    hi = x.astype(BF16)
    lo = (x - hi.astype(F32)).astype(BF16)
    return hi, lo


def _rwkv_kernel(r_ref, k_ref, v_ref, hw_ref, ha_ref, hg_ref, ww2_ref, aw2_ref, gw2_ref, w0_ref, a0_ref,
                 kk_ref, ka_ref, rk_ref, lnw_ref, lnb_ref, o_ref, state_ref):
    nb, c, width = r_ref.shape
    unit = RWKV_UNIT
    nu = width // unit
    reps = unit // c

    @pl.when(pl.program_id(1) == 0)
    def _():
        state_ref[...] = jnp.zeros_like(state_ref)

    ri = lax.broadcasted_iota(jnp.int32, (unit, unit), 0)
    ci = lax.broadcasted_iota(jnp.int32, (unit, unit), 1)
    head_bd = (ri // RWKV_HEAD) == (ci // RWKV_HEAD)
    stack_bd = (ri // c) == (ci // RWKV_HEAD)
    chunk_bd = (ri // c) == (ci // c)
    ones_bd = head_bd.astype(BF16)
    tr = lax.broadcasted_iota(jnp.int32, (c, unit), 0)
    tc = lax.broadcasted_iota(jnp.int32, (c, unit), 1) % c
    eye_cat = (tc == tr).astype(F32)
    strict = tc < tr
    incl = tc <= tr
    li = lax.broadcasted_iota(jnp.int32, (c, c), 0)
    lj = lax.broadcasted_iota(jnp.int32, (c, c), 1)
    tri = (lj <= li).astype(BF16)

    def headsums(*cols):
        pieces = [p for xs in zip(*cols) for x in xs for p in _split2(x)]
        parts = _dot(jnp.concatenate(pieces, axis=0), ones_bd)
        sums = [parts[2 * i * c:(2 * i + 1) * c] + parts[(2 * i + 1) * c:(2 * i + 2) * c]
                for i in range(len(pieces) // 2)]
        return [sums[i::len(cols)] for i in range(len(cols))]

    def stack(x, mask):
        return jnp.where(mask, jnp.concatenate([x] * reps, axis=0), 0.0).astype(BF16)

    units = [(b, slice(q * unit, (q + 1) * unit), q) for b in range(nb) for q in range(nu)]
    each = lambda fn, *cols: [fn(*vals) for vals in zip(*cols)]

    def load(ref):
        return [ref[b, :, sl] for b, sl, _ in units]

    def row(ref):
        return [ref[:, sl] for _, sl, _ in units]

    r, k, v = (load(ref) for ref in (r_ref, k_ref, v_ref))
    k_k, k_a, r_k, ln_w, ln_b = (row(ref) for ref in (kk_ref, ka_ref, rk_ref, lnw_ref, lnb_ref))

    def low_rank(h_ref, w2_ref):
        wide = [_dot(h_ref[b], w2_ref[...]) for b in range(nb)]
        return [wide[b][:, sl] for b, sl, _ in units]

    wz = each(lambda z, bias: z + bias, low_rank(hw_ref, ww2_ref), row(w0_ref))
    az = each(lambda z, bias: z + bias, low_rank(ha_ref, aw2_ref), row(a0_ref))
    gate = low_rank(hg_ref, gw2_ref)

    logw = each(lambda z: -jnp.exp(-jax.nn.softplus(-z) - 0.5), wz)
    a = each(jax.nn.sigmoid, az)
    kk0 = each(lambda x, y: x * y, k, k_k)
    k2 = each(lambda x, al, ka: x * (1.0 + (al - 1.0) * ka), k, a, k_a)
    ssq, rk_sum = headsums(each(lambda x: x * x, kk0), each(lambda rr, kv, rk: rr * kv * rk, r, k2, r_k))
    kk = each(lambda x, s: x * lax.rsqrt(jnp.maximum(s, 1e-24)), kk0, ssq)
    bonus = each(lambda s, vv: s * vv, rk_sum, v)
    bvec = each(lambda x, al: x * al, kk, a)

    def cumsum(lw):
        p0 = lw.astype(BF16)
        r1 = lw - p0.astype(F32)
        p1 = r1.astype(BF16)
        p2 = (r1 - p1.astype(F32)).astype(BF16)
        return _dot(tri, p0) + _dot(tri, p1) + _dot(tri, p2)

    cum = each(cumsum, logw)
    cend = each(lambda x: x[c - 1:c, :], cum)
    a_t = each(lambda x, cu, lw: (-x * jnp.exp(cu - lw)).astype(BF16), kk, cum, logw)
    r_t = each(lambda x, cu: (x * jnp.exp(cu)).astype(BF16), r, cum)
    w_inv = each(lambda cu: jnp.exp(-cu), cum)
    w_rest = each(lambda ce, cu: jnp.exp(ce - cu), cend, cum)
    bs = each(lambda x, w: stack(x * w, stack_bd), bvec, w_inv)
    ks = each(lambda x, w: stack(x * w, stack_bd), k2, w_inv)
    vs = each(lambda x: stack(x, stack_bd), v)

    scores = each(lambda at, rt, b_, k_: _nt(jnp.concatenate([at, rt], axis=0), jnp.concatenate([b_, k_], axis=0)),
                  a_t, r_t, bs, ks)
    s_k = each(lambda s: jnp.concatenate([jnp.where(strict, s[:c, unit:], 0.0), jnp.where(incl, s[c:, unit:], 0.0)],
                                         axis=0).astype(BF16), scores)
    s_rb = each(lambda s: jnp.where(incl, s[c:, :unit], 0.0).astype(BF16), scores)

    pw = each(lambda s: jnp.where(strict, s[:c, :unit], 0.0), scores)
    t_cat = each(lambda p: eye_cat + p, pw)
    pw = each(lambda p: _dot(p.astype(BF16), stack(p, chunk_bd)), pw)
    for _ in range(c.bit_length() - 3):
        both = each(lambda t, p: _dot(jnp.concatenate([t, p], axis=0).astype(BF16), stack(p, chunk_bd)), t_cat, pw)
        t_cat = each(lambda t, r_: t + r_[:c], t_cat, both)
        pw = each(lambda r_: r_[c:], both)
    t_cat = each(lambda t, p: t + _dot(t.astype(BF16), stack(p, chunk_bd)), t_cat, pw)

    state = [state_ref[b, q] for b, _, q in units]
    state_b = each(lambda s: s.astype(BF16), state)
    base = each(lambda at, rt, sb, sk, vv: _nt(jnp.concatenate([at, rt], axis=0), sb) + _dot(sk, vv),
                a_t, r_t, state_b, s_k, vs)
    u = each(lambda t, bb: _dot(t.astype(BF16), stack(bb[:c], stack_bd)), t_cat, base)
    y = each(lambda bb, srb, uu: bb[c:] + _dot(srb, stack(uu, stack_bd)), base, s_rb, u)
    upd = each(lambda uu, vv, bv, kv, wr: _tn(jnp.concatenate([uu, vv], axis=0).astype(BF16),
                                               jnp.concatenate([bv * wr, kv * wr], axis=0).astype(BF16)),
               u, v, bvec, k2, w_rest)
    for (b, _, q), s, ce, up in zip(units, state, cend, upd):
        state_ref[b, q] = jnp.where(head_bd, s * jnp.exp(ce) + up, 0.0)

    inv_n = 1.0 / RWKV_HEAD
    (y_sum,) = headsums(y)
    yc = each(lambda yy, m: yy - m * inv_n, y, y_sum)
    (sq_sum,) = headsums(each(lambda z: z * z, yc))
    var = each(lambda s: s * inv_n, sq_sum)
    for (b, sl, _), z, vr, lw, lb, bo, gt in zip(units, yc, var, ln_w, ln_b, bonus, gate):
        o_ref[b, :, sl] = ((z * lax.rsqrt(vr + RWKV_GN_EPS) * lw + lb + bo) * gt).astype(o_ref.dtype)


def rwkv_core(r, k, v, low, low_w2, w0, a0, k_k, k_a, r_k, ln_w, ln_b, units_per_step=8):
    b, s, d = r.shape
    c = RWKV_CHUNK
    assert s % c == 0 and RWKV_UNIT % c == 0
    width = _tile(d, RWKV_UNIT * units_per_step)
    nu = width // RWKV_UNIT
    act = pl.BlockSpec((b, c, width), lambda i, j: (0, j, i))
    row = pl.BlockSpec((1, width), lambda i, j: (0, i))
    rows = [t.reshape(1, d).astype(F32) for t in (w0, a0, k_k, k_a, r_k, ln_w, ln_b)]
    return pl.pallas_call(
        _rwkv_kernel,
        grid=(d // width, s // c),
        in_specs=[act] * 3
        + [pl.BlockSpec((b, c, h.shape[2]), lambda i, j: (0, j, 0)) for h in low]
        + [pl.BlockSpec((w.shape[0], width), lambda i, j: (0, i)) for w in low_w2]
        + [row] * len(rows),
        out_specs=act,
        out_shape=jax.ShapeDtypeStruct((b, s, d), BF16),
        scratch_shapes=[pltpu.VMEM((b, nu, RWKV_UNIT, RWKV_UNIT), F32)],
        compiler_params=_params("parallel", "arbitrary"),
        name="rwkv7_chunk",
    )(r, k, v, *low, *low_w2, *rows)


def _rwkv_mix_kernel(x_ref, xp_ref, g_ref, mu_ref, ww1_ref, aw1_ref, gw1_ref,
                     xr_ref, xk_ref, xv_ref, hw_ref, ha_ref, hg_ref):
    gain = g_ref[...]
    hn = _rms(x_ref[0], gain)
    tail = xp_ref.shape[1]
    prev_last = _rms(xp_ref[0, tail - 1:tail, :], gain)
    prev_last = jnp.where(pl.program_id(1) > 0, prev_last, 0.0)
    row = lax.broadcasted_iota(jnp.int32, hn.shape, 0)
    xx = jnp.where(row == 0, prev_last, pltpu.roll(hn, 1, 0)) - hn
    mix = lambda i: (hn + xx * mu_ref[i:i + 1, :]).astype(BF16)
    xr_ref[0] = mix(0)
    xk_ref[0] = mix(2)
    xv_ref[0] = mix(3)
    hw_ref[0] = jnp.tanh(_dot(mix(1), ww1_ref[...])).astype(BF16)
    ha_ref[0] = _dot(mix(4), aw1_ref[...]).astype(BF16)
    hg_ref[0] = jax.nn.sigmoid(_dot(mix(5), gw1_ref[...])).astype(BF16)


def rwkv_mix(x, gain, mu, w_w1, a_w1, g_w1, tm=512, tail=SUBLANES):
    b, s, d = x.shape
    tm = _tile(s, tm)
    tok = lambda width: pl.BlockSpec((1, tm, width), lambda bi, i: (bi, i, 0))
    whole = lambda arr: pl.BlockSpec(arr.shape, lambda bi, i: (0, 0))
    ranks = [w.shape[1] for w in (w_w1, a_w1, g_w1)]
    mu = mu.astype(F32)
    return pl.pallas_call(
        _rwkv_mix_kernel,
        grid=(b, s // tm),
        in_specs=[tok(d),
                  pl.BlockSpec((1, tail, d), lambda bi, i: (bi, jnp.maximum(i * (tm // tail) - 1, 0), 0)),
                  pl.BlockSpec((1, d), lambda bi, i: (0, 0)),
                  whole(mu), whole(w_w1), whole(a_w1), whole(g_w1)],
        out_specs=[tok(d)] * 3 + [tok(rk) for rk in ranks],
        out_shape=[jax.ShapeDtypeStruct((b, s, d), BF16)] * 3
        + [jax.ShapeDtypeStruct((b, s, rk), BF16) for rk in ranks],
        compiler_params=_params("parallel", "parallel"),
        name="rwkv_mix",
    )(x, x, gain.reshape(1, d), mu, w_w1, a_w1, g_w1)


def _pad_cols(w, mult=128):
    n = w.shape[1]
    return jnp.pad(w, ((0, 0), (0, (-n) % mult)))


def _pad_rows(w, mult=128):
    n = w.shape[0]
    return jnp.pad(w, ((0, (-n) % mult), (0, 0)))


def rwkv_layer(x, b, s, gain, mu, w_r, w_k, w_v, w0, w_w1, w_w2, a0, a_w1, a_w2, g_w1, g_w2,
               k_k, k_a, r_k, ln_w, ln_b, w_o):
    t, d = x.shape
    bf = lambda w: w.astype(BF16)
    xr, xk, xv, *low = rwkv_mix(x.reshape(b, s, d), gain, mu,
                                bf(_pad_cols(w_w1)), bf(_pad_cols(a_w1)), bf(_pad_cols(g_w1)))
    r, k, v = (matmul(m.reshape(t, d), bf(w)).reshape(b, s, d) for m, w in ((xr, w_r), (xk, w_k), (xv, w_v)))
    low_w2 = [bf(_pad_rows(w)) for w in (w_w2, a_w2, g_w2)]
    y = rwkv_core(r, k, v, low, low_w2, w0, a0, k_k, k_a, r_k, ln_w, ln_b)
    return matmul(y.reshape(t, d), bf(w_o), residual=x)


def kernel(x, positions, l0_norm_mix, l0_attn_w_in, l0_attn_w_out, l0_norm_ffn, l0_ffn_w_in, l0_ffn_w_out, l1_norm_mix, l1_s5_a_re, l1_s5_a_im, l1_s5_log_dt, l1_s5_b_re, l1_s5_b_im, l1_s5_c_re, l1_s5_c_im, l1_s5_d, l1_s5_w_glu, l1_norm_ffn, l1_ffn_w_in, l1_ffn_w_out, l2_norm_mix, l2_rwkv_mu, l2_rwkv_w_r, l2_rwkv_w_k, l2_rwkv_w_v, l2_rwkv_w0, l2_rwkv_w_w1, l2_rwkv_w_w2, l2_rwkv_a0, l2_rwkv_a_w1, l2_rwkv_a_w2, l2_rwkv_g_w1, l2_rwkv_g_w2, l2_rwkv_k_k, l2_rwkv_k_a, l2_rwkv_r_k, l2_rwkv_ln_w, l2_rwkv_ln_b, l2_rwkv_w_o, l2_norm_ffn, l2_ffn_w_in, l2_ffn_w_out, l3_norm_mix, l3_attn_w_in, l3_attn_w_out, l3_norm_ffn, l3_ffn_w_in, l3_ffn_w_out, final_norm):
    b, s, d = x.shape
    h = x.reshape(b * s, d)
    rope = _rope_tables(positions)

    def channel_mixer(h, gain, w_in, w_out, final_gain=None):
        return ffn(h, gain, w_in.astype(BF16), w_out.astype(BF16), final_gain)

    h = attention_layer(h, b, s, rope, l0_norm_mix, l0_attn_w_in, l0_attn_w_out)
    h = channel_mixer(h, l0_norm_ffn, l0_ffn_w_in, l0_ffn_w_out)
    h = s5_layer(h, b, s, l1_norm_mix, l1_s5_a_re, l1_s5_a_im, l1_s5_log_dt, l1_s5_b_re, l1_s5_b_im,
                 l1_s5_c_re, l1_s5_c_im, l1_s5_d, l1_s5_w_glu)
    h = channel_mixer(h, l1_norm_ffn, l1_ffn_w_in, l1_ffn_w_out)
    h = rwkv_layer(h, b, s, l2_norm_mix, l2_rwkv_mu, l2_rwkv_w_r, l2_rwkv_w_k, l2_rwkv_w_v, l2_rwkv_w0,
                   l2_rwkv_w_w1, l2_rwkv_w_w2, l2_rwkv_a0, l2_rwkv_a_w1, l2_rwkv_a_w2, l2_rwkv_g_w1,
                   l2_rwkv_g_w2, l2_rwkv_k_k, l2_rwkv_k_a, l2_rwkv_r_k, l2_rwkv_ln_w, l2_rwkv_ln_b,
                   l2_rwkv_w_o)
    h = channel_mixer(h, l2_norm_ffn, l2_ffn_w_in, l2_ffn_w_out)
    h = attention_layer(h, b, s, rope, l3_norm_mix, l3_attn_w_in, l3_attn_w_out)
    h = channel_mixer(h, l3_norm_ffn, l3_ffn_w_in, l3_ffn_w_out, final_gain=final_norm)
    return h.reshape(b, s, d)
```

```python
import functools
import math

import jax
import jax.numpy as jnp
from jax import lax
from jax.experimental import pallas as pl
from jax.experimental.pallas import tpu as pltpu

F32 = jnp.float32
BF16 = jnp.bfloat16

RMS_EPS = 1e-6
LANES = 128
SUBLANES = 8
MXU_TILE = 256
VMEM_LIMIT_BYTES = 56 * 1024 * 1024

ATTN_HEAD_DIM = 128
ATTN_HEADS = 8
ATTN_GROUPS = ((128, 1), (512, 4), (2048, 16))
ATTN_BLOCK = 128
ATTN_BLOCKS_PER_STEP = 4
ROPE_THETA = 500000.0
ROPE_DIM = ATTN_HEAD_DIM // 4
ATTN_WIDTH = ATTN_HEADS * ATTN_HEAD_DIM
QKV_COL_CHUNK = MXU_TILE
NEG_BIG = -1e30

S5_GROUP = 16
S5_STATE = 64
S5_CHUNK = MXU_TILE // S5_GROUP

RWKV_HEAD = 64
RWKV_CHUNK = 64
RWKV_UNIT = MXU_TILE
RWKV_GN_EPS = 64e-5


def _params(*sem):
    return pltpu.CompilerParams(dimension_semantics=sem, vmem_limit_bytes=VMEM_LIMIT_BYTES)


def _tile(n, pref):
    t = min(n, pref)
    assert n % t == 0, (n, pref)
    return t


def _rms(x, g):
    return x * lax.rsqrt(jnp.mean(x * x, axis=-1, keepdims=True) + RMS_EPS) * g


def _nt(a, b):
    return lax.dot_general(a, b, (((1,), (1,)), ((), ())), preferred_element_type=F32)


def _tn(a, b):
    return lax.dot_general(a, b, (((0,), (0,)), ((), ())), preferred_element_type=F32)


def _dot(a, b):
    return jnp.dot(a, b, preferred_element_type=F32)


def _rmsnorm_kernel(x_ref, g_ref, o_ref):
    o_ref[...] = _rms(x_ref[...], g_ref[...]).astype(o_ref.dtype)


def rmsnorm(x, gain, tm=512):
    t, d = x.shape
    tm = _tile(t, tm)
    return pl.pallas_call(
        _rmsnorm_kernel,
        grid=(t // tm,),
        in_specs=[pl.BlockSpec((tm, d), lambda i: (i, 0)), pl.BlockSpec((1, d), lambda i: (0, 0))],
        out_specs=pl.BlockSpec((tm, d), lambda i: (i, 0)),
        out_shape=jax.ShapeDtypeStruct((t, d), F32),
        compiler_params=_params("parallel"),
        name="rmsnorm",
    )(x, gain.reshape(1, d))


def _mm_kernel(x_ref, w_ref, *refs):
    *r_ref, o_ref = refs
    acc = _dot(x_ref[...], w_ref[...])
    if r_ref:
        acc = acc + r_ref[0][...]
    o_ref[...] = acc.astype(o_ref.dtype)


def matmul(x, w, *, residual=None, tm=1024, tn=1024):
    t, k = x.shape
    n = w.shape[1]
    tm, tn = _tile(t, tm), _tile(n, tn)
    ins = [x, w]
    specs = [pl.BlockSpec((tm, k), lambda i, j: (i, 0)), pl.BlockSpec((k, tn), lambda i, j: (0, j))]
    if residual is not None:
        ins.append(residual)
        specs.append(pl.BlockSpec((tm, tn), lambda i, j: (i, j)))
    return pl.pallas_call(
        _mm_kernel,
        grid=(t // tm, n // tn),
        in_specs=specs,
        out_specs=pl.BlockSpec((tm, tn), lambda i, j: (i, j)),
        out_shape=jax.ShapeDtypeStruct((t, n), F32),
        compiler_params=_params("parallel", "parallel"),
        name="matmul",
    )(*ins)


def _ffn_kernel(x_ref, g_ref, wg_ref, wu_ref, wo_ref, fg_ref, o_ref, xn_ref, *, final_norm):
    j = pl.program_id(1)

    @pl.when(j == 0)
    def _():
        x = x_ref[...]
        xn_ref[...] = _rms(x, g_ref[...]).astype(BF16)
        o_ref[...] = x

    xn = xn_ref[...]
    gate = _dot(xn, wg_ref[...])
    up = _dot(xn, wu_ref[...])
    h = (gate * jax.nn.sigmoid(gate) * up).astype(BF16)
    o_ref[...] += _dot(h, wo_ref[...])

    if final_norm:
        @pl.when(j == pl.num_programs(1) - 1)
        def _():
            o_ref[...] = _rms(o_ref[...], fg_ref[...])


def ffn(x, gain, w_in, w_out, final_gain=None, tm=1024, tf=256):
    t, d = x.shape
    f = w_out.shape[0]
    tm, tf = _tile(t, tm), _tile(f, tf)
    nf = f // tf
    fg = gain if final_gain is None else final_gain
    return pl.pallas_call(
        functools.partial(_ffn_kernel, final_norm=final_gain is not None),
        grid=(t // tm, nf),
        in_specs=[
            pl.BlockSpec((tm, d), lambda i, j: (i, 0)),
            pl.BlockSpec((1, d), lambda i, j: (0, 0)),
            pl.BlockSpec((d, tf), lambda i, j: (0, j)),
            pl.BlockSpec((d, tf), lambda i, j: (0, j + nf)),
            pl.BlockSpec((tf, d), lambda i, j: (j, 0)),
            pl.BlockSpec((1, d), lambda i, j: (0, 0)),
        ],
        out_specs=pl.BlockSpec((tm, d), lambda i, j: (i, 0)),
        out_shape=jax.ShapeDtypeStruct((t, d), F32),
        scratch_shapes=[pltpu.VMEM((tm, d), BF16)],
        compiler_params=_params("parallel", "arbitrary"),
        name="ffn",
    )(x, gain.reshape(1, d), w_in, w_in, w_out, fg.reshape(1, d))


def _norm_residues_kernel(x_ref, g_ref, *refs):
    out_refs, hn_ref = refs[:-1], refs[-1]
    hn = _rms(x_ref[0], g_ref[...])
    ncol, tm, lanes = hn_ref.shape
    for c in range(ncol):
        hn_ref[c] = hn[:, c * lanes:(c + 1) * lanes]
    for o_ref in out_refs:
        d = o_ref.shape[1]
        if d == 1:
            o_ref[0, 0] = hn.astype(o_ref.dtype)
            continue
        for r in range(d):
            for c in range(ncol):
                o_ref[0, r, :, c * lanes:(c + 1) * lanes] = (
                    hn_ref[c, pl.ds(r, tm // d, stride=d), :].astype(o_ref.dtype))


def norm_by_residue(x, gain, dilations, tm=512):
    b, s, d_model = x.shape
    tm = _tile(s, tm)
    return pl.pallas_call(
        _norm_residues_kernel,
        grid=(b, s // tm),
        in_specs=[pl.BlockSpec((1, tm, d_model), lambda bi, i: (bi, i, 0)),
                  pl.BlockSpec((1, d_model), lambda bi, i: (0, 0))],
        out_specs=[pl.BlockSpec((1, d, tm // d, d_model), lambda bi, i: (bi, 0, i, 0)) for d in dilations],
        out_shape=[jax.ShapeDtypeStruct((b, d, s // d, d_model), BF16) for d in dilations],
        scratch_shapes=[pltpu.VMEM((d_model // LANES, tm, LANES), F32)],
        compiler_params=_params("parallel", "parallel"),
        name="norm_by_residue",
    )(x, gain.reshape(1, d_model))


def _qkv_kernel(x_ref, w_ref, cos_ref, sin_ref, o_ref):
    kind = pl.program_id(1)
    scale = jnp.where(kind == 0, ATTN_HEAD_DIM ** -0.5, 1.0).astype(F32)
    is_v = kind == 2
    cos = jnp.where(is_v, 1.0, cos_ref[...] * scale)
    sin = jnp.where(is_v, 0.0, sin_ref[...] * scale)
    x = x_ref[...]
    for c0 in range(0, o_ref.shape[1], QKV_COL_CHUNK):
        acc = _dot(x, w_ref[:, c0:c0 + QKV_COL_CHUNK])
        for h0 in range(0, QKV_COL_CHUNK, ATTN_HEAD_DIM):
            xh = acc[:, h0:h0 + ATTN_HEAD_DIM]
            rot = pltpu.roll(xh, ATTN_HEAD_DIM // 2, 1)
            o_ref[:, c0 + h0:c0 + h0 + ATTN_HEAD_DIM] = (xh * cos + rot * sin).astype(o_ref.dtype)


def qkv_projection(x, w_in, gi, cos_t, sin_t, tm=1024):
    t, d = x.shape
    tm = _tile(t, tm)
    tn = ATTN_WIDTH
    return pl.pallas_call(
        _qkv_kernel,
        grid=(t // tm, 3),
        in_specs=[
            pl.BlockSpec((tm, d), lambda i, j: (i, 0)),
            pl.BlockSpec((d, tn), lambda i, j: (0, gi * 3 + j)),
            pl.BlockSpec((tm, ATTN_HEAD_DIM), lambda i, j: (i, 0)),
            pl.BlockSpec((tm, ATTN_HEAD_DIM), lambda i, j: (i, 0)),
        ],
        out_specs=pl.BlockSpec((tm, tn), lambda i, j: (i, j)),
        out_shape=jax.ShapeDtypeStruct((t, 3 * tn), BF16),
        compiler_params=_params("parallel", "parallel"),
        name="qkv_rope",
    )(x, w_in, cos_t, sin_t)


def _attn_kernel(q_ref, kp_ref, kc_ref, vp_ref, vc_ref, o_ref, st_ref):
    n = pl.program_id(1)
    blk = ATTN_BLOCK
    qi = lax.broadcasted_iota(jnp.int32, (blk, 2 * blk), 0)
    kj = lax.broadcasted_iota(jnp.int32, (blk, 2 * blk), 1)
    dist = qi + blk - kj
    band = (dist >= 0) & (dist <= blk)
    first = band & ((kj >= blk) | (n > 0))
    lane = lax.broadcasted_iota(jnp.int32, (blk, ATTN_HEAD_DIM), 1)
    stats = [jnp.zeros((blk, ATTN_HEAD_DIM), F32) for _ in range(ATTN_BLOCKS_PER_STEP)]
    for h in range(ATTN_HEADS):
        sl = slice(h * ATTN_HEAD_DIM, (h + 1) * ATTN_HEAD_DIM)
        k = jnp.concatenate([kp_ref[0, :, sl], kc_ref[0, :, sl]], axis=0)
        v = jnp.concatenate([vp_ref[0, :, sl], vc_ref[0, :, sl]], axis=0)
        for j in range(ATTN_BLOCKS_PER_STEP):
            rows = slice(j * blk, (j + 1) * blk)
            keys = slice(j * blk, (j + 2) * blk)
            s = jnp.where(first if j == 0 else band, _nt(q_ref[0, rows, sl], k[keys]), NEG_BIG)
            mx = jnp.max(s, axis=-1, keepdims=True)
            p = jnp.exp(s - mx)
            den = jnp.sum(p, axis=-1, keepdims=True)
            num = _dot(p.astype(BF16), v[keys])
            o_ref[0, rows, sl] = (num / den).astype(o_ref.dtype)
            stats[j] = jnp.where(lane == h, mx + jnp.log(den), stats[j])
    for j in range(ATTN_BLOCKS_PER_STEP):
        st_ref[0, j * blk:(j + 1) * blk, :] = stats[j]


def attention_group(qkv):
    nseq, length, _ = qkv.shape
    per = ATTN_BLOCKS_PER_STEP
    rows = per * ATTN_BLOCK
    assert length % rows == 0
    nb = length // rows
    blk = (1, rows, ATTN_WIDTH)

    def cur(off):
        return pl.BlockSpec(blk, lambda r, nn: (r, nn, off))

    def prev(off):
        return pl.BlockSpec((1, ATTN_BLOCK, ATTN_WIDTH), lambda r, nn: (r, jnp.maximum(nn * per - 1, 0), off))

    return pl.pallas_call(
        _attn_kernel,
        grid=(nseq, nb),
        in_specs=[cur(0), prev(1), cur(1), prev(2), cur(2)],
        out_specs=[
            pl.BlockSpec(blk, lambda r, nn: (r, nn, 0)),
            pl.BlockSpec((1, rows, ATTN_HEAD_DIM), lambda r, nn: (r, nn, 0)),
        ],
        out_shape=[
            jax.ShapeDtypeStruct((nseq, length, ATTN_WIDTH), BF16),
            jax.ShapeDtypeStruct((nseq, length, ATTN_HEAD_DIM), F32),
        ],
        compiler_params=_params("parallel", "parallel"),
        name="dilated_attn",
    )(qkv, qkv, qkv, qkv, qkv)


def _attn_out_kernel(*refs):
    ng = len(ATTN_GROUPS)
    o_refs, s_refs = refs[:ng], refs[ng:2 * ng]
    w_ref, r_ref, out_ref, om_ref, su_ref = refs[2 * ng:]

    @pl.when(pl.program_id(2) == 0)
    def _():
        tm = om_ref.shape[0]
        tok = lax.broadcasted_iota(jnp.int32, (tm, tm), 0)
        src = lax.broadcasted_iota(jnp.int32, (tm, tm), 1)
        outs = []
        for g in range(ng):
            d = o_refs[g].shape[1]
            for r in range(d):
                su_ref[g, pl.ds(r, tm // d, stride=d), :] = s_refs[g][0, r]
            o = o_refs[g][0].reshape(tm, ATTN_WIDTH)
            if d > 1:
                to_token_order = (src == (tok % d) * (tm // d) + tok // d).astype(BF16)
                o = _dot(to_token_order, o)
            outs.append(o)
        lse = [su_ref[g] for g in range(ng)]
        m = functools.reduce(jnp.maximum, lse)
        e = [jnp.exp(l - m) for l in lse]
        inv = 1.0 / sum(e)
        wts = [e[g] * inv for g in range(1, ng)]
        for h in range(ATTN_HEADS):
            sl = slice(h * ATTN_HEAD_DIM, (h + 1) * ATTN_HEAD_DIM)
            base = outs[0][:, sl].astype(F32)
            om = base + sum(wt[:, h:h + 1] * (o[:, sl].astype(F32) - base) for wt, o in zip(wts, outs[1:]))
            om_ref[:, sl] = om.astype(BF16)

    out_ref[0] = r_ref[0] + _dot(om_ref[...], w_ref[...])


def attention_out(parts, w_out, residual, tm=512, tn=2048):
    b, s, n = residual.shape
    k = w_out.shape[0]
    tm, tn = _tile(s, tm), _tile(n, tn)
    ng = len(parts)

    def by_residue(arr):
        d, width = arr.shape[1], arr.shape[3]
        return pl.BlockSpec((1, d, tm // d, width), lambda bi, i, j: (bi, 0, i, 0))

    res = pl.BlockSpec((1, tm, tn), lambda bi, i, j: (bi, i, j))
    return pl.pallas_call(
        _attn_out_kernel,
        grid=(b, s // tm, n // tn),
        in_specs=[by_residue(o) for o, _ in parts] + [by_residue(st) for _, st in parts]
        + [pl.BlockSpec((k, tn), lambda bi, i, j: (0, j)), res],
        out_specs=res,
        out_shape=jax.ShapeDtypeStruct((b, s, n), F32),
        scratch_shapes=[pltpu.VMEM((tm, k), BF16),
                        pltpu.VMEM((ng, tm, ATTN_HEAD_DIM), F32)],
        compiler_params=_params("parallel", "parallel", "arbitrary"),
        name="attn_merge_out",
    )(*[o for o, _ in parts], *[st for _, st in parts], w_out, residual)


def _by_residue(t, dilation):
    b, s = t.shape[:2]
    return jnp.swapaxes(t.reshape(b, s // dilation, dilation, *t.shape[2:]), 1, 2)


def _qkv_weight_kernel(w_ref, o_ref):
    w = w_ref[...].astype(BF16)
    kind = pl.program_id(0) % 3

    @pl.when(kind == 2)
    def _():
        o_ref[...] = w

    @pl.when(kind != 2)
    def _():
        half = ROPE_DIM // 2
        mid = ATTN_HEAD_DIM // 2
        src = lax.broadcasted_iota(jnp.int32, (ATTN_HEAD_DIM, ATTN_HEAD_DIM), 0)
        dst = lax.broadcasted_iota(jnp.int32, (ATTN_HEAD_DIM, ATTN_HEAD_DIM), 1)
        from_lane = jnp.where(dst < half, dst,
                              jnp.where(dst < mid, dst + half, jnp.where(dst < mid + half, dst - mid + half, dst)))
        reorder = (src == from_lane).astype(BF16)
        for h in range(ATTN_HEADS):
            sl = slice(h * ATTN_HEAD_DIM, (h + 1) * ATTN_HEAD_DIM)
            o_ref[:, sl] = _dot(w[:, sl], reorder).astype(BF16)


def _rope_layout(w_in):
    d, n = w_in.shape
    return pl.pallas_call(
        _qkv_weight_kernel,
        grid=(n // ATTN_WIDTH,),
        in_specs=[pl.BlockSpec((d, ATTN_WIDTH), lambda j: (0, j))],
        out_specs=pl.BlockSpec((d, ATTN_WIDTH), lambda j: (0, j)),
        out_shape=jax.ShapeDtypeStruct((d, n), BF16),
        compiler_params=_params("parallel"),
        name="qkv_weight_layout",
    )(w_in)


def _rope_tables(positions):
    half = ROPE_DIM // 2
    inv_freq = ROPE_THETA ** (-jnp.arange(half, dtype=F32) * 2.0 / ROPE_DIM)
    lane = jnp.arange(ATTN_HEAD_DIM)
    rotary = lane % (ATTN_HEAD_DIM // 2) < half
    freq = jnp.where(rotary, inv_freq[lane % half], 0.0)
    sign = jnp.where(lane < half, -1.0, 1.0).astype(F32)
    tables = []
    for _, dil in ATTN_GROUPS:
        ang = _by_residue(positions, dil).astype(F32).reshape(-1, 1) * freq
        tables.append((jnp.cos(ang), sign * jnp.sin(ang)))
    return tables


def attention_layer(x, b, s, rope, gain, w_in, w_out):
    t, d = x.shape
    w_in = _rope_layout(w_in)
    dils = [dil for _, dil in ATTN_GROUPS]
    hns = norm_by_residue(x.reshape(b, s, d), gain, dils)
    parts = []
    for gi, (dil, hn, (cos_t, sin_t)) in enumerate(zip(dils, hns, rope)):
        qkv = qkv_projection(hn.reshape(t, d), w_in, gi, cos_t, sin_t)
        o, st = attention_group(qkv.reshape(b * dil, s // dil, -1))
        parts.append((o.reshape(b, dil, s // dil, -1), st.reshape(b, dil, s // dil, -1)))
    return attention_out(parts, w_out.astype(BF16), x.reshape(b, s, d)).reshape(t, d)


def _gelu_tanh(y):
    return 0.5 * y * (1.0 + jnp.tanh(math.sqrt(2.0 / math.pi) * (y + 0.044715 * (y * y * y))))


def _s5_kernel(u_ref, perm_ref, toep_ref, wp_ref, q_ref, c1_ref, c2_ref, z_ref):
    lc = S5_CHUNK
    groups = toep_ref.shape[0]
    nch = u_ref.shape[1] // lc
    width = lc * S5_GROUP
    nsteps = c1_ref.shape[1]
    by_pos = jnp.concatenate([u_ref[0, pl.ds(s, nch, stride=lc), :].astype(BF16) for s in range(lc)], axis=1)
    regroup = lambda g: _dot(by_pos, perm_ref[:, g * width:(g + 1) * width]).astype(BF16)
    row = lax.broadcasted_iota(jnp.int32, (nch, 2 * S5_STATE), 0)
    zs = []
    u_next = regroup(0)
    for g in range(groups):
        u, u_next = u_next, (regroup(g + 1) if g + 1 < groups else None)
        y = _dot(u, toep_ref[g])
        x = _dot(u, wp_ref[g])
        for i in range(nsteps):
            m = 1 << i
            sh = jnp.where(row >= m, pltpu.roll(x, m, 0), 0.0)
            x = x + c1_ref[g, i:i + 1, :] * sh + c2_ref[g, i:i + 1, :] * pltpu.roll(sh, S5_STATE, 1)
        xprev = jnp.where(row >= 1, pltpu.roll(x, 1, 0), 0.0)
        y = y + _dot(xprev.astype(BF16), q_ref[g])
        zs.append(_gelu_tanh(y).astype(BF16))
    out = _nt(jnp.concatenate(zs, axis=1), perm_ref[...])
    for s in range(lc):
        z_ref[0, pl.ds(s, nch, stride=lc), :] = out[:, s * LANES:(s + 1) * LANES]


def _s5_operators(a_re, a_im, log_dt, b_re, b_im, c_re, c_im, d_skip, chunks_per_seq):
    lc = S5_CHUNK
    g, p = a_re.shape
    dt = jnp.exp(log_dt)[:, None]
    mag = jnp.exp(dt * a_re)
    ab_re = mag * jnp.cos(dt * a_im)
    ab_im = mag * jnp.sin(dt * a_im)
    inv = 1.0 / (a_re * a_re + a_im * a_im)
    f_re = ((ab_re - 1.0) * a_re + ab_im * a_im) * inv
    f_im = (ab_im * a_re - (ab_re - 1.0) * a_im) * inv
    bb_re = f_re[..., None] * b_re - f_im[..., None] * b_im
    bb_im = f_re[..., None] * b_im + f_im[..., None] * b_re

    def power(j):
        jf = j.astype(F32)[:, None, None]
        m = jnp.exp(jf * (dt * a_re))
        return m * jnp.cos(jf * (dt * a_im)), m * jnp.sin(jf * (dt * a_im))

    pr, pi = power(jnp.arange(lc + 1))
    ba_re = pr[..., None] * bb_re - pi[..., None] * bb_im
    ba_im = pr[..., None] * bb_im + pi[..., None] * bb_re
    kern = (jnp.einsum("jgpa,gcp->jgac", ba_re, c_re) - jnp.einsum("jgpa,gcp->jgac", ba_im, c_im))
    ti = jnp.arange(lc)
    lag = ti[None, :] - ti[:, None]
    toep = jnp.where((lag >= 0)[None, :, None, :, None],
                     jnp.transpose(kern[jnp.clip(lag, 0, lc)], (2, 0, 3, 1, 4)), 0.0)
    eye = jnp.eye(lc, dtype=F32)[:, None, :, None] * jnp.eye(S5_GROUP, dtype=F32)[None, :, None, :]
    toep = toep + d_skip.reshape(g, 1, S5_GROUP, 1, 1) * eye[None]
    toep = toep.reshape(g, lc * S5_GROUP, lc * S5_GROUP)
    rev = lc - 1 - ti
    wp = jnp.concatenate([ba_re[rev], ba_im[rev]], axis=2)
    wp = jnp.transpose(wp, (1, 0, 3, 2)).reshape(g, lc * S5_GROUP, 2 * p)
    qr, qi = pr[1:], pi[1:]
    q_top = (jnp.einsum("tgp,gcp->gptc", qr, c_re) - jnp.einsum("tgp,gcp->gptc", qi, c_im))
    q_bot = (-jnp.einsum("tgp,gcp->gptc", qi, c_re) - jnp.einsum("tgp,gcp->gptc", qr, c_im))
    q = jnp.concatenate([q_top, q_bot], axis=1).reshape(g, 2 * p, lc * S5_GROUP)
    nsteps = chunks_per_seq.bit_length() - 1
    sr, si = power(lc * (2 ** jnp.arange(nsteps)))
    c1 = jnp.transpose(jnp.concatenate([sr, sr], axis=2), (1, 0, 2))
    c2 = jnp.transpose(jnp.concatenate([-si, si], axis=2), (1, 0, 2))
    return toep.astype(BF16), wp.astype(BF16), q.astype(BF16), c1, c2


def s5_core(u, ops):
    b, s, d = u.shape
    lc = S5_CHUNK
    cps = s // lc
    assert cps & (cps - 1) == 0 and d % LANES == 0
    gs = LANES // S5_GROUP
    width = lc * S5_GROUP
    toep, wp, q, c1, c2 = ops
    nsteps = c1.shape[1]
    i = jnp.arange(lc * LANES)
    dest = (i % LANES // S5_GROUP) * width + (i // LANES) * S5_GROUP + i % S5_GROUP
    perm = (dest[:, None] == i[None, :]).astype(BF16)
    per_tile = lambda shape: pl.BlockSpec((gs,) + shape, lambda ti, bi: (ti, 0, 0))
    act = pl.BlockSpec((1, s, LANES), lambda ti, bi: (bi, 0, ti))
    return pl.pallas_call(
        _s5_kernel,
        grid=(d // LANES, b),
        in_specs=[
            act,
            pl.BlockSpec((lc * LANES, lc * LANES), lambda ti, bi: (0, 0), pipeline_mode=pl.Buffered(1)),
            per_tile((width, width)),
            per_tile((width, 2 * S5_STATE)),
            per_tile((2 * S5_STATE, width)),
            per_tile((nsteps, 2 * S5_STATE)),
            per_tile((nsteps, 2 * S5_STATE)),
        ],
        out_specs=act,
        out_shape=jax.ShapeDtypeStruct((b, s, d), F32),
        compiler_params=_params("parallel", "parallel"),
        name="s5_chunk_scan",
    )(u, perm, toep, wp, q, c1, c2)


def _glu_kernel(z_ref, wv_ref, wg_ref, r_ref, o_ref):
    z = z_ref[...].astype(BF16)
    val = _dot(z, wv_ref[...])
    gate = _dot(z, wg_ref[...])
    o_ref[...] = r_ref[...] + val * jax.nn.sigmoid(gate)


def glu_out(z, w_glu, residual, tm=1024, tn=1024):
    t, k = z.shape
    n = w_glu.shape[1] // 2
    tm, tn = _tile(t, tm), _tile(n, tn)
    nn = n // tn
    return pl.pallas_call(
        _glu_kernel,
        grid=(t // tm, nn),
        in_specs=[
            pl.BlockSpec((tm, k), lambda i, j: (i, 0)),
            pl.BlockSpec((k, tn), lambda i, j: (0, j)),
            pl.BlockSpec((k, tn), lambda i, j: (0, j + nn)),
            pl.BlockSpec((tm, tn), lambda i, j: (i, j)),
        ],
        out_specs=pl.BlockSpec((tm, tn), lambda i, j: (i, j)),
        out_shape=jax.ShapeDtypeStruct((t, n), F32),
        compiler_params=_params("parallel", "parallel"),
        name="glu_out",
    )(z, w_glu, w_glu, residual)


def s5_layer(x, b, s, gain, a_re, a_im, log_dt, b_re, b_im, c_re, c_im, d_skip, w_glu):
    u = rmsnorm(x, gain)
    ops = _s5_operators(a_re, a_im, log_dt, b_re, b_im, c_re, c_im, d_skip, s // S5_CHUNK)
    z = s5_core(u.reshape(b, s, -1), ops)
    return glu_out(z.reshape(x.shape), w_glu.astype(BF16), x)


def _split2(x):
    hi = x.astype(BF16)
    lo = (x - hi.astype(F32)).astype(BF16)
    return hi, lo


def _rwkv_kernel(r_ref, k_ref, v_ref, hw_ref, ha_ref, hg_ref, ww2_ref, aw2_ref, gw2_ref, w0_ref, a0_ref,
                 kk_ref, ka_ref, rk_ref, lnw_ref, lnb_ref, o_ref, state_ref):
    nb, c, width = r_ref.shape
    unit = RWKV_UNIT
    nu = width // unit
    reps = unit // c

    @pl.when(pl.program_id(1) == 0)
    def _():
        state_ref[...] = jnp.zeros_like(state_ref)

    ri = lax.broadcasted_iota(jnp.int32, (unit, unit), 0)
    ci = lax.broadcasted_iota(jnp.int32, (unit, unit), 1)
    head_bd = (ri // RWKV_HEAD) == (ci // RWKV_HEAD)
    stack_bd = (ri // c) == (ci // RWKV_HEAD)
    chunk_bd = (ri // c) == (ci // c)
    ones_bd = head_bd.astype(BF16)
    tr = lax.broadcasted_iota(jnp.int32, (c, unit), 0)
    tc = lax.broadcasted_iota(jnp.int32, (c, unit), 1) % c
    eye_cat = (tc == tr).astype(F32)
    strict = tc < tr
    incl = tc <= tr
    li = lax.broadcasted_iota(jnp.int32, (c, c), 0)
    lj = lax.broadcasted_iota(jnp.int32, (c, c), 1)
    tri = (lj <= li).astype(BF16)

    def headsums(*cols):
        pieces = [p for xs in zip(*cols) for x in xs for p in _split2(x)]
        parts = _dot(jnp.concatenate(pieces, axis=0), ones_bd)
        sums = [parts[2 * i * c:(2 * i + 1) * c] + parts[(2 * i + 1) * c:(2 * i + 2) * c]
                for i in range(len(pieces) // 2)]
        return [sums[i::len(cols)] for i in range(len(cols))]

    def stack(x, mask):
        return jnp.where(mask, jnp.concatenate([x] * reps, axis=0), 0.0).astype(BF16)

    units = [(b, slice(q * unit, (q + 1) * unit), q) for b in range(nb) for q in range(nu)]
    each = lambda fn, *cols: [fn(*vals) for vals in zip(*cols)]

    def load(ref):
        return [ref[b, :, sl] for b, sl, _ in units]

    def row(ref):
        return [ref[:, sl] for _, sl, _ in units]

    r, k, v = (load(ref) for ref in (r_ref, k_ref, v_ref))
    k_k, k_a, r_k, ln_w, ln_b = (row(ref) for ref in (kk_ref, ka_ref, rk_ref, lnw_ref, lnb_ref))

    def low_rank(h_ref, w2_ref):
        wide = [_dot(h_ref[b], w2_ref[...]) for b in range(nb)]
        return [wide[b][:, sl] for b, sl, _ in units]

    wz = each(lambda z, bias: z + bias, low_rank(hw_ref, ww2_ref), row(w0_ref))
    az = each(lambda z, bias: z + bias, low_rank(ha_ref, aw2_ref), row(a0_ref))
    gate = low_rank(hg_ref, gw2_ref)

    logw = each(lambda z: -jnp.exp(-jax.nn.softplus(-z) - 0.5), wz)
    a = each(jax.nn.sigmoid, az)
    kk0 = each(lambda x, y: x * y, k, k_k)
    k2 = each(lambda x, al, ka: x * (1.0 + (al - 1.0) * ka), k, a, k_a)
    ssq, rk_sum = headsums(each(lambda x: x * x, kk0), each(lambda rr, kv, rk: rr * kv * rk, r, k2, r_k))
    kk = each(lambda x, s: x * lax.rsqrt(jnp.maximum(s, 1e-24)), kk0, ssq)
    bonus = each(lambda s, vv: s * vv, rk_sum, v)
    bvec = each(lambda x, al: x * al, kk, a)

    def cumsum(lw):
        p0 = lw.astype(BF16)
        r1 = lw - p0.astype(F32)
        p1 = r1.astype(BF16)
        p2 = (r1 - p1.astype(F32)).astype(BF16)
        return _dot(tri, p0) + _dot(tri, p1) + _dot(tri, p2)

    cum = each(cumsum, logw)
    cend = each(lambda x: x[c - 1:c, :], cum)
    a_t = each(lambda x, cu, lw: (-x * jnp.exp(cu - lw)).astype(BF16), kk, cum, logw)
    r_t = each(lambda x, cu: (x * jnp.exp(cu)).astype(BF16), r, cum)
    w_inv = each(lambda cu: jnp.exp(-cu), cum)
    w_rest = each(lambda ce, cu: jnp.exp(ce - cu), cend, cum)
    bs = each(lambda x, w: stack(x * w, stack_bd), bvec, w_inv)
    ks = each(lambda x, w: stack(x * w, stack_bd), k2, w_inv)
    vs = each(lambda x: stack(x, stack_bd), v)

    scores = each(lambda at, rt, b_, k_: _nt(jnp.concatenate([at, rt], axis=0), jnp.concatenate([b_, k_], axis=0)),
                  a_t, r_t, bs, ks)
    s_k = each(lambda s: jnp.concatenate([jnp.where(strict, s[:c, unit:], 0.0), jnp.where(incl, s[c:, unit:], 0.0)],
                                         axis=0).astype(BF16), scores)
    s_rb = each(lambda s: jnp.where(incl, s[c:, :unit], 0.0).astype(BF16), scores)

    pw = each(lambda s: jnp.where(strict, s[:c, :unit], 0.0), scores)
    t_cat = each(lambda p: eye_cat + p, pw)
    pw = each(lambda p: _dot(p.astype(BF16), stack(p, chunk_bd)), pw)
    for _ in range(c.bit_length() - 3):
        both = each(lambda t, p: _dot(jnp.concatenate([t, p], axis=0).astype(BF16), stack(p, chunk_bd)), t_cat, pw)
        t_cat = each(lambda t, r_: t + r_[:c], t_cat, both)
        pw = each(lambda r_: r_[c:], both)
    t_cat = each(lambda t, p: t + _dot(t.astype(BF16), stack(p, chunk_bd)), t_cat, pw)

    state = [state_ref[b, q] for b, _, q in units]
    state_b = each(lambda s: s.astype(BF16), state)
    base = each(lambda at, rt, sb, sk, vv: _nt(jnp.concatenate([at, rt], axis=0), sb) + _dot(sk, vv),
                a_t, r_t, state_b, s_k, vs)
    u = each(lambda t, bb: _dot(t.astype(BF16), stack(bb[:c], stack_bd)), t_cat, base)
    y = each(lambda bb, srb, uu: bb[c:] + _dot(srb, stack(uu, stack_bd)), base, s_rb, u)
    upd = each(lambda uu, vv, bv, kv, wr: _tn(jnp.concatenate([uu, vv], axis=0).astype(BF16),
                                               jnp.concatenate([bv * wr, kv * wr], axis=0).astype(BF16)),
               u, v, bvec, k2, w_rest)
    for (b, _, q), s, ce, up in zip(units, state, cend, upd):
        state_ref[b, q] = jnp.where(head_bd, s * jnp.exp(ce) + up, 0.0)

    inv_n = 1.0 / RWKV_HEAD
    (y_sum,) = headsums(y)
    yc = each(lambda yy, m: yy - m * inv_n, y, y_sum)
    (sq_sum,) = headsums(each(lambda z: z * z, yc))
    var = each(lambda s: s * inv_n, sq_sum)
    for (b, sl, _), z, vr, lw, lb, bo, gt in zip(units, yc, var, ln_w, ln_b, bonus, gate):
        o_ref[b, :, sl] = ((z * lax.rsqrt(vr + RWKV_GN_EPS) * lw + lb + bo) * gt).astype(o_ref.dtype)


def rwkv_core(r, k, v, low, low_w2, w0, a0, k_k, k_a, r_k, ln_w, ln_b, units_per_step=8):
    b, s, d = r.shape
    c = RWKV_CHUNK
    assert s % c == 0 and RWKV_UNIT % c == 0
    width = _tile(d, RWKV_UNIT * units_per_step)
    nu = width // RWKV_UNIT
    act = pl.BlockSpec((b, c, width), lambda i, j: (0, j, i))
    row = pl.BlockSpec((1, width), lambda i, j: (0, i))
    rows = [t.reshape(1, d).astype(F32) for t in (w0, a0, k_k, k_a, r_k, ln_w, ln_b)]
    return pl.pallas_call(
        _rwkv_kernel,
        grid=(d // width, s // c),
        in_specs=[act] * 3
        + [pl.BlockSpec((b, c, h.shape[2]), lambda i, j: (0, j, 0)) for h in low]
        + [pl.BlockSpec((w.shape[0], width), lambda i, j: (0, i)) for w in low_w2]
        + [row] * len(rows),
        out_specs=act,
        out_shape=jax.ShapeDtypeStruct((b, s, d), BF16),
        scratch_shapes=[pltpu.VMEM((b, nu, RWKV_UNIT, RWKV_UNIT), F32)],
        compiler_params=_params("parallel", "arbitrary"),
        name="rwkv7_chunk",
    )(r, k, v, *low, *low_w2, *rows)


def _rwkv_mix_kernel(x_ref, xp_ref, g_ref, mu_ref, ww1_ref, aw1_ref, gw1_ref,
                     xr_ref, xk_ref, xv_ref, hw_ref, ha_ref, hg_ref):
    gain = g_ref[...]
    hn = _rms(x_ref[0], gain)
    tail = xp_ref.shape[1]
    prev_last = _rms(xp_ref[0, tail - 1:tail, :], gain)
    prev_last = jnp.where(pl.program_id(1) > 0, prev_last, 0.0)
    row = lax.broadcasted_iota(jnp.int32, hn.shape, 0)
    xx = jnp.where(row == 0, prev_last, pltpu.roll(hn, 1, 0)) - hn
    mix = lambda i: (hn + xx * mu_ref[i:i + 1, :]).astype(BF16)
    xr_ref[0] = mix(0)
    xk_ref[0] = mix(2)
    xv_ref[0] = mix(3)
    hw_ref[0] = jnp.tanh(_dot(mix(1), ww1_ref[...])).astype(BF16)
    ha_ref[0] = _dot(mix(4), aw1_ref[...]).astype(BF16)
    hg_ref[0] = jax.nn.sigmoid(_dot(mix(5), gw1_ref[...])).astype(BF16)


def rwkv_mix(x, gain, mu, w_w1, a_w1, g_w1, tm=512, tail=SUBLANES):
    b, s, d = x.shape
    tm = _tile(s, tm)
    tok = lambda width: pl.BlockSpec((1, tm, width), lambda bi, i: (bi, i, 0))
    whole = lambda arr: pl.BlockSpec(arr.shape, lambda bi, i: (0, 0))
    ranks = [w.shape[1] for w in (w_w1, a_w1, g_w1)]
    mu = mu.astype(F32)
    return pl.pallas_call(
        _rwkv_mix_kernel,
        grid=(b, s // tm),
        in_specs=[tok(d),
                  pl.BlockSpec((1, tail, d), lambda bi, i: (bi, jnp.maximum(i * (tm // tail) - 1, 0), 0)),
                  pl.BlockSpec((1, d), lambda bi, i: (0, 0)),
                  whole(mu), whole(w_w1), whole(a_w1), whole(g_w1)],
        out_specs=[tok(d)] * 3 + [tok(rk) for rk in ranks],
        out_shape=[jax.ShapeDtypeStruct((b, s, d), BF16)] * 3
        + [jax.ShapeDtypeStruct((b, s, rk), BF16) for rk in ranks],
        compiler_params=_params("parallel", "parallel"),
        name="rwkv_mix",
    )(x, x, gain.reshape(1, d), mu, w_w1, a_w1, g_w1)


def _pad_cols(w, mult=128):
    n = w.shape[1]
    return jnp.pad(w, ((0, 0), (0, (-n) % mult)))


def _pad_rows(w, mult=128):
    n = w.shape[0]
    return jnp.pad(w, ((0, (-n) % mult), (0, 0)))


def rwkv_layer(x, b, s, gain, mu, w_r, w_k, w_v, w0, w_w1, w_w2, a0, a_w1, a_w2, g_w1, g_w2,
               k_k, k_a, r_k, ln_w, ln_b, w_o):
    t, d = x.shape
    bf = lambda w: w.astype(BF16)
    xr, xk, xv, *low = rwkv_mix(x.reshape(b, s, d), gain, mu,
                                bf(_pad_cols(w_w1)), bf(_pad_cols(a_w1)), bf(_pad_cols(g_w1)))
    r, k, v = (matmul(m.reshape(t, d), bf(w)).reshape(b, s, d) for m, w in ((xr, w_r), (xk, w_k), (xv, w_v)))
    low_w2 = [bf(_pad_rows(w)) for w in (w_w2, a_w2, g_w2)]
    y = rwkv_core(r, k, v, low, low_w2, w0, a0, k_k, k_a, r_k, ln_w, ln_b)
    return matmul(y.reshape(t, d), bf(w_o), residual=x)


def kernel(x, positions, l0_norm_mix, l0_attn_w_in, l0_attn_w_out, l0_norm_ffn, l0_ffn_w_in, l0_ffn_w_out, l1_norm_mix, l1_s5_a_re, l1_s5_a_im, l1_s5_log_dt, l1_s5_b_re, l1_s5_b_im, l1_s5_c_re, l1_s5_c_im, l1_s5_d, l1_s5_w_glu, l1_norm_ffn, l1_ffn_w_in, l1_ffn_w_out, l2_norm_mix, l2_rwkv_mu, l2_rwkv_w_r, l2_rwkv_w_k, l2_rwkv_w_v, l2_rwkv_w0, l2_rwkv_w_w1, l2_rwkv_w_w2, l2_rwkv_a0, l2_rwkv_a_w1, l2_rwkv_a_w2, l2_rwkv_g_w1, l2_rwkv_g_w2, l2_rwkv_k_k, l2_rwkv_k_a, l2_rwkv_r_k, l2_rwkv_ln_w, l2_rwkv_ln_b, l2_rwkv_w_o, l2_norm_ffn, l2_ffn_w_in, l2_ffn_w_out, l3_norm_mix, l3_attn_w_in, l3_attn_w_out, l3_norm_ffn, l3_ffn_w_in, l3_ffn_w_out, final_norm):
    b, s, d = x.shape
    h = x.reshape(b * s, d)
    rope = _rope_tables(positions)

    def channel_mixer(h, gain, w_in, w_out, final_gain=None):
        return ffn(h, gain, w_in.astype(BF16), w_out.astype(BF16), final_gain)

    h = attention_layer(h, b, s, rope, l0_norm_mix, l0_attn_w_in, l0_attn_w_out)
    h = channel_mixer(h, l0_norm_ffn, l0_ffn_w_in, l0_ffn_w_out)
    h = s5_layer(h, b, s, l1_norm_mix, l1_s5_a_re, l1_s5_a_im, l1_s5_log_dt, l1_s5_b_re, l1_s5_b_im,
                 l1_s5_c_re, l1_s5_c_im, l1_s5_d, l1_s5_w_glu)
    h = channel_mixer(h, l1_norm_ffn, l1_ffn_w_in, l1_ffn_w_out)
    h = rwkv_layer(h, b, s, l2_norm_mix, l2_rwkv_mu, l2_rwkv_w_r, l2_rwkv_w_k, l2_rwkv_w_v, l2_rwkv_w0,
                   l2_rwkv_w_w1, l2_rwkv_w_w2, l2_rwkv_a0, l2_rwkv_a_w1, l2_rwkv_a_w2, l2_rwkv_g_w1,
                   l2_rwkv_g_w2, l2_rwkv_k_k, l2_rwkv_k_a, l2_rwkv_r_k, l2_rwkv_ln_w, l2_rwkv_ln_b,
                   l2_rwkv_w_o)
    h = channel_mixer(h, l2_norm_ffn, l2_ffn_w_in, l2_ffn_w_out)
    h = attention_layer(h, b, s, rope, l3_norm_mix, l3_attn_w_in, l3_attn_w_out)
    h = channel_mixer(h, l3_norm_ffn, l3_ffn_w_in, l3_ffn_w_out, final_gain=final_norm)
    return h.reshape(b, s, d)
```

```python
import functools
import math

import jax
import jax.numpy as jnp
from jax import lax
from jax.experimental import pallas as pl
from jax.experimental.pallas import tpu as pltpu

F32 = jnp.float32
BF16 = jnp.bfloat16

RMS_EPS = 1e-6
LANES = 128
SUBLANES = 8
MXU_TILE = 256
VMEM_LIMIT_BYTES = 56 * 1024 * 1024

ATTN_HEAD_DIM = 128
ATTN_HEADS = 8
ATTN_GROUPS = ((128, 1), (512, 4), (2048, 16))
ATTN_BLOCK = 128
ATTN_BLOCKS_PER_STEP = 4
ROPE_THETA = 500000.0
ROPE_DIM = ATTN_HEAD_DIM // 4
ATTN_WIDTH = ATTN_HEADS * ATTN_HEAD_DIM
QKV_COL_CHUNK = MXU_TILE
NEG_BIG = -1e30

S5_GROUP = 16
S5_STATE = 64
S5_CHUNK = MXU_TILE // S5_GROUP

RWKV_HEAD = 64
RWKV_CHUNK = 64
RWKV_UNIT = MXU_TILE
RWKV_GN_EPS = 64e-5


def _params(*sem):
    return pltpu.CompilerParams(dimension_semantics=sem, vmem_limit_bytes=VMEM_LIMIT_BYTES)


def _tile(n, pref):
    t = min(n, pref)
    assert n % t == 0, (n, pref)
    return t


def _rms(x, g):
    return x * lax.rsqrt(jnp.mean(x * x, axis=-1, keepdims=True) + RMS_EPS) * g


def _nt(a, b):
    return lax.dot_general(a, b, (((1,), (1,)), ((), ())), preferred_element_type=F32)


def _tn(a, b):
    return lax.dot_general(a, b, (((0,), (0,)), ((), ())), preferred_element_type=F32)


def _dot(a, b):
    return jnp.dot(a, b, preferred_element_type=F32)


def _rmsnorm_kernel(x_ref, g_ref, o_ref):
    o_ref[...] = _rms(x_ref[...], g_ref[...]).astype(o_ref.dtype)


def rmsnorm(x, gain, tm=512):
    t, d = x.shape
    tm = _tile(t, tm)
    return pl.pallas_call(
        _rmsnorm_kernel,
        grid=(t // tm,),
        in_specs=[pl.BlockSpec((tm, d), lambda i: (i, 0)), pl.BlockSpec((1, d), lambda i: (0, 0))],
        out_specs=pl.BlockSpec((tm, d), lambda i: (i, 0)),
        out_shape=jax.ShapeDtypeStruct((t, d), F32),
        compiler_params=_params("parallel"),
        name="rmsnorm",
    )(x, gain.reshape(1, d))


def _mm_kernel(x_ref, w_ref, *refs):
    *r_ref, o_ref = refs
    acc = _dot(x_ref[...], w_ref[...])
    if r_ref:
        acc = acc + r_ref[0][...]
    o_ref[...] = acc.astype(o_ref.dtype)


def matmul(x, w, *, residual=None, tm=1024, tn=1024):
    t, k = x.shape
    n = w.shape[1]
    tm, tn = _tile(t, tm), _tile(n, tn)
    ins = [x, w]
    specs = [pl.BlockSpec((tm, k), lambda i, j: (i, 0)), pl.BlockSpec((k, tn), lambda i, j: (0, j))]
    if residual is not None:
        ins.append(residual)
        specs.append(pl.BlockSpec((tm, tn), lambda i, j: (i, j)))
    return pl.pallas_call(
        _mm_kernel,
        grid=(t // tm, n // tn),
        in_specs=specs,
        out_specs=pl.BlockSpec((tm, tn), lambda i, j: (i, j)),
        out_shape=jax.ShapeDtypeStruct((t, n), F32),
        compiler_params=_params("parallel", "parallel"),
        name="matmul",
    )(*ins)


def _ffn_kernel(x_ref, g_ref, wg_ref, wu_ref, wo_ref, fg_ref, o_ref, xn_ref, *, final_norm):
    j = pl.program_id(1)

    @pl.when(j == 0)
    def _():
        x = x_ref[...]
        xn_ref[...] = _rms(x, g_ref[...]).astype(BF16)
        o_ref[...] = x

    xn = xn_ref[...]
    gate = _dot(xn, wg_ref[...])
    up = _dot(xn, wu_ref[...])
    h = (gate * jax.nn.sigmoid(gate) * up).astype(BF16)
    o_ref[...] += _dot(h, wo_ref[...])

    if final_norm:
        @pl.when(j == pl.num_programs(1) - 1)
        def _():
            o_ref[...] = _rms(o_ref[...], fg_ref[...])


def ffn(x, gain, w_in, w_out, final_gain=None, tm=512, tf=512):
    t, d = x.shape
    f = w_out.shape[0]
    tm, tf = _tile(t, tm), _tile(f, tf)
    nf = f // tf
    fg = gain if final_gain is None else final_gain
    return pl.pallas_call(
        functools.partial(_ffn_kernel, final_norm=final_gain is not None),
        grid=(t // tm, nf),
        in_specs=[
            pl.BlockSpec((tm, d), lambda i, j: (i, 0)),
            pl.BlockSpec((1, d), lambda i, j: (0, 0)),
            pl.BlockSpec((d, tf), lambda i, j: (0, j)),
            pl.BlockSpec((d, tf), lambda i, j: (0, j + nf)),
            pl.BlockSpec((tf, d), lambda i, j: (j, 0)),
            pl.BlockSpec((1, d), lambda i, j: (0, 0)),
        ],
        out_specs=pl.BlockSpec((tm, d), lambda i, j: (i, 0)),
        out_shape=jax.ShapeDtypeStruct((t, d), F32),
        scratch_shapes=[pltpu.VMEM((tm, d), BF16)],
        compiler_params=_params("parallel", "arbitrary"),
        name="ffn",
    )(x, gain.reshape(1, d), w_in, w_in, w_out, fg.reshape(1, d))


def _norm_residues_kernel(x_ref, g_ref, *out_refs):
    hn = _rms(x_ref[0], g_ref[...]).astype(BF16)
    tm = hn.shape[0]
    dst = lax.broadcasted_iota(jnp.int32, (tm, tm), 0)
    src = lax.broadcasted_iota(jnp.int32, (tm, tm), 1)
    for o_ref in out_refs:
        d = o_ref.shape[1]
        if d == 1:
            o_ref[0, 0] = hn
            continue
        by_residue = (src == (dst % (tm // d)) * d + dst // (tm // d)).astype(BF16)
        o_ref[0] = _dot(by_residue, hn).astype(BF16).reshape(o_ref.shape[1:])


def norm_by_residue(x, gain, dilations, tm=512):
    b, s, d_model = x.shape
    tm = _tile(s, tm)
    return pl.pallas_call(
        _norm_residues_kernel,
        grid=(b, s // tm),
        in_specs=[pl.BlockSpec((1, tm, d_model), lambda bi, i: (bi, i, 0)),
                  pl.BlockSpec((1, d_model), lambda bi, i: (0, 0))],
        out_specs=[pl.BlockSpec((1, d, tm // d, d_model), lambda bi, i: (bi, 0, i, 0)) for d in dilations],
        out_shape=[jax.ShapeDtypeStruct((b, d, s // d, d_model), BF16) for d in dilations],
        compiler_params=_params("parallel", "parallel"),
        name="norm_by_residue",
    )(x, gain.reshape(1, d_model))


def _qkv_kernel(x_ref, w_ref, cos_ref, sin_ref, o_ref):
    kind = pl.program_id(1)
    scale = jnp.where(kind == 0, ATTN_HEAD_DIM ** -0.5, 1.0).astype(F32)
    is_v = kind == 2
    cos = jnp.where(is_v, 1.0, cos_ref[...] * scale)
    sin = jnp.where(is_v, 0.0, sin_ref[...] * scale)
    x = x_ref[...]
    for c0 in range(0, o_ref.shape[1], QKV_COL_CHUNK):
        acc = _dot(x, w_ref[:, c0:c0 + QKV_COL_CHUNK])
        for h0 in range(0, QKV_COL_CHUNK, ATTN_HEAD_DIM):
            xh = acc[:, h0:h0 + ATTN_HEAD_DIM]
            rot = pltpu.roll(xh, ATTN_HEAD_DIM // 2, 1)
            o_ref[:, c0 + h0:c0 + h0 + ATTN_HEAD_DIM] = (xh * cos + rot * sin).astype(o_ref.dtype)


def qkv_projection(x, w_in, gi, cos_t, sin_t, tm=1024):
    t, d = x.shape
    tm = _tile(t, tm)
    tn = ATTN_WIDTH
    return pl.pallas_call(
        _qkv_kernel,
        grid=(t // tm, 3),
        in_specs=[
            pl.BlockSpec((tm, d), lambda i, j: (i, 0)),
            pl.BlockSpec((d, tn), lambda i, j: (0, gi * 3 + j)),
            pl.BlockSpec((tm, ATTN_HEAD_DIM), lambda i, j: (i, 0)),
            pl.BlockSpec((tm, ATTN_HEAD_DIM), lambda i, j: (i, 0)),
        ],
        out_specs=pl.BlockSpec((tm, tn), lambda i, j: (i, j)),
        out_shape=jax.ShapeDtypeStruct((t, 3 * tn), BF16),
        compiler_params=_params("parallel", "parallel"),
        name="qkv_rope",
    )(x, w_in, cos_t, sin_t)


def _attn_kernel(q_ref, kp_ref, kc_ref, vp_ref, vc_ref, o_ref, st_ref):
    n = pl.program_id(1)
    blk = ATTN_BLOCK
    qi = lax.broadcasted_iota(jnp.int32, (blk, 2 * blk), 0)
    kj = lax.broadcasted_iota(jnp.int32, (blk, 2 * blk), 1)
    dist = qi + blk - kj
    band = (dist >= 0) & (dist <= blk)
    first = band & ((kj >= blk) | (n > 0))
    lane = lax.broadcasted_iota(jnp.int32, (blk, ATTN_HEAD_DIM), 1)
    stats = [jnp.zeros((blk, ATTN_HEAD_DIM), F32) for _ in range(ATTN_BLOCKS_PER_STEP)]
    for h in range(ATTN_HEADS):
        sl = slice(h * ATTN_HEAD_DIM, (h + 1) * ATTN_HEAD_DIM)
        k = jnp.concatenate([kp_ref[0, :, sl], kc_ref[0, :, sl]], axis=0)
        v = jnp.concatenate([vp_ref[0, :, sl], vc_ref[0, :, sl]], axis=0)
        for j in range(ATTN_BLOCKS_PER_STEP):
            rows = slice(j * blk, (j + 1) * blk)
            keys = slice(j * blk, (j + 2) * blk)
            s = jnp.where(first if j == 0 else band, _nt(q_ref[0, rows, sl], k[keys]), NEG_BIG)
            mx = jnp.max(s, axis=-1, keepdims=True)
            p = jnp.exp(s - mx)
            den = jnp.sum(p, axis=-1, keepdims=True)
            num = _dot(p.astype(BF16), v[keys])
            o_ref[0, rows, sl] = (num / den).astype(o_ref.dtype)
            stats[j] = jnp.where(lane == h, mx + jnp.log(den), stats[j])
    for j in range(ATTN_BLOCKS_PER_STEP):
        st_ref[0, j * blk:(j + 1) * blk, :] = stats[j]


def attention_group(qkv):
    nseq, length, _ = qkv.shape
    per = ATTN_BLOCKS_PER_STEP
    rows = per * ATTN_BLOCK
    assert length % rows == 0
    nb = length // rows
    blk = (1, rows, ATTN_WIDTH)

    def cur(off):
        return pl.BlockSpec(blk, lambda r, nn: (r, nn, off))

    def prev(off):
        return pl.BlockSpec((1, ATTN_BLOCK, ATTN_WIDTH), lambda r, nn: (r, jnp.maximum(nn * per - 1, 0), off))

    return pl.pallas_call(
        _attn_kernel,
        grid=(nseq, nb),
        in_specs=[cur(0), prev(1), cur(1), prev(2), cur(2)],
        out_specs=[
            pl.BlockSpec(blk, lambda r, nn: (r, nn, 0)),
            pl.BlockSpec((1, rows, ATTN_HEAD_DIM), lambda r, nn: (r, nn, 0)),
        ],
        out_shape=[
            jax.ShapeDtypeStruct((nseq, length, ATTN_WIDTH), BF16),
            jax.ShapeDtypeStruct((nseq, length, ATTN_HEAD_DIM), F32),
        ],
        compiler_params=_params("parallel", "parallel"),
        name="dilated_attn",
    )(qkv, qkv, qkv, qkv, qkv)


def _attn_out_kernel(*refs):
    ng = len(ATTN_GROUPS)
    o_refs, s_refs = refs[:ng], refs[ng:2 * ng]
    w_ref, r_ref, out_ref, om_ref, su_ref = refs[2 * ng:]

    @pl.when(pl.program_id(2) == 0)
    def _():
        tm = om_ref.shape[0]
        tok = lax.broadcasted_iota(jnp.int32, (tm, tm), 0)
        src = lax.broadcasted_iota(jnp.int32, (tm, tm), 1)
        outs = []
        for g in range(ng):
            d = o_refs[g].shape[1]
            for r in range(d):
                su_ref[g, pl.ds(r, tm // d, stride=d), :] = s_refs[g][0, r]
            o = o_refs[g][0].reshape(tm, ATTN_WIDTH)
            if d > 1:
                to_token_order = (src == (tok % d) * (tm // d) + tok // d).astype(BF16)
                o = _dot(to_token_order, o)
            outs.append(o)
        lse = [su_ref[g] for g in range(ng)]
        m = functools.reduce(jnp.maximum, lse)
        e = [jnp.exp(l - m) for l in lse]
        inv = 1.0 / sum(e)
        wts = [e[g] * inv for g in range(1, ng)]
        for h in range(ATTN_HEADS):
            sl = slice(h * ATTN_HEAD_DIM, (h + 1) * ATTN_HEAD_DIM)
            base = outs[0][:, sl].astype(F32)
            om = base + sum(wt[:, h:h + 1] * (o[:, sl].astype(F32) - base) for wt, o in zip(wts, outs[1:]))
            om_ref[:, sl] = om.astype(BF16)

    out_ref[0] = r_ref[0] + _dot(om_ref[...], w_ref[...])


def attention_out(parts, w_out, residual, tm=512, tn=2048):
    b, s, n = residual.shape
    k = w_out.shape[0]
    tm, tn = _tile(s, tm), _tile(n, tn)
    ng = len(parts)

    def by_residue(arr):
        d, width = arr.shape[1], arr.shape[3]
        return pl.BlockSpec((1, d, tm // d, width), lambda bi, i, j: (bi, 0, i, 0))

    res = pl.BlockSpec((1, tm, tn), lambda bi, i, j: (bi, i, j))
    return pl.pallas_call(
        _attn_out_kernel,
        grid=(b, s // tm, n // tn),
        in_specs=[by_residue(o) for o, _ in parts] + [by_residue(st) for _, st in parts]
        + [pl.BlockSpec((k, tn), lambda bi, i, j: (0, j)), res],
        out_specs=res,
        out_shape=jax.ShapeDtypeStruct((b, s, n), F32),
        scratch_shapes=[pltpu.VMEM((tm, k), BF16),
                        pltpu.VMEM((ng, tm, ATTN_HEAD_DIM), F32)],
        compiler_params=_params("parallel", "parallel", "arbitrary"),
        name="attn_merge_out",
    )(*[o for o, _ in parts], *[st for _, st in parts], w_out, residual)


def _by_residue(t, dilation):
    b, s = t.shape[:2]
    return jnp.swapaxes(t.reshape(b, s // dilation, dilation, *t.shape[2:]), 1, 2)


def _qkv_weight_kernel(w_ref, o_ref):
    w = w_ref[...].astype(BF16)
    kind = pl.program_id(0) % 3

    @pl.when(kind == 2)
    def _():
        o_ref[...] = w

    @pl.when(kind != 2)
    def _():
        half = ROPE_DIM // 2
        mid = ATTN_HEAD_DIM // 2
        src = lax.broadcasted_iota(jnp.int32, (ATTN_HEAD_DIM, ATTN_HEAD_DIM), 0)
        dst = lax.broadcasted_iota(jnp.int32, (ATTN_HEAD_DIM, ATTN_HEAD_DIM), 1)
        from_lane = jnp.where(dst < half, dst,
                              jnp.where(dst < mid, dst + half, jnp.where(dst < mid + half, dst - mid + half, dst)))
        reorder = (src == from_lane).astype(BF16)
        for h in range(ATTN_HEADS):
            sl = slice(h * ATTN_HEAD_DIM, (h + 1) * ATTN_HEAD_DIM)
            o_ref[:, sl] = _dot(w[:, sl], reorder).astype(BF16)


def _rope_layout(w_in):
    d, n = w_in.shape
    return pl.pallas_call(
        _qkv_weight_kernel,
        grid=(n // ATTN_WIDTH,),
        in_specs=[pl.BlockSpec((d, ATTN_WIDTH), lambda j: (0, j))],
        out_specs=pl.BlockSpec((d, ATTN_WIDTH), lambda j: (0, j)),
        out_shape=jax.ShapeDtypeStruct((d, n), BF16),
        compiler_params=_params("parallel"),
        name="qkv_weight_layout",
    )(w_in)


def _rope_tables(positions):
    half = ROPE_DIM // 2
    inv_freq = ROPE_THETA ** (-jnp.arange(half, dtype=F32) * 2.0 / ROPE_DIM)
    lane = jnp.arange(ATTN_HEAD_DIM)
    rotary = lane % (ATTN_HEAD_DIM // 2) < half
    freq = jnp.where(rotary, inv_freq[lane % half], 0.0)
    sign = jnp.where(lane < half, -1.0, 1.0).astype(F32)
    tables = []
    for _, dil in ATTN_GROUPS:
        ang = _by_residue(positions, dil).astype(F32).reshape(-1, 1) * freq
        tables.append((jnp.cos(ang), sign * jnp.sin(ang)))
    return tables


def attention_layer(x, b, s, rope, gain, w_in, w_out):
    t, d = x.shape
    w_in = _rope_layout(w_in)
    dils = [dil for _, dil in ATTN_GROUPS]
    hns = norm_by_residue(x.reshape(b, s, d), gain, dils)
    parts = []
    for gi, (dil, hn, (cos_t, sin_t)) in enumerate(zip(dils, hns, rope)):
        qkv = qkv_projection(hn.reshape(t, d), w_in, gi, cos_t, sin_t)
        o, st = attention_group(qkv.reshape(b * dil, s // dil, -1))
        parts.append((o.reshape(b, dil, s // dil, -1), st.reshape(b, dil, s // dil, -1)))
    return attention_out(parts, w_out.astype(BF16), x.reshape(b, s, d)).reshape(t, d)


def _gelu_tanh(y):
    return 0.5 * y * (1.0 + jnp.tanh(math.sqrt(2.0 / math.pi) * (y + 0.044715 * (y * y * y))))


def _s5_kernel(u_ref, perm_ref, toep_ref, wp_ref, q_ref, c1_ref, c2_ref, z_ref):
    lc = S5_CHUNK
    groups = toep_ref.shape[0]
    nch = u_ref.shape[1] // lc
    width = lc * S5_GROUP
    nsteps = c1_ref.shape[1]
    by_pos = jnp.concatenate([u_ref[0, pl.ds(s, nch, stride=lc), :].astype(BF16) for s in range(lc)], axis=1)
    regroup = lambda g: _dot(by_pos, perm_ref[:, g * width:(g + 1) * width]).astype(BF16)
    row = lax.broadcasted_iota(jnp.int32, (nch, 2 * S5_STATE), 0)
    zs = []
    u_next = regroup(0)
    for g in range(groups):
        u, u_next = u_next, (regroup(g + 1) if g + 1 < groups else None)
        y = _dot(u, toep_ref[g])
        x = _dot(u, wp_ref[g])
        for i in range(nsteps):
            m = 1 << i
            sh = jnp.where(row >= m, pltpu.roll(x, m, 0), 0.0)
            x = x + c1_ref[g, i:i + 1, :] * sh + c2_ref[g, i:i + 1, :] * pltpu.roll(sh, S5_STATE, 1)
        xprev = jnp.where(row >= 1, pltpu.roll(x, 1, 0), 0.0)
        y = y + _dot(xprev.astype(BF16), q_ref[g])
        zs.append(_gelu_tanh(y).astype(BF16))
    out = _nt(jnp.concatenate(zs, axis=1), perm_ref[...])
    for s in range(lc):
        z_ref[0, pl.ds(s, nch, stride=lc), :] = out[:, s * LANES:(s + 1) * LANES]


def _s5_operators(a_re, a_im, log_dt, b_re, b_im, c_re, c_im, d_skip, chunks_per_seq):
    lc = S5_CHUNK
    g, p = a_re.shape
    dt = jnp.exp(log_dt)[:, None]
    mag = jnp.exp(dt * a_re)
    ab_re = mag * jnp.cos(dt * a_im)
    ab_im = mag * jnp.sin(dt * a_im)
    inv = 1.0 / (a_re * a_re + a_im * a_im)
    f_re = ((ab_re - 1.0) * a_re + ab_im * a_im) * inv
    f_im = (ab_im * a_re - (ab_re - 1.0) * a_im) * inv
    bb_re = f_re[..., None] * b_re - f_im[..., None] * b_im
    bb_im = f_re[..., None] * b_im + f_im[..., None] * b_re

    def power(j):
        jf = j.astype(F32)[:, None, None]
        m = jnp.exp(jf * (dt * a_re))
        return m * jnp.cos(jf * (dt * a_im)), m * jnp.sin(jf * (dt * a_im))

    pr, pi = power(jnp.arange(lc + 1))
    ba_re = pr[..., None] * bb_re - pi[..., None] * bb_im
    ba_im = pr[..., None] * bb_im + pi[..., None] * bb_re
    kern = (jnp.einsum("jgpa,gcp->jgac", ba_re, c_re) - jnp.einsum("jgpa,gcp->jgac", ba_im, c_im))
    ti = jnp.arange(lc)
    lag = ti[None, :] - ti[:, None]
    toep = jnp.where((lag >= 0)[None, :, None, :, None],
                     jnp.transpose(kern[jnp.clip(lag, 0, lc)], (2, 0, 3, 1, 4)), 0.0)
    eye = jnp.eye(lc, dtype=F32)[:, None, :, None] * jnp.eye(S5_GROUP, dtype=F32)[None, :, None, :]
    toep = toep + d_skip.reshape(g, 1, S5_GROUP, 1, 1) * eye[None]
    toep = toep.reshape(g, lc * S5_GROUP, lc * S5_GROUP)
    rev = lc - 1 - ti
    wp = jnp.concatenate([ba_re[rev], ba_im[rev]], axis=2)
    wp = jnp.transpose(wp, (1, 0, 3, 2)).reshape(g, lc * S5_GROUP, 2 * p)
    qr, qi = pr[1:], pi[1:]
    q_top = (jnp.einsum("tgp,gcp->gptc", qr, c_re) - jnp.einsum("tgp,gcp->gptc", qi, c_im))
    q_bot = (-jnp.einsum("tgp,gcp->gptc", qi, c_re) - jnp.einsum("tgp,gcp->gptc", qr, c_im))
    q = jnp.concatenate([q_top, q_bot], axis=1).reshape(g, 2 * p, lc * S5_GROUP)
    nsteps = chunks_per_seq.bit_length() - 1
    sr, si = power(lc * (2 ** jnp.arange(nsteps)))
    c1 = jnp.transpose(jnp.concatenate([sr, sr], axis=2), (1, 0, 2))
    c2 = jnp.transpose(jnp.concatenate([-si, si], axis=2), (1, 0, 2))
    return toep.astype(BF16), wp.astype(BF16), q.astype(BF16), c1, c2


def s5_core(u, ops):
    b, s, d = u.shape
    lc = S5_CHUNK
    cps = s // lc
    assert cps & (cps - 1) == 0 and d % LANES == 0
    gs = LANES // S5_GROUP
    width = lc * S5_GROUP
    toep, wp, q, c1, c2 = ops
    nsteps = c1.shape[1]
    i = jnp.arange(lc * LANES)
    dest = (i % LANES // S5_GROUP) * width + (i // LANES) * S5_GROUP + i % S5_GROUP
    perm = (dest[:, None] == i[None, :]).astype(BF16)
    per_tile = lambda shape: pl.BlockSpec((gs,) + shape, lambda ti, bi: (ti, 0, 0))
    act = pl.BlockSpec((1, s, LANES), lambda ti, bi: (bi, 0, ti))
    return pl.pallas_call(
        _s5_kernel,
        grid=(d // LANES, b),
        in_specs=[
            act,
            pl.BlockSpec((lc * LANES, lc * LANES), lambda ti, bi: (0, 0), pipeline_mode=pl.Buffered(1)),
            per_tile((width, width)),
            per_tile((width, 2 * S5_STATE)),
            per_tile((2 * S5_STATE, width)),
            per_tile((nsteps, 2 * S5_STATE)),
            per_tile((nsteps, 2 * S5_STATE)),
        ],
        out_specs=act,
        out_shape=jax.ShapeDtypeStruct((b, s, d), F32),
        compiler_params=_params("parallel", "parallel"),
        name="s5_chunk_scan",
    )(u, perm, toep, wp, q, c1, c2)


def _glu_kernel(z_ref, wv_ref, wg_ref, r_ref, o_ref):
    z = z_ref[...].astype(BF16)
    val = _dot(z, wv_ref[...])
    gate = _dot(z, wg_ref[...])
    o_ref[...] = r_ref[...] + val * jax.nn.sigmoid(gate)


def glu_out(z, w_glu, residual, tm=1024, tn=1024):
    t, k = z.shape
    n = w_glu.shape[1] // 2
    tm, tn = _tile(t, tm), _tile(n, tn)
    nn = n // tn
    return pl.pallas_call(
        _glu_kernel,
        grid=(t // tm, nn),
        in_specs=[
            pl.BlockSpec((tm, k), lambda i, j: (i, 0)),
            pl.BlockSpec((k, tn), lambda i, j: (0, j)),
            pl.BlockSpec((k, tn), lambda i, j: (0, j + nn)),
            pl.BlockSpec((tm, tn), lambda i, j: (i, j)),
        ],
        out_specs=pl.BlockSpec((tm, tn), lambda i, j: (i, j)),
        out_shape=jax.ShapeDtypeStruct((t, n), F32),
        compiler_params=_params("parallel", "parallel"),
        name="glu_out",
    )(z, w_glu, w_glu, residual)


def s5_layer(x, b, s, gain, a_re, a_im, log_dt, b_re, b_im, c_re, c_im, d_skip, w_glu):
    u = rmsnorm(x, gain)
    ops = _s5_operators(a_re, a_im, log_dt, b_re, b_im, c_re, c_im, d_skip, s // S5_CHUNK)
    z = s5_core(u.reshape(b, s, -1), ops)
    return glu_out(z.reshape(x.shape), w_glu.astype(BF16), x)


def _split2(x):
    hi = x.astype(BF16)
    lo = (x - hi.astype(F32)).astype(BF16)
    return hi, lo


def _rwkv_kernel(r_ref, k_ref, v_ref, hw_ref, ha_ref, hg_ref, ww2_ref, aw2_ref, gw2_ref, w0_ref, a0_ref,
                 kk_ref, ka_ref, rk_ref, lnw_ref, lnb_ref, o_ref, state_ref):
    nb, c, width = r_ref.shape
    unit = RWKV_UNIT
    nu = width // unit
    reps = unit // c

    @pl.when(pl.program_id(1) == 0)
    def _():
        state_ref[...] = jnp.zeros_like(state_ref)

    ri = lax.broadcasted_iota(jnp.int32, (unit, unit), 0)
    ci = lax.broadcasted_iota(jnp.int32, (unit, unit), 1)
    head_bd = (ri // RWKV_HEAD) == (ci // RWKV_HEAD)
    stack_bd = (ri // c) == (ci // RWKV_HEAD)
    chunk_bd = (ri // c) == (ci // c)
    ones_bd = head_bd.astype(BF16)
    tr = lax.broadcasted_iota(jnp.int32, (c, unit), 0)
    tc = lax.broadcasted_iota(jnp.int32, (c, unit), 1) % c
    eye_cat = (tc == tr).astype(F32)
    strict = tc < tr
    incl = tc <= tr
    li = lax.broadcasted_iota(jnp.int32, (c, c), 0)
    lj = lax.broadcasted_iota(jnp.int32, (c, c), 1)
    tri = (lj <= li).astype(BF16)

    def headsums(*cols):
        pieces = [p for xs in zip(*cols) for x in xs for p in _split2(x)]
        parts = _dot(jnp.concatenate(pieces, axis=0), ones_bd)
        sums = [parts[2 * i * c:(2 * i + 1) * c] + parts[(2 * i + 1) * c:(2 * i + 2) * c]
                for i in range(len(pieces) // 2)]
        return [sums[i::len(cols)] for i in range(len(cols))]

    def stack(x, mask):
        return jnp.where(mask, jnp.concatenate([x] * reps, axis=0), 0.0).astype(BF16)

    units = [(b, slice(q * unit, (q + 1) * unit), q) for b in range(nb) for q in range(nu)]
    each = lambda fn, *cols: [fn(*vals) for vals in zip(*cols)]

    def load(ref):
        return [ref[b, :, sl] for b, sl, _ in units]

    def row(ref):
        return [ref[:, sl] for _, sl, _ in units]

    r, k, v = (load(ref) for ref in (r_ref, k_ref, v_ref))
    k_k, k_a, r_k, ln_w, ln_b = (row(ref) for ref in (kk_ref, ka_ref, rk_ref, lnw_ref, lnb_ref))

    def low_rank(h_ref, w2_ref):
        wide = [_dot(h_ref[b], w2_ref[...]) for b in range(nb)]
        return [wide[b][:, sl] for b, sl, _ in units]

    wz = each(lambda z, bias: z + bias, low_rank(hw_ref, ww2_ref), row(w0_ref))
    az = each(lambda z, bias: z + bias, low_rank(ha_ref, aw2_ref), row(a0_ref))
    gate = low_rank(hg_ref, gw2_ref)

    logw = each(lambda z: -jnp.exp(-jax.nn.softplus(-z) - 0.5), wz)
    a = each(jax.nn.sigmoid, az)
    kk0 = each(lambda x, y: x * y, k, k_k)
    k2 = each(lambda x, al, ka: x * (1.0 + (al - 1.0) * ka), k, a, k_a)
    ssq, rk_sum = headsums(each(lambda x: x * x, kk0), each(lambda rr, kv, rk: rr * kv * rk, r, k2, r_k))
    kk = each(lambda x, s: x * lax.rsqrt(jnp.maximum(s, 1e-24)), kk0, ssq)
    bonus = each(lambda s, vv: s * vv, rk_sum, v)
    bvec = each(lambda x, al: x * al, kk, a)

    def cumsum(lw):
        p0 = lw.astype(BF16)
        r1 = lw - p0.astype(F32)
        p1 = r1.astype(BF16)
        p2 = (r1 - p1.astype(F32)).astype(BF16)
        return _dot(tri, p0) + _dot(tri, p1) + _dot(tri, p2)

    cum = each(cumsum, logw)
    cend = each(lambda x: x[c - 1:c, :], cum)
    a_t = each(lambda x, cu, lw: (-x * jnp.exp(cu - lw)).astype(BF16), kk, cum, logw)
    r_t = each(lambda x, cu: (x * jnp.exp(cu)).astype(BF16), r, cum)
    w_inv = each(lambda cu: jnp.exp(-cu), cum)
    w_rest = each(lambda ce, cu: jnp.exp(ce - cu), cend, cum)
    bs = each(lambda x, w: stack(x * w, stack_bd), bvec, w_inv)
    ks = each(lambda x, w: stack(x * w, stack_bd), k2, w_inv)
    vs = each(lambda x: stack(x, stack_bd), v)

    scores = each(lambda at, rt, b_, k_: _nt(jnp.concatenate([at, rt], axis=0), jnp.concatenate([b_, k_], axis=0)),
                  a_t, r_t, bs, ks)
    s_k = each(lambda s: jnp.concatenate([jnp.where(strict, s[:c, unit:], 0.0), jnp.where(incl, s[c:, unit:], 0.0)],
                                         axis=0).astype(BF16), scores)
    s_rb = each(lambda s: jnp.where(incl, s[c:, :unit], 0.0).astype(BF16), scores)

    pw = each(lambda s: jnp.where(strict, s[:c, :unit], 0.0), scores)
    t_cat = each(lambda p: eye_cat + p, pw)
    pw = each(lambda p: _dot(p.astype(BF16), stack(p, chunk_bd)), pw)
    for _ in range(c.bit_length() - 3):
        both = each(lambda t, p: _dot(jnp.concatenate([t, p], axis=0).astype(BF16), stack(p, chunk_bd)), t_cat, pw)
        t_cat = each(lambda t, r_: t + r_[:c], t_cat, both)
        pw = each(lambda r_: r_[c:], both)
    t_cat = each(lambda t, p: t + _dot(t.astype(BF16), stack(p, chunk_bd)), t_cat, pw)

    state = [state_ref[b, q] for b, _, q in units]
    state_b = each(lambda s: s.astype(BF16), state)
    base = each(lambda at, rt, sb, sk, vv: _nt(jnp.concatenate([at, rt], axis=0), sb) + _dot(sk, vv),
                a_t, r_t, state_b, s_k, vs)
    u = each(lambda t, bb: _dot(t.astype(BF16), stack(bb[:c], stack_bd)), t_cat, base)
    y = each(lambda bb, srb, uu: bb[c:] + _dot(srb, stack(uu, stack_bd)), base, s_rb, u)
    upd = each(lambda uu, vv, bv, kv, wr: _tn(jnp.concatenate([uu, vv], axis=0).astype(BF16),
                                               jnp.concatenate([bv * wr, kv * wr], axis=0).astype(BF16)),
               u, v, bvec, k2, w_rest)
    for (b, _, q), s, ce, up in zip(units, state, cend, upd):
        state_ref[b, q] = jnp.where(head_bd, s * jnp.exp(ce) + up, 0.0)

    inv_n = 1.0 / RWKV_HEAD
    (y_sum,) = headsums(y)
    yc = each(lambda yy, m: yy - m * inv_n, y, y_sum)
    (sq_sum,) = headsums(each(lambda z: z * z, yc))
    var = each(lambda s: s * inv_n, sq_sum)
    for (b, sl, _), z, vr, lw, lb, bo, gt in zip(units, yc, var, ln_w, ln_b, bonus, gate):
        o_ref[b, :, sl] = ((z * lax.rsqrt(vr + RWKV_GN_EPS) * lw + lb + bo) * gt).astype(o_ref.dtype)


def rwkv_core(r, k, v, low, low_w2, w0, a0, k_k, k_a, r_k, ln_w, ln_b, units_per_step=8):
    b, s, d = r.shape
    c = RWKV_CHUNK
    assert s % c == 0 and RWKV_UNIT % c == 0
    width = _tile(d, RWKV_UNIT * units_per_step)
    nu = width // RWKV_UNIT
    act = pl.BlockSpec((b, c, width), lambda i, j: (0, j, i))
    row = pl.BlockSpec((1, width), lambda i, j: (0, i))
    rows = [t.reshape(1, d).astype(F32) for t in (w0, a0, k_k, k_a, r_k, ln_w, ln_b)]
    return pl.pallas_call(
        _rwkv_kernel,
        grid=(d // width, s // c),
        in_specs=[act] * 3
        + [pl.BlockSpec((b, c, h.shape[2]), lambda i, j: (0, j, 0)) for h in low]
        + [pl.BlockSpec((w.shape[0], width), lambda i, j: (0, i)) for w in low_w2]
        + [row] * len(rows),
        out_specs=act,
        out_shape=jax.ShapeDtypeStruct((b, s, d), BF16),
        scratch_shapes=[pltpu.VMEM((b, nu, RWKV_UNIT, RWKV_UNIT), F32)],
        compiler_params=_params("parallel", "arbitrary"),
        name="rwkv7_chunk",
    )(r, k, v, *low, *low_w2, *rows)


def _rwkv_mix_kernel(x_ref, xp_ref, g_ref, mu_ref, ww1_ref, aw1_ref, gw1_ref,
                     xr_ref, xk_ref, xv_ref, hw_ref, ha_ref, hg_ref):
    gain = g_ref[...]
    hn = _rms(x_ref[0], gain)
    tail = xp_ref.shape[1]
    prev_last = _rms(xp_ref[0, tail - 1:tail, :], gain)
    prev_last = jnp.where(pl.program_id(1) > 0, prev_last, 0.0)
    row = lax.broadcasted_iota(jnp.int32, hn.shape, 0)
    xx = jnp.where(row == 0, prev_last, pltpu.roll(hn, 1, 0)) - hn
    mix = lambda i: (hn + xx * mu_ref[i:i + 1, :]).astype(BF16)
    xr_ref[0] = mix(0)
    xk_ref[0] = mix(2)
    xv_ref[0] = mix(3)
    hw_ref[0] = jnp.tanh(_dot(mix(1), ww1_ref[...])).astype(BF16)
    ha_ref[0] = _dot(mix(4), aw1_ref[...]).astype(BF16)
    hg_ref[0] = jax.nn.sigmoid(_dot(mix(5), gw1_ref[...])).astype(BF16)


def rwkv_mix(x, gain, mu, w_w1, a_w1, g_w1, tm=512, tail=SUBLANES):
    b, s, d = x.shape
    tm = _tile(s, tm)
    tok = lambda width: pl.BlockSpec((1, tm, width), lambda bi, i: (bi, i, 0))
    whole = lambda arr: pl.BlockSpec(arr.shape, lambda bi, i: (0, 0))
    ranks = [w.shape[1] for w in (w_w1, a_w1, g_w1)]
    mu = mu.astype(F32)
    return pl.pallas_call(
        _rwkv_mix_kernel,
        grid=(b, s // tm),
        in_specs=[tok(d),
                  pl.BlockSpec((1, tail, d), lambda bi, i: (bi, jnp.maximum(i * (tm // tail) - 1, 0), 0)),
                  pl.BlockSpec((1, d), lambda bi, i: (0, 0)),
                  whole(mu), whole(w_w1), whole(a_w1), whole(g_w1)],
        out_specs=[tok(d)] * 3 + [tok(rk) for rk in ranks],
        out_shape=[jax.ShapeDtypeStruct((b, s, d), BF16)] * 3
        + [jax.ShapeDtypeStruct((b, s, rk), BF16) for rk in ranks],
        compiler_params=_params("parallel", "parallel"),
        name="rwkv_mix",
    )(x, x, gain.reshape(1, d), mu, w_w1, a_w1, g_w1)


def _pad_cols(w, mult=128):
    n = w.shape[1]
    return jnp.pad(w, ((0, 0), (0, (-n) % mult)))


def _pad_rows(w, mult=128):
    n = w.shape[0]
    return jnp.pad(w, ((0, (-n) % mult), (0, 0)))


def rwkv_layer(x, b, s, gain, mu, w_r, w_k, w_v, w0, w_w1, w_w2, a0, a_w1, a_w2, g_w1, g_w2,
               k_k, k_a, r_k, ln_w, ln_b, w_o):
    t, d = x.shape
    bf = lambda w: w.astype(BF16)
    xr, xk, xv, *low = rwkv_mix(x.reshape(b, s, d), gain, mu,
                                bf(_pad_cols(w_w1)), bf(_pad_cols(a_w1)), bf(_pad_cols(g_w1)))
    r, k, v = (matmul(m.reshape(t, d), bf(w)).reshape(b, s, d) for m, w in ((xr, w_r), (xk, w_k), (xv, w_v)))
    low_w2 = [bf(_pad_rows(w)) for w in (w_w2, a_w2, g_w2)]
    y = rwkv_core(r, k, v, low, low_w2, w0, a0, k_k, k_a, r_k, ln_w, ln_b)
    return matmul(y.reshape(t, d), bf(w_o), residual=x)


def kernel(x, positions, l0_norm_mix, l0_attn_w_in, l0_attn_w_out, l0_norm_ffn, l0_ffn_w_in, l0_ffn_w_out, l1_norm_mix, l1_s5_a_re, l1_s5_a_im, l1_s5_log_dt, l1_s5_b_re, l1_s5_b_im, l1_s5_c_re, l1_s5_c_im, l1_s5_d, l1_s5_w_glu, l1_norm_ffn, l1_ffn_w_in, l1_ffn_w_out, l2_norm_mix, l2_rwkv_mu, l2_rwkv_w_r, l2_rwkv_w_k, l2_rwkv_w_v, l2_rwkv_w0, l2_rwkv_w_w1, l2_rwkv_w_w2, l2_rwkv_a0, l2_rwkv_a_w1, l2_rwkv_a_w2, l2_rwkv_g_w1, l2_rwkv_g_w2, l2_rwkv_k_k, l2_rwkv_k_a, l2_rwkv_r_k, l2_rwkv_ln_w, l2_rwkv_ln_b, l2_rwkv_w_o, l2_norm_ffn, l2_ffn_w_in, l2_ffn_w_out, l3_norm_mix, l3_attn_w_in, l3_attn_w_out, l3_norm_ffn, l3_ffn_w_in, l3_ffn_w_out, final_norm):
    b, s, d = x.shape
    h = x.reshape(b * s, d)
    rope = _rope_tables(positions)

    def channel_mixer(h, gain, w_in, w_out, final_gain=None):
        return ffn(h, gain, w_in.astype(BF16), w_out.astype(BF16), final_gain)

    h = attention_layer(h, b, s, rope, l0_norm_mix, l0_attn_w_in, l0_attn_w_out)
    h = channel_mixer(h, l0_norm_ffn, l0_ffn_w_in, l0_ffn_w_out)
    h = s5_layer(h, b, s, l1_norm_mix, l1_s5_a_re, l1_s5_a_im, l1_s5_log_dt, l1_s5_b_re, l1_s5_b_im,
                 l1_s5_c_re, l1_s5_c_im, l1_s5_d, l1_s5_w_glu)
    h = channel_mixer(h, l1_norm_ffn, l1_ffn_w_in, l1_ffn_w_out)
    h = rwkv_layer(h, b, s, l2_norm_mix, l2_rwkv_mu, l2_rwkv_w_r, l2_rwkv_w_k, l2_rwkv_w_v, l2_rwkv_w0,
                   l2_rwkv_w_w1, l2_rwkv_w_w2, l2_rwkv_a0, l2_rwkv_a_w1, l2_rwkv_a_w2, l2_rwkv_g_w1,
                   l2_rwkv_g_w2, l2_rwkv_k_k, l2_rwkv_k_a, l2_rwkv_r_k, l2_rwkv_ln_w, l2_rwkv_ln_b,
                   l2_rwkv_w_o)
    h = channel_mixer(h, l2_norm_ffn, l2_ffn_w_in, l2_ffn_w_out)
    h = attention_layer(h, b, s, rope, l3_norm_mix, l3_attn_w_in, l3_attn_w_out)
    h = channel_mixer(h, l3_norm_ffn, l3_ffn_w_in, l3_ffn_w_out, final_gain=final_norm)
    return h.reshape(b, s, d)
```
